```python
import math
import jax, jax.numpy as jnp
from jax import lax
import numpy as np

D_MODEL = 1024
BATCH = 32
SEQ = 2048
DEPTH = 1

RNN_WIDTH = D_MODEL * 5 // 4
RNN_BLOCKS = 10
RNN_BLOCK = RNN_WIDTH // RNN_BLOCKS
RNN_CONV = 4
LRU_C = 8.0
HEAD_DIM = 128
KV_HEADS = 4
DILATED_CONFIGS = ((128, 1), (512, 4), (2048, 16))
N_GROUPS = len(DILATED_CONFIGS)
Q_HEADS = N_GROUPS * KV_HEADS
ATTN_BLOCK = 128
REL_BUCKETS = 32
REL_MAX_DIST = 2048
FFN_WIDTH = 3 * D_MODEL
FFN_CONV = 3
EPS = 1e-6

IN_SPLITS = (RNN_WIDTH, Q_HEADS * HEAD_DIM, KV_HEADS * HEAD_DIM, KV_HEADS * HEAD_DIM, D_MODEL, D_MODEL)
IN_WIDTH = sum(IN_SPLITS)

kernel_name = "hybrid_rglru_dilated_attn_convffn"


def rms_norm(x, g):
    xf = x.astype(jnp.float32)
    y = xf * lax.rsqrt(jnp.mean(xf * xf, axis=-1, keepdims=True) + EPS)
    return (y * g.astype(jnp.float32)).astype(x.dtype)


def causal_dwconv(x, w, b):
    k = w.shape[0]
    y = lax.conv_general_dilated(x, w[:, None, :].astype(x.dtype), window_strides=(1,),
                                 padding=[(k - 1, 0)], dimension_numbers=("NWC", "WIO", "NWC"),
                                 feature_group_count=x.shape[-1])
    return y + b.astype(x.dtype)


def _t5_bucket(dist):
    max_exact = REL_BUCKETS // 2
    d = np.maximum(dist, 1).astype(np.float32)
    large = max_exact + np.log(d / max_exact) / math.log(REL_MAX_DIST / max_exact) * (REL_BUCKETS - max_exact)
    large = np.minimum(large.astype(np.int32), REL_BUCKETS - 1)
    return np.where(dist < max_exact, dist, large).astype(np.int32)


def _band_structure(n_blocks, dilation, n_back):
    qi = np.arange(ATTN_BLOCK)[None, :, None]
    kj = np.arange(2 * ATTN_BLOCK)[None, None, :]
    nb = np.arange(n_blocks)[:, None, None]
    delta = ATTN_BLOCK + qi - kj
    mask = (delta >= 0) & (delta <= n_back) & ((nb - 1) * ATTN_BLOCK + kj >= 0)
    bucket = _t5_bucket(np.maximum(delta[0], 0) * dilation)
    return mask, bucket


def dilated_group(q, k, v, bias_g, window, dilation):
    B, H, S, hd = q.shape
    r = dilation
    n_back = window // dilation
    M = S // r
    nb = -(-M // ATTN_BLOCK)
    Mp = nb * ATTN_BLOCK

    def to_sub(t):
        return t.reshape(B, H, M, r, hd).transpose(0, 1, 3, 2, 4)

    qs = jnp.pad(to_sub(q), ((0, 0), (0, 0), (0, 0), (0, Mp - M), (0, 0)))
    qs = qs.reshape(B, H, r, nb, ATTN_BLOCK, hd)

    def key_blocks(t):
        ts = jnp.pad(to_sub(t), ((0, 0), (0, 0), (0, 0), (ATTN_BLOCK, Mp - M), (0, 0)))
        ts = ts.reshape(B, H, r, nb + 1, ATTN_BLOCK, hd)
        return jnp.concatenate([ts[:, :, :, :-1], ts[:, :, :, 1:]], axis=4)

    kb = key_blocks(k)
    vb = key_blocks(v)
    mask, bucket = _band_structure(nb, r, n_back)
    bias = bias_g[jnp.asarray(bucket)].astype(jnp.float32).transpose(2, 0, 1)

    logits = jnp.einsum("bhrnqd,bhrnkd->bhrnqk", qs, kb).astype(jnp.float32) * (HEAD_DIM ** -0.5)
    logits = logits + bias[None, :, None, None]
    logits = jnp.where(jnp.asarray(mask)[None, None, None], logits, -jnp.inf)
    mx = jnp.max(logits, axis=-1, keepdims=True)
    p = jnp.exp(logits - mx)
    den = jnp.sum(p, axis=-1, keepdims=True)
    o = jnp.einsum("bhrnqk,bhrnkd->bhrnqd", p, vb.astype(jnp.float32)) / den
    lse = (mx + jnp.log(den))[..., 0]

    o = o.reshape(B, H, r, Mp, hd)[:, :, :, :M].transpose(0, 1, 3, 2, 4).reshape(B, H, S, hd)
    lse = lse.reshape(B, H, r, Mp)[:, :, :, :M].transpose(0, 1, 3, 2).reshape(B, H, S)
    return o, lse


def rg_lru(xc, w_a, b_a, w_x, b_x, lam):
    B, S, C = xc.shape
    xb = xc.reshape(B, S, RNN_BLOCKS, RNN_BLOCK)
    r = jax.nn.sigmoid(jnp.einsum("bsnc,ncd->bsnd", xb, w_a).reshape(B, S, C).astype(jnp.float32)
                       + b_a.astype(jnp.float32))
    i = jax.nn.sigmoid(jnp.einsum("bsnc,ncd->bsnd", xb, w_x).reshape(B, S, C).astype(jnp.float32)
                       + b_x.astype(jnp.float32))
    log_a = -LRU_C * r * jax.nn.softplus(-lam.astype(jnp.float32))
    a = jnp.exp(log_a)
    u = jnp.sqrt(-jnp.expm1(2.0 * log_a)) * (i * xc.astype(jnp.float32))

    def step(h, inp):
        a_t, u_t = inp
        h = a_t * h + u_t
        return h, h

    _, hs = lax.scan(step, jnp.zeros((B, C), jnp.float32), (a.transpose(1, 0, 2), u.transpose(1, 0, 2)))
    return hs.transpose(1, 0, 2)


def setup_inputs(seed: int = 0) -> dict:
    key = jax.random.key(seed)
    ks = jax.random.split(key, 24)
    f32 = jnp.float32
    L, D = DEPTH, D_MODEL

    def nrm(k, shape, scale):
        return jax.random.normal(k, shape, f32) * scale

    u = jax.random.uniform(ks[10], (L, RNN_WIDTH), f32, minval=0.9, maxval=0.999)
    p = u ** (1.0 / LRU_C)
    lru_lambda = jnp.log(p) - jnp.log1p(-p)
    return {
        "x": nrm(ks[0], (BATCH, SEQ, D), 1.0),
        "rel_bias": nrm(ks[1], (REL_BUCKETS, Q_HEADS), 0.1),
        "norm_mix_pre": 1.0 + nrm(ks[2], (L, D), 0.05),
        "norm_mix_post": 1.0 + nrm(ks[3], (L, D), 0.05),
        "w_in": nrm(ks[4], (L, D, IN_WIDTH), D ** -0.5),
        "conv_rnn_w": nrm(ks[5], (L, RNN_CONV, RNN_WIDTH), RNN_CONV ** -0.5),
        "conv_rnn_b": nrm(ks[6], (L, RNN_WIDTH), 0.01),
        "w_rg_a": nrm(ks[7], (L, RNN_BLOCKS, RNN_BLOCK, RNN_BLOCK), RNN_BLOCK ** -0.5),
        "b_rg_a": nrm(ks[8], (L, RNN_WIDTH), 0.01),
        "w_rg_x": nrm(ks[9], (L, RNN_BLOCKS, RNN_BLOCK, RNN_BLOCK), RNN_BLOCK ** -0.5),
        "b_rg_x": nrm(ks[11], (L, RNN_WIDTH), 0.01),
        "lru_lambda": lru_lambda,
        "w_branch_rnn": nrm(ks[12], (L, RNN_WIDTH, D), RNN_WIDTH ** -0.5),
        "w_branch_att": nrm(ks[13], (L, KV_HEADS * HEAD_DIM, D), (KV_HEADS * HEAD_DIM) ** -0.5),
        "w_out": nrm(ks[14], (L, D, D), D ** -0.5),
        "norm_ffn_pre": 1.0 + nrm(ks[15], (L, D), 0.05),
        "norm_ffn_post": 1.0 + nrm(ks[16], (L, D), 0.05),
        "w_ffn_gate": nrm(ks[17], (L, D, FFN_WIDTH), D ** -0.5),
        "w_ffn_up": nrm(ks[18], (L, D, FFN_WIDTH), D ** -0.5),
        "conv_ffn_w": nrm(ks[19], (L, FFN_CONV, FFN_WIDTH), FFN_CONV ** -0.5),
        "conv_ffn_b": nrm(ks[20], (L, FFN_WIDTH), 0.01),
        "w_ffn_down": nrm(ks[21], (L, FFN_WIDTH, D), FFN_WIDTH ** -0.5),
    }


def reference(x, rel_bias, norm_mix_pre, norm_mix_post, w_in, conv_rnn_w, conv_rnn_b, w_rg_a, b_rg_a,
              w_rg_x, b_rg_x, lru_lambda, w_branch_rnn, w_branch_att, w_out, norm_ffn_pre, norm_ffn_post,
              w_ffn_gate, w_ffn_up, conv_ffn_w, conv_ffn_b, w_ffn_down):
    B, S, D = x.shape
    split_idx = [int(s) for s in np.cumsum(IN_SPLITS)[:-1]]
    h = x
    for l in range(DEPTH):
        hn = rms_norm(h, norm_mix_pre[l])
        proj = hn @ w_in[l]
        xr, q, k, v, g_rnn, g_att = jnp.split(proj, split_idx, axis=-1)

        xc = causal_dwconv(xr, conv_rnn_w[l], conv_rnn_b[l])
        y_rnn = rg_lru(xc, w_rg_a[l], b_rg_a[l], w_rg_x[l], b_rg_x[l], lru_lambda[l]).astype(x.dtype)

        qg = q.reshape(B, S, N_GROUPS, KV_HEADS, HEAD_DIM).transpose(2, 0, 3, 1, 4)
        kh = k.reshape(B, S, KV_HEADS, HEAD_DIM).transpose(0, 2, 1, 3)
        vh = v.reshape(B, S, KV_HEADS, HEAD_DIM).transpose(0, 2, 1, 3)
        outs, lses = [], []
        for g, (window, dilation) in enumerate(DILATED_CONFIGS):
            o_g, lse_g = dilated_group(qg[g], kh, vh, rel_bias[:, g * KV_HEADS:(g + 1) * KV_HEADS],
                                       window, dilation)
            outs.append(o_g)
            lses.append(lse_g)
        alpha = jax.nn.softmax(jnp.stack(lses, axis=0), axis=0)
        o_att = jnp.sum(alpha[..., None] * jnp.stack(outs, axis=0), axis=0)
        o_att = o_att.transpose(0, 2, 1, 3).reshape(B, S, KV_HEADS * HEAD_DIM).astype(x.dtype)

        merged = (jax.nn.sigmoid(g_rnn) * (y_rnn @ w_branch_rnn[l])
                  + jax.nn.sigmoid(g_att) * (o_att @ w_branch_att[l]))
        mix = merged @ w_out[l]
        h = h + rms_norm(mix, norm_mix_post[l])

        hn = rms_norm(h, norm_ffn_pre[l])
        gate = causal_dwconv(hn @ w_ffn_gate[l], conv_ffn_w[l], conv_ffn_b[l])
        ff = (jax.nn.gelu(gate, approximate=True) * (hn @ w_ffn_up[l])) @ w_ffn_down[l]
        h = h + rms_norm(ff, norm_ffn_post[l])
    return h
```

```python
import functools
import math

import numpy as np
import jax
import jax.numpy as jnp
from jax import lax
from jax.experimental import pallas as pl
from jax.experimental.pallas import tpu as pltpu

F32 = jnp.float32
BF16 = jnp.bfloat16

D_MODEL = 1024
RNN_WIDTH = 1280
RNN_BLOCKS = 10
RNN_BLOCK = 128
RNN_CONV = 4
LRU_C = 8.0
HEAD_DIM = 128
KV_HEADS = 4
DILATED_CONFIGS = ((128, 1), (512, 4), (2048, 16))
N_GROUPS = 3
Q_HEADS = 12
ATTN_BLOCK = 128
REL_BUCKETS = 32
REL_MAX_DIST = 2048
FFN_WIDTH = 3072
FFN_CONV = 3
EPS = 1e-6
QKV_WIDTH = (Q_HEADS + 2 * KV_HEADS) * HEAD_DIM
GATE_WIDTH = 2 * D_MODEL
IN_WIDTH = RNN_WIDTH + QKV_WIDTH + GATE_WIDTH

SUBLANES = 8
LANES = 128
VMEM_LIMIT_BYTES = 56 * 1024 * 1024

MASK_VALUE = -1e30

IN_TM = 256
RG_TS = 128
RG_NB = 8
RG_PITCH = RG_TS + 8
MG_TM = 512
FF_TM = 1024
FF_TF = 512


def _rms_norm(x, g):
    ms = jnp.mean(x * x, axis=-1, keepdims=True)
    return x * lax.rsqrt(ms + EPS) * g


def _sigmoid(x):
    return 0.5 + 0.5 * jnp.tanh(0.5 * x)


def _t5_bucket(dist):
    max_exact = REL_BUCKETS // 2
    d = np.maximum(dist, 1).astype(np.float32)
    large = max_exact + np.log(d / max_exact) / math.log(REL_MAX_DIST / max_exact) * (REL_BUCKETS - max_exact)
    large = np.minimum(large.astype(np.int32), REL_BUCKETS - 1)
    return np.where(dist < max_exact, dist, large).astype(np.int32)


def _bucket_maps():
    qi = np.arange(ATTN_BLOCK)[:, None]
    kj = np.arange(2 * ATTN_BLOCK)[None, :]
    delta = ATTN_BLOCK + qi - kj
    maps = []
    for window, dilation in DILATED_CONFIGS:
        n_back = window // dilation
        valid = (delta >= 0) & (delta <= n_back)
        bucket = _t5_bucket(np.maximum(delta, 0) * dilation)
        maps.append(np.where(valid, bucket, -1).astype(np.int32))
    return np.stack(maps, axis=0)


def _bias_kernel(rb_ref, bucket_ref, out_ref):
    head = pl.program_id(0)
    bk = bucket_ref[0]
    acc = jnp.full(bk.shape, MASK_VALUE, F32)
    for b in range(REL_BUCKETS):
        acc = jnp.where(bk == b, rb_ref[b, head], acc)
    out_ref[0] = acc


def _bias_tables(rel_bias):
    buckets = jnp.asarray(_bucket_maps())
    return pl.pallas_call(
        _bias_kernel,
        grid=(Q_HEADS,),
        in_specs=[
            pl.BlockSpec(memory_space=pltpu.SMEM),
            pl.BlockSpec((1, ATTN_BLOCK, 2 * ATTN_BLOCK), lambda h: (h // KV_HEADS, 0, 0)),
        ],
        out_specs=pl.BlockSpec((1, ATTN_BLOCK, 2 * ATTN_BLOCK), lambda h: (h, 0, 0)),
        out_shape=jax.ShapeDtypeStruct((Q_HEADS, ATTN_BLOCK, 2 * ATTN_BLOCK), F32),
        name="bias_table",
    )(rel_bias, buckets)


def _in_proj_kernel(x_ref, g_ref, w_ref, xr_ref, qkv_ref, gate_ref):
    hn = _rms_norm(x_ref[...], g_ref[...]).astype(BF16)
    c0, c1 = RNN_WIDTH, RNN_WIDTH + QKV_WIDTH
    xr_ref[...] = jnp.dot(hn, w_ref[:, 0:c0], preferred_element_type=F32)
    qkv_ref[...] = jnp.dot(hn, w_ref[:, c0:c1], preferred_element_type=F32)
    gate_ref[...] = jnp.dot(hn, w_ref[:, c1:IN_WIDTH], preferred_element_type=F32).astype(BF16)


def _in_proj(x2, g, w_bf):
    T = x2.shape[0]
    const = lambda i: (0, 0)
    row = lambda i: (i, 0)
    return pl.pallas_call(
        _in_proj_kernel,
        grid=(T // IN_TM,),
        in_specs=[
            pl.BlockSpec((IN_TM, D_MODEL), row),
            pl.BlockSpec((1, D_MODEL), const),
            pl.BlockSpec((D_MODEL, IN_WIDTH), const, pipeline_mode=pl.Buffered(1)),
        ],
        out_specs=[
            pl.BlockSpec((IN_TM, RNN_WIDTH), row),
            pl.BlockSpec((IN_TM, QKV_WIDTH), row),
            pl.BlockSpec((IN_TM, GATE_WIDTH), row),
        ],
        out_shape=[
            jax.ShapeDtypeStruct((T, RNN_WIDTH), F32),
            jax.ShapeDtypeStruct((T, QKV_WIDTH), F32),
            jax.ShapeDtypeStruct((T, GATE_WIDTH), BF16),
        ],
        compiler_params=pltpu.CompilerParams(
            dimension_semantics=("parallel",), vmem_limit_bytes=VMEM_LIMIT_BYTES),
        name="in_proj",
    )(x2, g, w_bf)


def _rglru_kernel(xr_ref, cw_ref, cb_ref, wax_ref, bax_ref, lam_ref, y_ref,
                  xbuf, a_scr, u_scr, y_scr, h_scr):
    s = pl.program_id(1)
    ts, pitch = RG_TS, RG_PITCH

    @pl.when(s == 0)
    def _():
        xbuf[:, 0:SUBLANES, :] = jnp.zeros((RG_NB, SUBLANES, RNN_WIDTH), F32)
        h_scr[...] = jnp.zeros_like(h_scr)

    xbuf[:, SUBLANES:SUBLANES + ts, :] = xr_ref[...]

    for n in range(RNN_BLOCKS):
        cols = slice(n * RNN_BLOCK, (n + 1) * RNN_BLOCK)
        xc = cb_ref[:, cols][None]
        for k in range(RNN_CONV):
            back = RNN_CONV - 1 - k
            xc = xc + cw_ref[k:k + 1, cols][None] * xbuf[:, SUBLANES - back:SUBLANES - back + ts, cols]
        xc2 = xc.reshape(RG_NB * ts, RNN_BLOCK)
        g = jnp.dot(xc2.astype(BF16), wax_ref[n], preferred_element_type=F32)
        r = _sigmoid(g[:, 0:RNN_BLOCK] + bax_ref[0:1, cols])
        i = _sigmoid(g[:, RNN_BLOCK:2 * RNN_BLOCK] + bax_ref[1:2, cols])
        neg_lam = -lam_ref[:, cols]
        softplus = jnp.maximum(neg_lam, 0.0) + jnp.log1p(jnp.exp(-jnp.abs(neg_lam)))
        log_a = (-LRU_C * softplus) * r
        a = jnp.exp(log_a)
        th = jnp.tanh(log_a)
        u = jnp.sqrt(-2.0 * th / (1.0 - th)) * (i * xc2)
        for b in range(RG_NB):
            a_scr[n, b * pitch:b * pitch + ts, :] = a[b * ts:(b + 1) * ts]
            u_scr[n, b * pitch:b * pitch + ts, :] = u[b * ts:(b + 1) * ts]

    xbuf[:, 0:SUBLANES, :] = xr_ref[:, ts - SUBLANES:ts, :]

    def step(t, hs):
        new = []
        for n in range(RNN_BLOCKS):
            rows = pl.ds(t, RG_NB, stride=pitch)
            h = a_scr[n, rows, :] * hs[n] + u_scr[n, rows, :]
            y_scr[n, rows, :] = h
            new.append(h)
        return tuple(new)

    hs = lax.fori_loop(0, ts, step, tuple(h_scr[n] for n in range(RNN_BLOCKS)), unroll=8)
    for n in range(RNN_BLOCKS):
        h_scr[n] = hs[n]
        for b in range(RG_NB):
            y_ref[b, :, n * RNN_BLOCK:(n + 1) * RNN_BLOCK] = (
                y_scr[n, b * pitch:b * pitch + ts, :].astype(BF16))


def _rglru(xr3, conv_w, conv_b, wax, bax, lam):
    B, S, _ = xr3.shape
    const2 = lambda b, s: (0, 0)
    scan_scratch = pltpu.VMEM((RNN_BLOCKS, RG_NB * RG_PITCH, RNN_BLOCK), F32)
    return pl.pallas_call(
        _rglru_kernel,
        grid=(B // RG_NB, S // RG_TS),
        in_specs=[
            pl.BlockSpec((RG_NB, RG_TS, RNN_WIDTH), lambda b, s: (b, s, 0)),
            pl.BlockSpec((RNN_CONV, RNN_WIDTH), const2),
            pl.BlockSpec((1, RNN_WIDTH), const2),
            pl.BlockSpec((RNN_BLOCKS, RNN_BLOCK, 2 * RNN_BLOCK), lambda b, s: (0, 0, 0)),
            pl.BlockSpec((2, RNN_WIDTH), const2),
            pl.BlockSpec((1, RNN_WIDTH), const2),
        ],
        out_specs=pl.BlockSpec((RG_NB, RG_TS, RNN_WIDTH), lambda b, s: (b, s, 0)),
        out_shape=jax.ShapeDtypeStruct((B, S, RNN_WIDTH), BF16),
        scratch_shapes=[
            pltpu.VMEM((RG_NB, SUBLANES + RG_TS, RNN_WIDTH), F32),
            scan_scratch, scan_scratch, scan_scratch,
            pltpu.VMEM((RNN_BLOCKS, RG_NB, RNN_BLOCK), F32),
        ],
        compiler_params=pltpu.CompilerParams(
            dimension_semantics=("parallel", "arbitrary"), vmem_limit_bytes=VMEM_LIMIT_BYTES),
        name="rglru",
    )(xr3, conv_w, conv_b, wax, bax, lam)


def _attn_block(q, k, v, bias):
    s = lax.dot_general(q, k, (((1,), (1,)), ((), ())), preferred_element_type=F32)
    s = s * (HEAD_DIM ** -0.5) + bias
    m = jnp.max(s, axis=-1, keepdims=True)
    p = jnp.exp(s - m)
    den = jnp.sum(p, axis=-1, keepdims=True)
    o = jnp.dot(p.astype(BF16), v, preferred_element_type=F32) / den
    return o, m + jnp.log(den)


def _dilated_kernel(q1_ref, q2_ref, q3_ref, k_ref, v_ref, b1_ref, b2_ref, b3_ref, o_ref,
                    o_scr, l_scr):
    S = k_ref.shape[0]
    blk = ATTN_BLOCK

    def run_group(g, q_ref, b_ref, dil):
        m_len = S // dil
        nb = m_len // blk

        def rows(c, n):
            if dil == 1:
                return pl.ds(pl.multiple_of(n * blk, blk), blk)
            return pl.ds(n * (blk * dil) + c, blk, stride=dil)

        def one(c, n, has_prev):
            q = q_ref[rows(c, n), :].astype(BF16)
            k = k_ref[rows(c, n), :].astype(BF16)
            v = v_ref[rows(c, n), :].astype(BF16)
            if has_prev:
                k = jnp.concatenate([k_ref[rows(c, n - 1), :].astype(BF16), k], axis=0)
                v = jnp.concatenate([v_ref[rows(c, n - 1), :].astype(BF16), v], axis=0)
                bias = b_ref[...]
            else:
                bias = b_ref[:, blk:2 * blk]
            o, lse = _attn_block(q, k, v, bias)
            o_scr[g, rows(c, n), :] = o
            l_scr[g, rows(c, n), :] = jnp.broadcast_to(lse, (blk, HEAD_DIM))

        def per_class(c, carry):
            one(c, 0, False)
            if nb > 1:
                def inner(n, carry2):
                    one(c, n, True)
                    return carry2
                lax.fori_loop(1, nb, inner, 0)
            return carry

        lax.fori_loop(0, dil, per_class, 0)

    run_group(0, q1_ref, b1_ref, DILATED_CONFIGS[0][1])
    run_group(1, q2_ref, b2_ref, DILATED_CONFIGS[1][1])
    run_group(2, q3_ref, b3_ref, DILATED_CONFIGS[2][1])

    chunk = 256

    def merge(i, carry):
        r = pl.ds(pl.multiple_of(i * chunk, chunk), chunk)
        l0, l1, l2 = l_scr[0, r, :], l_scr[1, r, :], l_scr[2, r, :]
        m = jnp.maximum(jnp.maximum(l0, l1), l2)
        w0, w1, w2 = jnp.exp(l0 - m), jnp.exp(l1 - m), jnp.exp(l2 - m)
        num = w0 * o_scr[0, r, :] + w1 * o_scr[1, r, :] + w2 * o_scr[2, r, :]
        o_ref[r, :] = (num / (w0 + w1 + w2)).astype(o_ref.dtype)
        return carry

    lax.fori_loop(0, S // chunk, merge, 0)


def _dilated_attention(qkv3, bias_tab):
    B, S, _ = qkv3.shape
    head_spec = lambda col0: pl.BlockSpec((None, S, HEAD_DIM), lambda b, h: (b, 0, col0 + h))
    bias_spec = lambda g: pl.BlockSpec(
        (None, ATTN_BLOCK, 2 * ATTN_BLOCK), lambda b, h: (g * KV_HEADS + h, 0, 0))
    return pl.pallas_call(
        _dilated_kernel,
        grid=(B, KV_HEADS),
        in_specs=[
            head_spec(0), head_spec(KV_HEADS), head_spec(2 * KV_HEADS),
            head_spec(Q_HEADS), head_spec(Q_HEADS + KV_HEADS),
            bias_spec(0), bias_spec(1), bias_spec(2),
        ],
        out_specs=pl.BlockSpec((None, S, HEAD_DIM), lambda b, h: (b, 0, h)),
        out_shape=jax.ShapeDtypeStruct((B, S, KV_HEADS * HEAD_DIM), BF16),
        scratch_shapes=[
            pltpu.VMEM((N_GROUPS, S, HEAD_DIM), F32),
            pltpu.VMEM((N_GROUPS, S, HEAD_DIM), F32),
        ],
        compiler_params=pltpu.CompilerParams(
            dimension_semantics=("parallel", "parallel"), vmem_limit_bytes=VMEM_LIMIT_BYTES),
        name="dilated_attn",
    )(qkv3, qkv3, qkv3, qkv3, qkv3, bias_tab, bias_tab, bias_tab)


def _merge_kernel(y_ref, o_ref, gate_ref, x_ref, wbr_ref, wba_ref, wout_ref, g_ref, h_ref):
    br = jnp.dot(y_ref[...], wbr_ref[...], preferred_element_type=F32)
    ba = jnp.dot(o_ref[...], wba_ref[...], preferred_element_type=F32)
    g_rnn = _sigmoid(gate_ref[:, 0:D_MODEL].astype(F32))
    g_att = _sigmoid(gate_ref[:, D_MODEL:GATE_WIDTH].astype(F32))
    merged = g_rnn * br + g_att * ba
    mix = jnp.dot(merged.astype(BF16), wout_ref[...], preferred_element_type=F32)
    h_ref[...] = x_ref[...] + _rms_norm(mix, g_ref[...])


def _merge(y2, o2, gates, x2, wbr, wba, wout, g_post):
    T = x2.shape[0]
    const = lambda i: (0, 0)
    row = lambda i: (i, 0)
    return pl.pallas_call(
        _merge_kernel,
        grid=(T // MG_TM,),
        in_specs=[
            pl.BlockSpec((MG_TM, RNN_WIDTH), row),
            pl.BlockSpec((MG_TM, KV_HEADS * HEAD_DIM), row),
            pl.BlockSpec((MG_TM, GATE_WIDTH), row),
            pl.BlockSpec((MG_TM, D_MODEL), row),
            pl.BlockSpec((RNN_WIDTH, D_MODEL), const),
            pl.BlockSpec((KV_HEADS * HEAD_DIM, D_MODEL), const),
            pl.BlockSpec((D_MODEL, D_MODEL), const),
            pl.BlockSpec((1, D_MODEL), const),
        ],
        out_specs=pl.BlockSpec((MG_TM, D_MODEL), row),
        out_shape=jax.ShapeDtypeStruct((T, D_MODEL), F32),
        compiler_params=pltpu.CompilerParams(
            dimension_semantics=("parallel",), vmem_limit_bytes=VMEM_LIMIT_BYTES),
        name="merge_out",
    )(y2, o2, gates, x2, wbr, wba, wout, g_post)


def _gelu_tanh(x):
    return 0.5 * x * (1.0 + jnp.tanh(math.sqrt(2.0 / math.pi) * (x + 0.044715 * (x * x * x))))


def _ffn_kernel(h_ref, gpre_ref, wg_ref, wu_ref, cw_ref, cb_ref, wd_ref, gpost_ref, out_ref,
                hn_scr, acc_scr, gbuf, carry_scr, *, tiles_per_seq):
    i = pl.program_id(0)
    j = pl.program_id(1)
    tm = FF_TM

    @pl.when(j == 0)
    def _():
        hn_scr[...] = _rms_norm(h_ref[...], gpre_ref[...]).astype(BF16)
        acc_scr[...] = jnp.zeros_like(acc_scr)

    hn = hn_scr[...]
    gate = jnp.dot(hn, wg_ref[...], preferred_element_type=F32)
    up = jnp.dot(hn, wu_ref[...], preferred_element_type=F32)

    seq_start = (i % tiles_per_seq) == 0
    gbuf[0:SUBLANES, :] = jnp.where(seq_start, 0.0, carry_scr[j])
    gbuf[SUBLANES:SUBLANES + tm, :] = gate
    carry_scr[j] = gate[tm - SUBLANES:tm, :]

    conv = cb_ref[...] + cw_ref[FFN_CONV - 1:FFN_CONV, :] * gate
    for k in range(FFN_CONV - 1):
        back = FFN_CONV - 1 - k
        conv = conv + cw_ref[k:k + 1, :] * gbuf[SUBLANES - back:SUBLANES - back + tm, :]
    act = (_gelu_tanh(conv) * up).astype(BF16)
    acc_scr[...] += jnp.dot(act, wd_ref[...], preferred_element_type=F32)

    @pl.when(j == pl.num_programs(1) - 1)
    def _():
        out_ref[...] = h_ref[...] + _rms_norm(acc_scr[...], gpost_ref[...])


def _ffn(h2, g_pre, wg, wu, conv_w, conv_b, wd, g_post, seq_len):
    T = h2.shape[0]
    n_j = FFN_WIDTH // FF_TF
    return pl.pallas_call(
        functools.partial(_ffn_kernel, tiles_per_seq=seq_len // FF_TM),
        grid=(T // FF_TM, n_j),
        in_specs=[
            pl.BlockSpec((FF_TM, D_MODEL), lambda i, j: (i, 0)),
            pl.BlockSpec((1, D_MODEL), lambda i, j: (0, 0)),
            pl.BlockSpec((D_MODEL, FF_TF), lambda i, j: (0, j)),
            pl.BlockSpec((D_MODEL, FF_TF), lambda i, j: (0, j)),
            pl.BlockSpec((FFN_CONV, FF_TF), lambda i, j: (0, j)),
            pl.BlockSpec((1, FF_TF), lambda i, j: (0, j)),
            pl.BlockSpec((FF_TF, D_MODEL), lambda i, j: (j, 0)),
            pl.BlockSpec((1, D_MODEL), lambda i, j: (0, 0)),
        ],
        out_specs=pl.BlockSpec((FF_TM, D_MODEL), lambda i, j: (i, 0)),
        out_shape=jax.ShapeDtypeStruct((T, D_MODEL), F32),
        scratch_shapes=[
            pltpu.VMEM((FF_TM, D_MODEL), BF16),
            pltpu.VMEM((FF_TM, D_MODEL), F32),
            pltpu.VMEM((SUBLANES + FF_TM, FF_TF), F32),
            pltpu.VMEM((n_j, SUBLANES, FF_TF), F32),
        ],
        compiler_params=pltpu.CompilerParams(
            dimension_semantics=("arbitrary", "arbitrary"), vmem_limit_bytes=VMEM_LIMIT_BYTES),
        name="ffn",
    )(h2, g_pre, wg, wu, conv_w, conv_b, wd, g_post)


def kernel(x, rel_bias, norm_mix_pre, norm_mix_post, w_in, conv_rnn_w, conv_rnn_b, w_rg_a, b_rg_a,
           w_rg_x, b_rg_x, lru_lambda, w_branch_rnn, w_branch_att, w_out, norm_ffn_pre, norm_ffn_post,
           w_ffn_gate, w_ffn_up, conv_ffn_w, conv_ffn_b, w_ffn_down):
    B, S, D = x.shape
    assert D == D_MODEL and S % FF_TM == 0 and S % RG_TS == 0 and B % RG_NB == 0
    assert S == DILATED_CONFIGS[-1][0], "attention block structure assumes window == sequence for the widest group"
    depth = w_in.shape[0]
    T = B * S
    bias_tab = _bias_tables(rel_bias)
    h = x.reshape(T, D)
    for l in range(depth):
        xr, qkv, gates = _in_proj(h, norm_mix_pre[l][None], w_in[l].astype(BF16))
        wax = jnp.concatenate([w_rg_a[l], w_rg_x[l]], axis=-1).astype(BF16)
        bax = jnp.stack([b_rg_a[l], b_rg_x[l]], axis=0)
        y_rnn = _rglru(xr.reshape(B, S, RNN_WIDTH), conv_rnn_w[l], conv_rnn_b[l][None], wax, bax,
                       lru_lambda[l][None])
        o_att = _dilated_attention(qkv.reshape(B, S, QKV_WIDTH), bias_tab)
        h = _merge(y_rnn.reshape(T, RNN_WIDTH), o_att.reshape(T, KV_HEADS * HEAD_DIM), gates, h,
                   w_branch_rnn[l].astype(BF16), w_branch_att[l].astype(BF16), w_out[l].astype(BF16),
                   norm_mix_post[l][None])
        h = _ffn(h, norm_ffn_pre[l][None], w_ffn_gate[l].astype(BF16), w_ffn_up[l].astype(BF16),
                 conv_ffn_w[l], conv_ffn_b[l][None], w_ffn_down[l].astype(BF16), norm_ffn_post[l][None],
                 S)
    return h.reshape(B, S, D)
```

```python
import functools
import math

import numpy as np
import jax
import jax.numpy as jnp
from jax import lax
from jax.experimental import pallas as pl
from jax.experimental.pallas import tpu as pltpu

F32 = jnp.float32
BF16 = jnp.bfloat16

D_MODEL = 1024
RNN_WIDTH = 1280
RNN_BLOCKS = 10
RNN_BLOCK = 128
RNN_CONV = 4
LRU_C = 8.0
HEAD_DIM = 128
KV_HEADS = 4
DILATED_CONFIGS = ((128, 1), (512, 4), (2048, 16))
N_GROUPS = 3
Q_HEADS = 12
ATTN_BLOCK = 128
REL_BUCKETS = 32
REL_MAX_DIST = 2048
FFN_WIDTH = 3072
FFN_CONV = 3
EPS = 1e-6
QKV_WIDTH = (Q_HEADS + 2 * KV_HEADS) * HEAD_DIM
GATE_WIDTH = 2 * D_MODEL
IN_WIDTH = RNN_WIDTH + QKV_WIDTH + GATE_WIDTH

SUBLANES = 8
LANES = 128
VMEM_LIMIT_BYTES = 56 * 1024 * 1024

MASK_VALUE = -1e30

IN_TM = 256
RG_TS = 128
RG_NB = 8
RG_PITCH = RG_TS + 8
MG_TM = 512
FF_TM = 1024
FF_TF = 512


def _rms_norm(x, g):
    ms = jnp.mean(x * x, axis=-1, keepdims=True)
    return x * lax.rsqrt(ms + EPS) * g


def _sigmoid(x):
    return 0.5 + 0.5 * jnp.tanh(0.5 * x)


def _t5_bucket(dist):
    max_exact = REL_BUCKETS // 2
    d = np.maximum(dist, 1).astype(np.float32)
    large = max_exact + np.log(d / max_exact) / math.log(REL_MAX_DIST / max_exact) * (REL_BUCKETS - max_exact)
    large = np.minimum(large.astype(np.int32), REL_BUCKETS - 1)
    return np.where(dist < max_exact, dist, large).astype(np.int32)


def _bucket_maps():
    qi = np.arange(ATTN_BLOCK)[:, None]
    kj = np.arange(2 * ATTN_BLOCK)[None, :]
    delta = ATTN_BLOCK + qi - kj
    maps = []
    for window, dilation in DILATED_CONFIGS:
        n_back = window // dilation
        valid = (delta >= 0) & (delta <= n_back)
        bucket = _t5_bucket(np.maximum(delta, 0) * dilation)
        maps.append(np.where(valid, bucket, -1).astype(np.int32))
    return np.stack(maps, axis=0)


def _bias_kernel(rb_ref, bucket_ref, out_ref):
    head = pl.program_id(0)
    bk = bucket_ref[0]
    acc = jnp.full(bk.shape, MASK_VALUE, F32)
    for b in range(REL_BUCKETS):
        acc = jnp.where(bk == b, rb_ref[b, head], acc)
    out_ref[0] = acc


def _bias_tables(rel_bias):
    buckets = jnp.asarray(_bucket_maps())
    return pl.pallas_call(
        _bias_kernel,
        grid=(Q_HEADS,),
        in_specs=[
            pl.BlockSpec(memory_space=pltpu.SMEM),
            pl.BlockSpec((1, ATTN_BLOCK, 2 * ATTN_BLOCK), lambda h: (h // KV_HEADS, 0, 0)),
        ],
        out_specs=pl.BlockSpec((1, ATTN_BLOCK, 2 * ATTN_BLOCK), lambda h: (h, 0, 0)),
        out_shape=jax.ShapeDtypeStruct((Q_HEADS, ATTN_BLOCK, 2 * ATTN_BLOCK), F32),
        name="bias_table",
    )(rel_bias, buckets)


def _in_proj_kernel(x_ref, g_ref, w_ref, xr_ref, qkv_ref, gate_ref):
    hn = _rms_norm(x_ref[...], g_ref[...]).astype(BF16)
    c0, c1 = RNN_WIDTH, RNN_WIDTH + QKV_WIDTH
    xr_ref[...] = jnp.dot(hn, w_ref[:, 0:c0], preferred_element_type=F32)
    qkv_ref[...] = jnp.dot(hn, w_ref[:, c0:c1], preferred_element_type=F32)
    gate_ref[...] = jnp.dot(hn, w_ref[:, c1:IN_WIDTH], preferred_element_type=F32).astype(BF16)


def _in_proj(x2, g, w_bf):
    T = x2.shape[0]
    const = lambda i: (0, 0)
    row = lambda i: (i, 0)
    return pl.pallas_call(
        _in_proj_kernel,
        grid=(T // IN_TM,),
        in_specs=[
            pl.BlockSpec((IN_TM, D_MODEL), row),
            pl.BlockSpec((1, D_MODEL), const),
            pl.BlockSpec((D_MODEL, IN_WIDTH), const, pipeline_mode=pl.Buffered(1)),
        ],
        out_specs=[
            pl.BlockSpec((IN_TM, RNN_WIDTH), row),
            pl.BlockSpec((IN_TM, QKV_WIDTH), row),
            pl.BlockSpec((IN_TM, GATE_WIDTH), row),
        ],
        out_shape=[
            jax.ShapeDtypeStruct((T, RNN_WIDTH), F32),
            jax.ShapeDtypeStruct((T, QKV_WIDTH), F32),
            jax.ShapeDtypeStruct((T, GATE_WIDTH), BF16),
        ],
        compiler_params=pltpu.CompilerParams(
            dimension_semantics=("parallel",), vmem_limit_bytes=VMEM_LIMIT_BYTES),
        name="in_proj",
    )(x2, g, w_bf)


def _rglru_kernel(xr_ref, cw_ref, cb_ref, wax_ref, bax_ref, lam_ref, y_ref,
                  xbuf, a_scr, u_scr, y_scr, h_scr):
    s = pl.program_id(1)
    ts, pitch = RG_TS, RG_PITCH

    @pl.when(s == 0)
    def _():
        xbuf[:, 0:SUBLANES, :] = jnp.zeros((RG_NB, SUBLANES, RNN_WIDTH), F32)
        h_scr[...] = jnp.zeros_like(h_scr)

    xbuf[:, SUBLANES:SUBLANES + ts, :] = xr_ref[...]

    for n in range(RNN_BLOCKS):
        cols = slice(n * RNN_BLOCK, (n + 1) * RNN_BLOCK)
        xc = cb_ref[:, cols][None]
        for k in range(RNN_CONV):
            back = RNN_CONV - 1 - k
            xc = xc + cw_ref[k:k + 1, cols][None] * xbuf[:, SUBLANES - back:SUBLANES - back + ts, cols]
        xc2 = xc.reshape(RG_NB * ts, RNN_BLOCK)
        g = jnp.dot(xc2.astype(BF16), wax_ref[n], preferred_element_type=F32)
        r = _sigmoid(g[:, 0:RNN_BLOCK] + bax_ref[0:1, cols])
        i = _sigmoid(g[:, RNN_BLOCK:2 * RNN_BLOCK] + bax_ref[1:2, cols])
        neg_lam = -lam_ref[:, cols]
        softplus = jnp.maximum(neg_lam, 0.0) + jnp.log1p(jnp.exp(-jnp.abs(neg_lam)))
        log_a = (-LRU_C * softplus) * r
        a = jnp.exp(log_a)
        th = jnp.tanh(log_a)
        u = jnp.sqrt(-2.0 * th / (1.0 - th)) * (i * xc2)
        for b in range(RG_NB):
            a_scr[n, b * pitch:b * pitch + ts, :] = a[b * ts:(b + 1) * ts]
            u_scr[n, b * pitch:b * pitch + ts, :] = u[b * ts:(b + 1) * ts]

    xbuf[:, 0:SUBLANES, :] = xr_ref[:, ts - SUBLANES:ts, :]

    def step(t, hs):
        new = []
        for n in range(RNN_BLOCKS):
            rows = pl.ds(t, RG_NB, stride=pitch)
            h = a_scr[n, rows, :] * hs[n] + u_scr[n, rows, :]
            y_scr[n, rows, :] = h
            new.append(h)
        return tuple(new)

    hs = lax.fori_loop(0, ts, step, tuple(h_scr[n] for n in range(RNN_BLOCKS)), unroll=8)
    for n in range(RNN_BLOCKS):
        h_scr[n] = hs[n]
        for b in range(RG_NB):
            y_ref[b, :, n * RNN_BLOCK:(n + 1) * RNN_BLOCK] = (
                y_scr[n, b * pitch:b * pitch + ts, :].astype(BF16))


def _rglru(xr3, conv_w, conv_b, wax, bax, lam):
    B, S, _ = xr3.shape
    const2 = lambda b, s: (0, 0)
    scan_scratch = pltpu.VMEM((RNN_BLOCKS, RG_NB * RG_PITCH, RNN_BLOCK), F32)
    return pl.pallas_call(
        _rglru_kernel,
        grid=(B // RG_NB, S // RG_TS),
        in_specs=[
            pl.BlockSpec((RG_NB, RG_TS, RNN_WIDTH), lambda b, s: (b, s, 0)),
            pl.BlockSpec((RNN_CONV, RNN_WIDTH), const2),
            pl.BlockSpec((1, RNN_WIDTH), const2),
            pl.BlockSpec((RNN_BLOCKS, RNN_BLOCK, 2 * RNN_BLOCK), lambda b, s: (0, 0, 0)),
            pl.BlockSpec((2, RNN_WIDTH), const2),
            pl.BlockSpec((1, RNN_WIDTH), const2),
        ],
        out_specs=pl.BlockSpec((RG_NB, RG_TS, RNN_WIDTH), lambda b, s: (b, s, 0)),
        out_shape=jax.ShapeDtypeStruct((B, S, RNN_WIDTH), BF16),
        scratch_shapes=[
            pltpu.VMEM((RG_NB, SUBLANES + RG_TS, RNN_WIDTH), F32),
            scan_scratch, scan_scratch, scan_scratch,
            pltpu.VMEM((RNN_BLOCKS, RG_NB, RNN_BLOCK), F32),
        ],
        compiler_params=pltpu.CompilerParams(
            dimension_semantics=("parallel", "arbitrary"), vmem_limit_bytes=VMEM_LIMIT_BYTES),
        name="rglru",
    )(xr3, conv_w, conv_b, wax, bax, lam)


def _attn_block(q, k, v, bias):
    s = lax.dot_general(q, k, (((1,), (1,)), ((), ())), preferred_element_type=F32)
    s = s * (HEAD_DIM ** -0.5) + bias
    m = jnp.max(s, axis=-1, keepdims=True)
    p = jnp.exp(s - m)
    den = jnp.sum(p, axis=-1, keepdims=True)
    o = jnp.dot(p.astype(BF16), v, preferred_element_type=F32)
    return o, m, den


def _dilated_kernel(q1_ref, q2_ref, q3_ref, k_ref, v_ref, b1_ref, b2_ref, b3_ref, o_ref,
                    o_scr, m_scr, d_scr):
    S = k_ref.shape[0]
    blk = ATTN_BLOCK

    def run_group(g, q_ref, b_ref, dil):
        nb = S // dil // blk

        def rows(c, n):
            start = n * (blk * dil) + c
            return pl.ds(start, blk) if dil == 1 else pl.ds(start, blk, stride=dil)

        for c in range(dil):
            k_prev = v_prev = None
            for n in range(nb):
                r = rows(c, n)
                q = q_ref[r, :].astype(BF16)
                k_cur = k_ref[r, :].astype(BF16)
                v_cur = v_ref[r, :].astype(BF16)
                if n == 0:
                    o, m, den = _attn_block(q, k_cur, v_cur, b_ref[:, blk:2 * blk])
                else:
                    k2 = jnp.concatenate([k_prev, k_cur], axis=0)
                    v2 = jnp.concatenate([v_prev, v_cur], axis=0)
                    o, m, den = _attn_block(q, k2, v2, b_ref[...])
                k_prev, v_prev = k_cur, v_cur
                o_scr[g, r, :] = o
                m_scr[g, r, :] = jnp.broadcast_to(m, (blk, HEAD_DIM))
                d_scr[g, r, :] = jnp.broadcast_to(den, (blk, HEAD_DIM))

    run_group(0, q1_ref, b1_ref, DILATED_CONFIGS[0][1])
    run_group(1, q2_ref, b2_ref, DILATED_CONFIGS[1][1])
    run_group(2, q3_ref, b3_ref, DILATED_CONFIGS[2][1])

    chunk = 256

    def merge(i, carry):
        r = pl.ds(pl.multiple_of(i * chunk, chunk), chunk)
        m0, m1, m2 = m_scr[0, r, :], m_scr[1, r, :], m_scr[2, r, :]
        mx = jnp.maximum(jnp.maximum(m0, m1), m2)
        w0, w1, w2 = jnp.exp(m0 - mx), jnp.exp(m1 - mx), jnp.exp(m2 - mx)
        num = w0 * o_scr[0, r, :] + w1 * o_scr[1, r, :] + w2 * o_scr[2, r, :]
        z = w0 * d_scr[0, r, :] + w1 * d_scr[1, r, :] + w2 * d_scr[2, r, :]
        o_ref[r, :] = (num / z).astype(o_ref.dtype)
        return carry

    lax.fori_loop(0, S // chunk, merge, 0)


def _dilated_attention(qkv3, bias_tab):
    B, S, _ = qkv3.shape
    head_spec = lambda col0: pl.BlockSpec((None, S, HEAD_DIM), lambda b, h: (b, 0, col0 + h))
    bias_spec = lambda g: pl.BlockSpec(
        (None, ATTN_BLOCK, 2 * ATTN_BLOCK), lambda b, h: (g * KV_HEADS + h, 0, 0))
    return pl.pallas_call(
        _dilated_kernel,
        grid=(B, KV_HEADS),
        in_specs=[
            head_spec(0), head_spec(KV_HEADS), head_spec(2 * KV_HEADS),
            head_spec(Q_HEADS), head_spec(Q_HEADS + KV_HEADS),
            bias_spec(0), bias_spec(1), bias_spec(2),
        ],
        out_specs=pl.BlockSpec((None, S, HEAD_DIM), lambda b, h: (b, 0, h)),
        out_shape=jax.ShapeDtypeStruct((B, S, KV_HEADS * HEAD_DIM), BF16),
        scratch_shapes=[pltpu.VMEM((N_GROUPS, S, HEAD_DIM), F32)] * 3,
        compiler_params=pltpu.CompilerParams(
            dimension_semantics=("parallel", "parallel"), vmem_limit_bytes=VMEM_LIMIT_BYTES),
        name="dilated_attn",
    )(qkv3, qkv3, qkv3, qkv3, qkv3, bias_tab, bias_tab, bias_tab)


def _merge_kernel(y_ref, o_ref, gate_ref, x_ref, wbr_ref, wba_ref, wout_ref, g_ref, h_ref):
    br = jnp.dot(y_ref[...], wbr_ref[...], preferred_element_type=F32)
    ba = jnp.dot(o_ref[...], wba_ref[...], preferred_element_type=F32)
    g_rnn = _sigmoid(gate_ref[:, 0:D_MODEL].astype(F32))
    g_att = _sigmoid(gate_ref[:, D_MODEL:GATE_WIDTH].astype(F32))
    merged = g_rnn * br + g_att * ba
    mix = jnp.dot(merged.astype(BF16), wout_ref[...], preferred_element_type=F32)
    h_ref[...] = x_ref[...] + _rms_norm(mix, g_ref[...])


def _merge(y2, o2, gates, x2, wbr, wba, wout, g_post):
    T = x2.shape[0]
    const = lambda i: (0, 0)
    row = lambda i: (i, 0)
    return pl.pallas_call(
        _merge_kernel,
        grid=(T // MG_TM,),
        in_specs=[
            pl.BlockSpec((MG_TM, RNN_WIDTH), row),
            pl.BlockSpec((MG_TM, KV_HEADS * HEAD_DIM), row),
            pl.BlockSpec((MG_TM, GATE_WIDTH), row),
            pl.BlockSpec((MG_TM, D_MODEL), row),
            pl.BlockSpec((RNN_WIDTH, D_MODEL), const),
            pl.BlockSpec((KV_HEADS * HEAD_DIM, D_MODEL), const),
            pl.BlockSpec((D_MODEL, D_MODEL), const),
            pl.BlockSpec((1, D_MODEL), const),
        ],
        out_specs=pl.BlockSpec((MG_TM, D_MODEL), row),
        out_shape=jax.ShapeDtypeStruct((T, D_MODEL), F32),
        compiler_params=pltpu.CompilerParams(
            dimension_semantics=("parallel",), vmem_limit_bytes=VMEM_LIMIT_BYTES),
        name="merge_out",
    )(y2, o2, gates, x2, wbr, wba, wout, g_post)


def _gelu_tanh(x):
    return 0.5 * x * (1.0 + jnp.tanh(math.sqrt(2.0 / math.pi) * (x + 0.044715 * (x * x * x))))


def _ffn_kernel(h_ref, gpre_ref, wg_ref, wu_ref, cw_ref, cb_ref, wd_ref, gpost_ref, out_ref,
                hn_scr, acc_scr, gbuf, carry_scr, *, tiles_per_seq):
    i = pl.program_id(0)
    j = pl.program_id(1)
    tm = FF_TM

    @pl.when(j == 0)
    def _():
        hn_scr[...] = _rms_norm(h_ref[...], gpre_ref[...]).astype(BF16)
        acc_scr[...] = jnp.zeros_like(acc_scr)

    hn = hn_scr[...]
    gate = jnp.dot(hn, wg_ref[...], preferred_element_type=F32)
    up = jnp.dot(hn, wu_ref[...], preferred_element_type=F32)

    seq_start = (i % tiles_per_seq) == 0
    gbuf[0:SUBLANES, :] = jnp.where(seq_start, 0.0, carry_scr[j])
    gbuf[SUBLANES:SUBLANES + tm, :] = gate
    carry_scr[j] = gate[tm - SUBLANES:tm, :]

    conv = cb_ref[...] + cw_ref[FFN_CONV - 1:FFN_CONV, :] * gate
    for k in range(FFN_CONV - 1):
        back = FFN_CONV - 1 - k
        conv = conv + cw_ref[k:k + 1, :] * gbuf[SUBLANES - back:SUBLANES - back + tm, :]
    act = (_gelu_tanh(conv) * up).astype(BF16)
    acc_scr[...] += jnp.dot(act, wd_ref[...], preferred_element_type=F32)

    @pl.when(j == pl.num_programs(1) - 1)
    def _():
        out_ref[...] = h_ref[...] + _rms_norm(acc_scr[...], gpost_ref[...])


def _ffn(h2, g_pre, wg, wu, conv_w, conv_b, wd, g_post, seq_len):
    T = h2.shape[0]
    n_j = FFN_WIDTH // FF_TF
    return pl.pallas_call(
        functools.partial(_ffn_kernel, tiles_per_seq=seq_len // FF_TM),
        grid=(T // FF_TM, n_j),
        in_specs=[
            pl.BlockSpec((FF_TM, D_MODEL), lambda i, j: (i, 0)),
            pl.BlockSpec((1, D_MODEL), lambda i, j: (0, 0)),
            pl.BlockSpec((D_MODEL, FF_TF), lambda i, j: (0, j)),
            pl.BlockSpec((D_MODEL, FF_TF), lambda i, j: (0, j)),
            pl.BlockSpec((FFN_CONV, FF_TF), lambda i, j: (0, j)),
            pl.BlockSpec((1, FF_TF), lambda i, j: (0, j)),
            pl.BlockSpec((FF_TF, D_MODEL), lambda i, j: (j, 0)),
            pl.BlockSpec((1, D_MODEL), lambda i, j: (0, 0)),
        ],
        out_specs=pl.BlockSpec((FF_TM, D_MODEL), lambda i, j: (i, 0)),
        out_shape=jax.ShapeDtypeStruct((T, D_MODEL), F32),
        scratch_shapes=[
            pltpu.VMEM((FF_TM, D_MODEL), BF16),
            pltpu.VMEM((FF_TM, D_MODEL), F32),
            pltpu.VMEM((SUBLANES + FF_TM, FF_TF), F32),
            pltpu.VMEM((n_j, SUBLANES, FF_TF), F32),
        ],
        compiler_params=pltpu.CompilerParams(
            dimension_semantics=("arbitrary", "arbitrary"), vmem_limit_bytes=VMEM_LIMIT_BYTES),
        name="ffn",
    )(h2, g_pre, wg, wu, conv_w, conv_b, wd, g_post)


def kernel(x, rel_bias, norm_mix_pre, norm_mix_post, w_in, conv_rnn_w, conv_rnn_b, w_rg_a, b_rg_a,
           w_rg_x, b_rg_x, lru_lambda, w_branch_rnn, w_branch_att, w_out, norm_ffn_pre, norm_ffn_post,
           w_ffn_gate, w_ffn_up, conv_ffn_w, conv_ffn_b, w_ffn_down):
    B, S, D = x.shape
    assert D == D_MODEL and S % FF_TM == 0 and S % RG_TS == 0 and B % RG_NB == 0
    assert S == DILATED_CONFIGS[-1][0], "attention block structure assumes window == sequence for the widest group"
    depth = w_in.shape[0]
    T = B * S
    bias_tab = _bias_tables(rel_bias)
    h = x.reshape(T, D)
    for l in range(depth):
        xr, qkv, gates = _in_proj(h, norm_mix_pre[l][None], w_in[l].astype(BF16))
        wax = jnp.concatenate([w_rg_a[l], w_rg_x[l]], axis=-1).astype(BF16)
        bax = jnp.stack([b_rg_a[l], b_rg_x[l]], axis=0)
        y_rnn = _rglru(xr.reshape(B, S, RNN_WIDTH), conv_rnn_w[l], conv_rnn_b[l][None], wax, bax,
                       lru_lambda[l][None])
        o_att = _dilated_attention(qkv.reshape(B, S, QKV_WIDTH), bias_tab)
        h = _merge(y_rnn.reshape(T, RNN_WIDTH), o_att.reshape(T, KV_HEADS * HEAD_DIM), gates, h,
                   w_branch_rnn[l].astype(BF16), w_branch_att[l].astype(BF16), w_out[l].astype(BF16),
                   norm_mix_post[l][None])
        h = _ffn(h, norm_ffn_pre[l][None], w_ffn_gate[l].astype(BF16), w_ffn_up[l].astype(BF16),
                 conv_ffn_w[l], conv_ffn_b[l][None], w_ffn_down[l].astype(BF16), norm_ffn_post[l][None],
                 S)
    return h.reshape(B, S, D)
```

```python
import functools
import math

import numpy as np
import jax
import jax.numpy as jnp
from jax import lax
from jax.experimental import pallas as pl
from jax.experimental.pallas import tpu as pltpu

F32 = jnp.float32
BF16 = jnp.bfloat16

D_MODEL = 1024
RNN_WIDTH = 1280
RNN_BLOCKS = 10
RNN_BLOCK = 128
RNN_CONV = 4
LRU_C = 8.0
HEAD_DIM = 128
KV_HEADS = 4
DILATED_CONFIGS = ((128, 1), (512, 4), (2048, 16))
N_GROUPS = 3
Q_HEADS = 12
ATTN_BLOCK = 128
REL_BUCKETS = 32
REL_MAX_DIST = 2048
FFN_WIDTH = 3072
FFN_CONV = 3
EPS = 1e-6
QKV_WIDTH = (Q_HEADS + 2 * KV_HEADS) * HEAD_DIM
GATE_WIDTH = 2 * D_MODEL
IN_WIDTH = RNN_WIDTH + QKV_WIDTH + GATE_WIDTH

SUBLANES = 8
LANES = 128
VMEM_LIMIT_BYTES = 56 * 1024 * 1024

MASK_VALUE = -1e30
LOG2E = math.log2(math.e)
QK_SCALE_LOG2 = HEAD_DIM ** -0.5 * LOG2E

IN_TM = 256
RG_TS = 128
RG_NB = 8
RG_PITCH = RG_TS + 8
MG_TM = 512
FF_TM = 1024
FF_TF = 512


def _rms_norm(x, g):
    ms = jnp.mean(x * x, axis=-1, keepdims=True)
    return x * lax.rsqrt(ms + EPS) * g


def _sigmoid(x):
    return 0.5 + 0.5 * jnp.tanh(0.5 * x)


def _t5_bucket(dist):
    max_exact = REL_BUCKETS // 2
    d = np.maximum(dist, 1).astype(np.float32)
    large = max_exact + np.log(d / max_exact) / math.log(REL_MAX_DIST / max_exact) * (REL_BUCKETS - max_exact)
    large = np.minimum(large.astype(np.int32), REL_BUCKETS - 1)
    return np.where(dist < max_exact, dist, large).astype(np.int32)


def _bucket_maps():
    qi = np.arange(ATTN_BLOCK)[:, None]
    kj = np.arange(2 * ATTN_BLOCK)[None, :]
    delta = ATTN_BLOCK + qi - kj
    maps = []
    for window, dilation in DILATED_CONFIGS:
        n_back = window // dilation
        valid = (delta >= 0) & (delta <= n_back)
        bucket = _t5_bucket(np.maximum(delta, 0) * dilation)
        maps.append(np.where(valid, bucket, -1).astype(np.int32))
    return np.stack(maps, axis=0)


def _bias_kernel(rb_ref, bucket_ref, out_ref):
    head = pl.program_id(0)
    bk = bucket_ref[0]
    acc = jnp.full(bk.shape, MASK_VALUE, F32)
    for b in range(REL_BUCKETS):
        acc = jnp.where(bk == b, rb_ref[b, head] * LOG2E, acc)
    out_ref[0] = acc


def _bias_tables(rel_bias):
    buckets = jnp.asarray(_bucket_maps())
    return pl.pallas_call(
        _bias_kernel,
        grid=(Q_HEADS,),
        in_specs=[
            pl.BlockSpec(memory_space=pltpu.SMEM),
            pl.BlockSpec((1, ATTN_BLOCK, 2 * ATTN_BLOCK), lambda h: (h // KV_HEADS, 0, 0)),
        ],
        out_specs=pl.BlockSpec((1, ATTN_BLOCK, 2 * ATTN_BLOCK), lambda h: (h, 0, 0)),
        out_shape=jax.ShapeDtypeStruct((Q_HEADS, ATTN_BLOCK, 2 * ATTN_BLOCK), F32),
        name="bias_table",
    )(rel_bias, buckets)


def _in_proj_kernel(x_ref, g_ref, w_ref, xr_ref, qkv_ref, gate_ref):
    hn = _rms_norm(x_ref[...], g_ref[...]).astype(BF16)
    c0, c1 = RNN_WIDTH, RNN_WIDTH + QKV_WIDTH
    xr_ref[...] = jnp.dot(hn, w_ref[:, 0:c0], preferred_element_type=F32)
    qkv_ref[...] = jnp.dot(hn, w_ref[:, c0:c1], preferred_element_type=F32)
    gate_ref[...] = jnp.dot(hn, w_ref[:, c1:IN_WIDTH], preferred_element_type=F32).astype(BF16)


def _in_proj(x2, g, w_bf):
    T = x2.shape[0]
    const = lambda i: (0, 0)
    row = lambda i: (i, 0)
    return pl.pallas_call(
        _in_proj_kernel,
        grid=(T // IN_TM,),
        in_specs=[
            pl.BlockSpec((IN_TM, D_MODEL), row),
            pl.BlockSpec((1, D_MODEL), const),
            pl.BlockSpec((D_MODEL, IN_WIDTH), const, pipeline_mode=pl.Buffered(1)),
        ],
        out_specs=[
            pl.BlockSpec((IN_TM, RNN_WIDTH), row),
            pl.BlockSpec((IN_TM, QKV_WIDTH), row),
            pl.BlockSpec((IN_TM, GATE_WIDTH), row),
        ],
        out_shape=[
            jax.ShapeDtypeStruct((T, RNN_WIDTH), F32),
            jax.ShapeDtypeStruct((T, QKV_WIDTH), F32),
            jax.ShapeDtypeStruct((T, GATE_WIDTH), BF16),
        ],
        compiler_params=pltpu.CompilerParams(
            dimension_semantics=("parallel",), vmem_limit_bytes=VMEM_LIMIT_BYTES),
        name="in_proj",
    )(x2, g, w_bf)


def _rglru_kernel(xr_ref, cw_ref, cb_ref, wax_ref, bax_ref, lam_ref, y_ref,
                  xbuf, a_scr, u_scr, y_scr, h_scr):
    s = pl.program_id(1)
    ts, pitch = RG_TS, RG_PITCH

    @pl.when(s == 0)
    def _():
        xbuf[:, 0:SUBLANES, :] = jnp.zeros((RG_NB, SUBLANES, RNN_WIDTH), F32)
        h_scr[...] = jnp.zeros_like(h_scr)

    xbuf[:, SUBLANES:SUBLANES + ts, :] = xr_ref[...]

    for n in range(RNN_BLOCKS):
        cols = slice(n * RNN_BLOCK, (n + 1) * RNN_BLOCK)
        xc = cb_ref[:, cols][None]
        for k in range(RNN_CONV):
            back = RNN_CONV - 1 - k
            xc = xc + cw_ref[k:k + 1, cols][None] * xbuf[:, SUBLANES - back:SUBLANES - back + ts, cols]
        xc2 = xc.reshape(RG_NB * ts, RNN_BLOCK)
        g = jnp.dot(xc2.astype(BF16), wax_ref[n], preferred_element_type=F32)
        r = _sigmoid(g[:, 0:RNN_BLOCK] + bax_ref[0:1, cols])
        i = _sigmoid(g[:, RNN_BLOCK:2 * RNN_BLOCK] + bax_ref[1:2, cols])
        neg_lam = -lam_ref[:, cols]
        softplus = jnp.maximum(neg_lam, 0.0) + jnp.log1p(jnp.exp(-jnp.abs(neg_lam)))
        log_a = (-LRU_C * softplus) * r
        a = jnp.exp(log_a)
        th = jnp.tanh(log_a)
        u = jnp.sqrt(-2.0 * th / (1.0 - th)) * (i * xc2)
        for b in range(RG_NB):
            a_scr[n, b * pitch:b * pitch + ts, :] = a[b * ts:(b + 1) * ts]
            u_scr[n, b * pitch:b * pitch + ts, :] = u[b * ts:(b + 1) * ts]

    xbuf[:, 0:SUBLANES, :] = xr_ref[:, ts - SUBLANES:ts, :]

    def step(t, hs):
        new = []
        for n in range(RNN_BLOCKS):
            rows = pl.ds(t, RG_NB, stride=pitch)
            h = a_scr[n, rows, :] * hs[n] + u_scr[n, rows, :]
            y_scr[n, rows, :] = h
            new.append(h)
        return tuple(new)

    hs = lax.fori_loop(0, ts, step, tuple(h_scr[n] for n in range(RNN_BLOCKS)), unroll=8)
    for n in range(RNN_BLOCKS):
        h_scr[n] = hs[n]
        for b in range(RG_NB):
            y_ref[b, :, n * RNN_BLOCK:(n + 1) * RNN_BLOCK] = (
                y_scr[n, b * pitch:b * pitch + ts, :].astype(BF16))


def _rglru(xr3, conv_w, conv_b, wax, bax, lam):
    B, S, _ = xr3.shape
    const2 = lambda b, s: (0, 0)
    scan_scratch = pltpu.VMEM((RNN_BLOCKS, RG_NB * RG_PITCH, RNN_BLOCK), F32)
    return pl.pallas_call(
        _rglru_kernel,
        grid=(B // RG_NB, S // RG_TS),
        in_specs=[
            pl.BlockSpec((RG_NB, RG_TS, RNN_WIDTH), lambda b, s: (b, s, 0)),
            pl.BlockSpec((RNN_CONV, RNN_WIDTH), const2),
            pl.BlockSpec((1, RNN_WIDTH), const2),
            pl.BlockSpec((RNN_BLOCKS, RNN_BLOCK, 2 * RNN_BLOCK), lambda b, s: (0, 0, 0)),
            pl.BlockSpec((2, RNN_WIDTH), const2),
            pl.BlockSpec((1, RNN_WIDTH), const2),
        ],
        out_specs=pl.BlockSpec((RG_NB, RG_TS, RNN_WIDTH), lambda b, s: (b, s, 0)),
        out_shape=jax.ShapeDtypeStruct((B, S, RNN_WIDTH), BF16),
        scratch_shapes=[
            pltpu.VMEM((RG_NB, SUBLANES + RG_TS, RNN_WIDTH), F32),
            scan_scratch, scan_scratch, scan_scratch,
            pltpu.VMEM((RNN_BLOCKS, RG_NB, RNN_BLOCK), F32),
        ],
        compiler_params=pltpu.CompilerParams(
            dimension_semantics=("parallel", "arbitrary"), vmem_limit_bytes=VMEM_LIMIT_BYTES),
        name="rglru",
    )(xr3, conv_w, conv_b, wax, bax, lam)


def _dilated_kernel(q1_ref, q2_ref, q3_ref, k_ref, v_ref, b1_ref, b2_ref, b3_ref, o_ref,
                    qp, kp, vp, tq, tk, tv, s_scr, p_scr, o_scr, m_scr, d_scr):
    S = k_ref.shape[0]
    blk = ATTN_BLOCK
    n_blocks = S // blk
    quarter = S // 4
    nt = (((1,), (1,)), ((), ()))

    def put(g, r0, n, q, k, v):
        qp[g, r0:r0 + n, :] = (q * QK_SCALE_LOG2).astype(BF16)
        kp[g, r0:r0 + n, :] = k.astype(BF16)
        vp[g, r0:r0 + n, 0:HEAD_DIM] = v.astype(BF16)

    for g in range(N_GROUPS):
        vp[g, :, HEAD_DIM:2 * HEAD_DIM] = jnp.ones((S, HEAD_DIM), BF16)
    for c in range(4):
        r = slice(c * quarter, (c + 1) * quarter)
        put(0, c * quarter, quarter, q1_ref[r, :], k_ref[r, :], v_ref[r, :])
    for c in range(4):
        sr = pl.ds(c, quarter, stride=4)
        r = slice(c * quarter, (c + 1) * quarter)
        k4, v4 = k_ref[sr, :], v_ref[sr, :]
        put(1, c * quarter, quarter, q2_ref[sr, :], k4, v4)
        tk[r, :] = k4
        tv[r, :] = v4
        tq[r, :] = q3_ref[sr, :]
    for c4 in range(4):
        for j in range(4):
            sr = pl.ds(c4 * quarter + j, blk, stride=4)
            put(2, (4 * j + c4) * blk, blk, tq[sr, :], tk[sr, :], tv[sr, :])

    def run_group(g, b_ref, dil):
        nb = S // dil // blk

        def nat_rows(p):
            c, n = divmod(p, nb)
            start = n * (blk * dil) + c
            return pl.ds(start, blk) if dil == 1 else pl.ds(start, blk, stride=dil)

        def key_rows(p):
            has_prev = p % nb != 0
            return (slice((p - 1) * blk, (p + 1) * blk) if has_prev else slice(p * blk, (p + 1) * blk),
                    slice(0, 2 * blk) if has_prev else slice(blk, 2 * blk))

        for p in range(n_blocks):
            kr, cols = key_rows(p)
            s = lax.dot_general(qp[g, p * blk:(p + 1) * blk, :], kp[g, kr, :], nt, preferred_element_type=F32)
            s_scr[p, :, cols] = s + b_ref[:, cols]
        for p in range(n_blocks):
            _, cols = key_rows(p)
            m = jnp.max(s_scr[p, :, cols], axis=-1, keepdims=True)
            m_scr[g, nat_rows(p), :] = jnp.broadcast_to(m, (blk, HEAD_DIM))
            p_scr[p, :, cols] = jnp.exp2(s_scr[p, :, cols] - m).astype(BF16)
        for p in range(n_blocks):
            kr, cols = key_rows(p)
            od = jnp.dot(p_scr[p, :, cols], vp[g, kr, :], preferred_element_type=F32)
            o_scr[g, nat_rows(p), :] = od[:, 0:HEAD_DIM]
            d_scr[g, nat_rows(p), :] = od[:, HEAD_DIM:2 * HEAD_DIM]

    run_group(0, b1_ref, DILATED_CONFIGS[0][1])
    run_group(1, b2_ref, DILATED_CONFIGS[1][1])
    run_group(2, b3_ref, DILATED_CONFIGS[2][1])

    chunk = 256

    def merge(i, carry):
        r = pl.ds(pl.multiple_of(i * chunk, chunk), chunk)
        m0, m1, m2 = m_scr[0, r, :], m_scr[1, r, :], m_scr[2, r, :]
        mx = jnp.maximum(jnp.maximum(m0, m1), m2)
        w0, w1, w2 = jnp.exp2(m0 - mx), jnp.exp2(m1 - mx), jnp.exp2(m2 - mx)
        num = w0 * o_scr[0, r, :] + w1 * o_scr[1, r, :] + w2 * o_scr[2, r, :]
        z = w0 * d_scr[0, r, :] + w1 * d_scr[1, r, :] + w2 * d_scr[2, r, :]
        o_ref[r, :] = (num / z).astype(o_ref.dtype)
        return carry

    lax.fori_loop(0, S // chunk, merge, 0)


def _dilated_attention(qkv3, bias_tab):
    B, S, _ = qkv3.shape
    head_spec = lambda col0: pl.BlockSpec((None, S, HEAD_DIM), lambda b, h: (b, 0, col0 + h))
    bias_spec = lambda g: pl.BlockSpec(
        (None, ATTN_BLOCK, 2 * ATTN_BLOCK), lambda b, h: (g * KV_HEADS + h, 0, 0))
    return pl.pallas_call(
        _dilated_kernel,
        grid=(B, KV_HEADS),
        in_specs=[
            head_spec(0), head_spec(KV_HEADS), head_spec(2 * KV_HEADS),
            head_spec(Q_HEADS), head_spec(Q_HEADS + KV_HEADS),
            bias_spec(0), bias_spec(1), bias_spec(2),
        ],
        out_specs=pl.BlockSpec((None, S, HEAD_DIM), lambda b, h: (b, 0, h)),
        out_shape=jax.ShapeDtypeStruct((B, S, KV_HEADS * HEAD_DIM), BF16),
        scratch_shapes=[
            pltpu.VMEM((N_GROUPS, S, HEAD_DIM), BF16),
            pltpu.VMEM((N_GROUPS, S, HEAD_DIM), BF16),
            pltpu.VMEM((N_GROUPS, S, 2 * HEAD_DIM), BF16),
            pltpu.VMEM((S, HEAD_DIM), F32),
            pltpu.VMEM((S, HEAD_DIM), F32),
            pltpu.VMEM((S, HEAD_DIM), F32),
            pltpu.VMEM((S // ATTN_BLOCK, ATTN_BLOCK, 2 * ATTN_BLOCK), F32),
            pltpu.VMEM((S // ATTN_BLOCK, ATTN_BLOCK, 2 * ATTN_BLOCK), BF16),
            pltpu.VMEM((N_GROUPS, S, HEAD_DIM), F32),
            pltpu.VMEM((N_GROUPS, S, HEAD_DIM), F32),
            pltpu.VMEM((N_GROUPS, S, HEAD_DIM), F32),
        ],
        compiler_params=pltpu.CompilerParams(
            dimension_semantics=("parallel", "parallel"), vmem_limit_bytes=VMEM_LIMIT_BYTES),
        name="dilated_attn",
    )(qkv3, qkv3, qkv3, qkv3, qkv3, bias_tab, bias_tab, bias_tab)


def _merge_kernel(y_ref, o_ref, gate_ref, x_ref, wbr_ref, wba_ref, wout_ref, g_ref, h_ref):
    br = jnp.dot(y_ref[...], wbr_ref[...], preferred_element_type=F32)
    ba = jnp.dot(o_ref[...], wba_ref[...], preferred_element_type=F32)
    g_rnn = _sigmoid(gate_ref[:, 0:D_MODEL].astype(F32))
    g_att = _sigmoid(gate_ref[:, D_MODEL:GATE_WIDTH].astype(F32))
    merged = g_rnn * br + g_att * ba
    mix = jnp.dot(merged.astype(BF16), wout_ref[...], preferred_element_type=F32)
    h_ref[...] = x_ref[...] + _rms_norm(mix, g_ref[...])


def _merge(y2, o2, gates, x2, wbr, wba, wout, g_post):
    T = x2.shape[0]
    const = lambda i: (0, 0)
    row = lambda i: (i, 0)
    return pl.pallas_call(
        _merge_kernel,
        grid=(T // MG_TM,),
        in_specs=[
            pl.BlockSpec((MG_TM, RNN_WIDTH), row),
            pl.BlockSpec((MG_TM, KV_HEADS * HEAD_DIM), row),
            pl.BlockSpec((MG_TM, GATE_WIDTH), row),
            pl.BlockSpec((MG_TM, D_MODEL), row),
            pl.BlockSpec((RNN_WIDTH, D_MODEL), const),
            pl.BlockSpec((KV_HEADS * HEAD_DIM, D_MODEL), const),
            pl.BlockSpec((D_MODEL, D_MODEL), const),
            pl.BlockSpec((1, D_MODEL), const),
        ],
        out_specs=pl.BlockSpec((MG_TM, D_MODEL), row),
        out_shape=jax.ShapeDtypeStruct((T, D_MODEL), F32),
        compiler_params=pltpu.CompilerParams(
            dimension_semantics=("parallel",), vmem_limit_bytes=VMEM_LIMIT_BYTES),
        name="merge_out",
    )(y2, o2, gates, x2, wbr, wba, wout, g_post)


def _gelu_tanh(x):
    return 0.5 * x * (1.0 + jnp.tanh(math.sqrt(2.0 / math.pi) * (x + 0.044715 * (x * x * x))))


def _ffn_kernel(h_ref, gpre_ref, wg_ref, wu_ref, cw_ref, cb_ref, wd_ref, gpost_ref, out_ref,
                hn_scr, acc_scr, gbuf, carry_scr, *, tiles_per_seq):
    i = pl.program_id(0)
    j = pl.program_id(1)
    tm = FF_TM

    @pl.when(j == 0)
    def _():
        hn_scr[...] = _rms_norm(h_ref[...], gpre_ref[...]).astype(BF16)
        acc_scr[...] = jnp.zeros_like(acc_scr)

    hn = hn_scr[...]
    gate = jnp.dot(hn, wg_ref[...], preferred_element_type=F32)
    up = jnp.dot(hn, wu_ref[...], preferred_element_type=F32)

    seq_start = (i % tiles_per_seq) == 0
    gbuf[0:SUBLANES, :] = jnp.where(seq_start, 0.0, carry_scr[j])
    gbuf[SUBLANES:SUBLANES + tm, :] = gate
    carry_scr[j] = gate[tm - SUBLANES:tm, :]

    conv = cb_ref[...] + cw_ref[FFN_CONV - 1:FFN_CONV, :] * gate
    for k in range(FFN_CONV - 1):
        back = FFN_CONV - 1 - k
        conv = conv + cw_ref[k:k + 1, :] * gbuf[SUBLANES - back:SUBLANES - back + tm, :]
    act = (_gelu_tanh(conv) * up).astype(BF16)
    acc_scr[...] += jnp.dot(act, wd_ref[...], preferred_element_type=F32)

    @pl.when(j == pl.num_programs(1) - 1)
    def _():
        out_ref[...] = h_ref[...] + _rms_norm(acc_scr[...], gpost_ref[...])


def _ffn(h2, g_pre, wg, wu, conv_w, conv_b, wd, g_post, seq_len):
    T = h2.shape[0]
    n_j = FFN_WIDTH // FF_TF
    return pl.pallas_call(
        functools.partial(_ffn_kernel, tiles_per_seq=seq_len // FF_TM),
        grid=(T // FF_TM, n_j),
        in_specs=[
            pl.BlockSpec((FF_TM, D_MODEL), lambda i, j: (i, 0)),
            pl.BlockSpec((1, D_MODEL), lambda i, j: (0, 0)),
            pl.BlockSpec((D_MODEL, FF_TF), lambda i, j: (0, j)),
            pl.BlockSpec((D_MODEL, FF_TF), lambda i, j: (0, j)),
            pl.BlockSpec((FFN_CONV, FF_TF), lambda i, j: (0, j)),
            pl.BlockSpec((1, FF_TF), lambda i, j: (0, j)),
            pl.BlockSpec((FF_TF, D_MODEL), lambda i, j: (j, 0)),
            pl.BlockSpec((1, D_MODEL), lambda i, j: (0, 0)),
        ],
        out_specs=pl.BlockSpec((FF_TM, D_MODEL), lambda i, j: (i, 0)),
        out_shape=jax.ShapeDtypeStruct((T, D_MODEL), F32),
        scratch_shapes=[
            pltpu.VMEM((FF_TM, D_MODEL), BF16),
            pltpu.VMEM((FF_TM, D_MODEL), F32),
            pltpu.VMEM((SUBLANES + FF_TM, FF_TF), F32),
            pltpu.VMEM((n_j, SUBLANES, FF_TF), F32),
        ],
        compiler_params=pltpu.CompilerParams(
            dimension_semantics=("arbitrary", "arbitrary"), vmem_limit_bytes=VMEM_LIMIT_BYTES),
        name="ffn",
    )(h2, g_pre, wg, wu, conv_w, conv_b, wd, g_post)


def kernel(x, rel_bias, norm_mix_pre, norm_mix_post, w_in, conv_rnn_w, conv_rnn_b, w_rg_a, b_rg_a,
           w_rg_x, b_rg_x, lru_lambda, w_branch_rnn, w_branch_att, w_out, norm_ffn_pre, norm_ffn_post,
           w_ffn_gate, w_ffn_up, conv_ffn_w, conv_ffn_b, w_ffn_down):
    B, S, D = x.shape
    assert D == D_MODEL and S % FF_TM == 0 and S % RG_TS == 0 and B % RG_NB == 0
    assert S == DILATED_CONFIGS[-1][0], "attention block structure assumes window == sequence for the widest group"
    depth = w_in.shape[0]
    T = B * S
    bias_tab = _bias_tables(rel_bias)
    h = x.reshape(T, D)
    for l in range(depth):
        xr, qkv, gates = _in_proj(h, norm_mix_pre[l][None], w_in[l].astype(BF16))
        wax = jnp.concatenate([w_rg_a[l], w_rg_x[l]], axis=-1).astype(BF16)
        bax = jnp.stack([b_rg_a[l], b_rg_x[l]], axis=0)
        y_rnn = _rglru(xr.reshape(B, S, RNN_WIDTH), conv_rnn_w[l], conv_rnn_b[l][None], wax, bax,
                       lru_lambda[l][None])
        o_att = _dilated_attention(qkv.reshape(B, S, QKV_WIDTH), bias_tab)
        h = _merge(y_rnn.reshape(T, RNN_WIDTH), o_att.reshape(T, KV_HEADS * HEAD_DIM), gates, h,
                   w_branch_rnn[l].astype(BF16), w_branch_att[l].astype(BF16), w_out[l].astype(BF16),
                   norm_mix_post[l][None])
        h = _ffn(h, norm_ffn_pre[l][None], w_ffn_gate[l].astype(BF16), w_ffn_up[l].astype(BF16),
                 conv_ffn_w[l], conv_ffn_b[l][None], w_ffn_down[l].astype(BF16), norm_ffn_post[l][None],
                 S)
    return h.reshape(B, S, D)
```

```python
import functools
import math

import numpy as np
import jax
import jax.numpy as jnp
from jax import lax
from jax.experimental import pallas as pl
from jax.experimental.pallas import tpu as pltpu

F32 = jnp.float32
BF16 = jnp.bfloat16

D_MODEL = 1024
RNN_WIDTH = 1280
RNN_BLOCKS = 10
RNN_BLOCK = 128
RNN_CONV = 4
LRU_C = 8.0
HEAD_DIM = 128
KV_HEADS = 4
DILATED_CONFIGS = ((128, 1), (512, 4), (2048, 16))
N_GROUPS = 3
Q_HEADS = 12
ATTN_BLOCK = 128
REL_BUCKETS = 32
REL_MAX_DIST = 2048
FFN_WIDTH = 3072
FFN_CONV = 3
EPS = 1e-6
QKV_WIDTH = (Q_HEADS + 2 * KV_HEADS) * HEAD_DIM
GATE_WIDTH = 2 * D_MODEL
IN_WIDTH = RNN_WIDTH + QKV_WIDTH + GATE_WIDTH

SUBLANES = 8
LANES = 128
VMEM_LIMIT_BYTES = 56 * 1024 * 1024

MASK_VALUE = -1e30
LOG2E = math.log2(math.e)
QK_SCALE_LOG2 = HEAD_DIM ** -0.5 * LOG2E

IN_TM = 256
RG_TS = 128
RG_NB = 8
RG_PITCH = RG_TS + 8
RG_TC = 16
MG_TM = 512
FF_TM = 1024
FF_TF = 512


def _rms_norm(x, g):
    ms = jnp.mean(x * x, axis=-1, keepdims=True)
    return x * lax.rsqrt(ms + EPS) * g


def _sigmoid(x):
    return 0.5 + 0.5 * jnp.tanh(0.5 * x)


def _t5_bucket(dist):
    max_exact = REL_BUCKETS // 2
    d = np.maximum(dist, 1).astype(np.float32)
    large = max_exact + np.log(d / max_exact) / math.log(REL_MAX_DIST / max_exact) * (REL_BUCKETS - max_exact)
    large = np.minimum(large.astype(np.int32), REL_BUCKETS - 1)
    return np.where(dist < max_exact, dist, large).astype(np.int32)


def _bucket_maps():
    qi = np.arange(ATTN_BLOCK)[:, None]
    kj = np.arange(2 * ATTN_BLOCK)[None, :]
    delta = ATTN_BLOCK + qi - kj
    maps = []
    for window, dilation in DILATED_CONFIGS:
        n_back = window // dilation
        valid = (delta >= 0) & (delta <= n_back)
        bucket = _t5_bucket(np.maximum(delta, 0) * dilation)
        maps.append(np.where(valid, bucket, -1).astype(np.int32))
    return np.stack(maps, axis=0)


def _bias_kernel(rb_ref, bucket_ref, out_ref):
    head = pl.program_id(0)
    bk = bucket_ref[0]
    acc = jnp.full(bk.shape, MASK_VALUE, F32)
    for b in range(REL_BUCKETS):
        acc = jnp.where(bk == b, rb_ref[b, head] * LOG2E, acc)
    out_ref[0] = acc


def _bias_tables(rel_bias):
    buckets = jnp.asarray(_bucket_maps())
    return pl.pallas_call(
        _bias_kernel,
        grid=(Q_HEADS,),
        in_specs=[
            pl.BlockSpec(memory_space=pltpu.SMEM),
            pl.BlockSpec((1, ATTN_BLOCK, 2 * ATTN_BLOCK), lambda h: (h // KV_HEADS, 0, 0)),
        ],
        out_specs=pl.BlockSpec((1, ATTN_BLOCK, 2 * ATTN_BLOCK), lambda h: (h, 0, 0)),
        out_shape=jax.ShapeDtypeStruct((Q_HEADS, ATTN_BLOCK, 2 * ATTN_BLOCK), F32),
        name="bias_table",
    )(rel_bias, buckets)


def _in_proj_kernel(x_ref, g_ref, w_ref, xr_ref, qkv_ref, gate_ref):
    hn = _rms_norm(x_ref[...], g_ref[...]).astype(BF16)
    c0, c1 = RNN_WIDTH, RNN_WIDTH + QKV_WIDTH
    xr = jnp.dot(hn, w_ref[:, 0:c0], preferred_element_type=F32)
    for n in range(RNN_BLOCKS):
        for k in range(IN_TM // RG_TS):
            xr_ref[n, k, 0:RG_TS, :] = xr[k * RG_TS:(k + 1) * RG_TS, n * RNN_BLOCK:(n + 1) * RNN_BLOCK]
            xr_ref[n, k, RG_TS:RG_PITCH, :] = jnp.zeros((RG_PITCH - RG_TS, RNN_BLOCK), F32)
    qkv_ref[...] = jnp.dot(hn, w_ref[:, c0:c1], preferred_element_type=F32)
    gate_ref[...] = jnp.dot(hn, w_ref[:, c1:IN_WIDTH], preferred_element_type=F32).astype(BF16)


def _in_proj(x2, g, w_bf, seq_len):
    T = x2.shape[0]
    tiles_per_seq = seq_len // IN_TM
    const = lambda i: (0, 0)
    row = lambda i: (i, 0)

    def xr_index(i):
        b = i // tiles_per_seq
        return (0, b // RG_NB, i % tiles_per_seq, b % RG_NB, 0)

    return pl.pallas_call(
        _in_proj_kernel,
        grid=(T // IN_TM,),
        in_specs=[
            pl.BlockSpec((IN_TM, D_MODEL), row),
            pl.BlockSpec((1, D_MODEL), const),
            pl.BlockSpec((D_MODEL, IN_WIDTH), const, pipeline_mode=pl.Buffered(1)),
        ],
        out_specs=[
            pl.BlockSpec((RNN_BLOCKS, None, IN_TM // RG_TS, RG_PITCH, RNN_BLOCK), xr_index),
            pl.BlockSpec((IN_TM, QKV_WIDTH), row),
            pl.BlockSpec((IN_TM, GATE_WIDTH), row),
        ],
        out_shape=[
            jax.ShapeDtypeStruct(
                (RNN_BLOCKS, T // seq_len // RG_NB, seq_len // RG_TS, RG_NB * RG_PITCH, RNN_BLOCK), F32),
            jax.ShapeDtypeStruct((T, QKV_WIDTH), F32),
            jax.ShapeDtypeStruct((T, GATE_WIDTH), BF16),
        ],
        compiler_params=pltpu.CompilerParams(
            dimension_semantics=("parallel",), vmem_limit_bytes=VMEM_LIMIT_BYTES),
        name="in_proj",
    )(x2, g, w_bf)


def _rglru_kernel(xr_ref, cw_ref, cb_ref, wax_ref, bax_ref, lam_ref, y_ref, y_scr, carry):
    s = pl.program_id(1)
    ts, pitch, tc = RG_TS, RG_PITCH, RG_TC

    @pl.when(s == 0)
    def _():
        carry[...] = jnp.zeros_like(carry)

    def chunk(ci, _):
        t0 = ci * tc
        for n in range(RNN_BLOCKS):
            cols = slice(n * RNN_BLOCK, (n + 1) * RNN_BLOCK)
            bcast = lambda row: jnp.broadcast_to(row, (RG_NB, RNN_BLOCK))
            w = [bcast(cw_ref[k:k + 1, cols]) for k in range(RNN_CONV)]
            cb = bcast(cb_ref[:, cols])
            xs = [carry[n, k] for k in range(RNN_CONV - 1)]
            xs += [xr_ref[n, pl.ds(t0 + j, RG_NB, stride=pitch), :] for j in range(tc)]
            xc = jnp.concatenate(
                [cb + w[3] * xs[j + 3] + w[2] * xs[j + 2] + w[1] * xs[j + 1] + w[0] * xs[j] for j in range(tc)],
                axis=0)
            g = jnp.dot(xc.astype(BF16), wax_ref[n], preferred_element_type=F32)
            r = _sigmoid(g[:, 0:RNN_BLOCK] + bax_ref[0:1, cols])
            i = _sigmoid(g[:, RNN_BLOCK:2 * RNN_BLOCK] + bax_ref[1:2, cols])
            neg_lam = -lam_ref[:, cols]
            softplus = jnp.maximum(neg_lam, 0.0) + jnp.log1p(jnp.exp(-jnp.abs(neg_lam)))
            log_a = (-LRU_C * softplus) * r
            a = jnp.exp(log_a)
            th = jnp.tanh(log_a)
            u = jnp.sqrt(-2.0 * th / (1.0 - th)) * (i * xc)
            h = carry[n, RNN_CONV - 1]
            for j in range(tc):
                h = a[j * RG_NB:(j + 1) * RG_NB] * h + u[j * RG_NB:(j + 1) * RG_NB]
                y_scr[n, pl.ds(t0 + j, RG_NB, stride=pitch), :] = h
            for k in range(RNN_CONV - 1):
                carry[n, k] = xs[tc + k]
            carry[n, RNN_CONV - 1] = h
        return 0

    lax.fori_loop(0, ts // tc, chunk, 0)

    for n in range(RNN_BLOCKS):
        for b in range(RG_NB):
            y_ref[b, :, n * RNN_BLOCK:(n + 1) * RNN_BLOCK] = (
                y_scr[n, b * pitch:b * pitch + ts, :].astype(BF16))


def _rglru(xr5, conv_w, conv_b, wax, bax, lam):
    _, n_bg, n_s, _, _ = xr5.shape
    const2 = lambda b, s: (0, 0)
    return pl.pallas_call(
        _rglru_kernel,
        grid=(n_bg, n_s),
        in_specs=[
            pl.BlockSpec((RNN_BLOCKS, None, None, RG_NB * RG_PITCH, RNN_BLOCK), lambda b, s: (0, b, s, 0, 0)),
            pl.BlockSpec((RNN_CONV, RNN_WIDTH), const2),
            pl.BlockSpec((1, RNN_WIDTH), const2),
            pl.BlockSpec((RNN_BLOCKS, RNN_BLOCK, 2 * RNN_BLOCK), lambda b, s: (0, 0, 0)),
            pl.BlockSpec((2, RNN_WIDTH), const2),
            pl.BlockSpec((1, RNN_WIDTH), const2),
        ],
        out_specs=pl.BlockSpec((RG_NB, RG_TS, RNN_WIDTH), lambda b, s: (b, s, 0)),
        out_shape=jax.ShapeDtypeStruct((n_bg * RG_NB, n_s * RG_TS, RNN_WIDTH), BF16),
        scratch_shapes=[
            pltpu.VMEM((RNN_BLOCKS, RG_NB * RG_PITCH, RNN_BLOCK), F32),
            pltpu.VMEM((RNN_BLOCKS, RNN_CONV, RG_NB, RNN_BLOCK), F32),
        ],
        compiler_params=pltpu.CompilerParams(
            dimension_semantics=("parallel", "arbitrary"), vmem_limit_bytes=VMEM_LIMIT_BYTES),
        name="rglru",
    )(xr5, conv_w, conv_b, wax, bax, lam)


def _dilated_kernel(q1_ref, q2_ref, q3_ref, k_ref, v_ref, b1_ref, b2_ref, b3_ref, o_ref,
                    qp, kp, vp, tq, tk, tv, s_scr, p_scr, o_scr, m_scr, d_scr):
    S = k_ref.shape[0]
    blk = ATTN_BLOCK
    n_blocks = S // blk
    quarter = S // 4
    nt = (((1,), (1,)), ((), ()))

    def put(g, r0, n, q, k, v):
        qp[g, r0:r0 + n, :] = (q * QK_SCALE_LOG2).astype(BF16)
        kp[g, r0:r0 + n, :] = k.astype(BF16)
        vp[g, r0:r0 + n, 0:HEAD_DIM] = v.astype(BF16)

    for g in range(N_GROUPS):
        vp[g, :, HEAD_DIM:2 * HEAD_DIM] = jnp.ones((S, HEAD_DIM), BF16)
    for c in range(4):
        r = slice(c * quarter, (c + 1) * quarter)
        put(0, c * quarter, quarter, q1_ref[r, :], k_ref[r, :], v_ref[r, :])
    for c in range(4):
        sr = pl.ds(c, quarter, stride=4)
        r = slice(c * quarter, (c + 1) * quarter)
        k4, v4 = k_ref[sr, :], v_ref[sr, :]
        put(1, c * quarter, quarter, q2_ref[sr, :], k4, v4)
        tk[r, :] = k4
        tv[r, :] = v4
        tq[r, :] = q3_ref[sr, :]
    for c4 in range(4):
        for j in range(4):
            sr = pl.ds(c4 * quarter + j, blk, stride=4)
            put(2, (4 * j + c4) * blk, blk, tq[sr, :], tk[sr, :], tv[sr, :])

    def run_group(g, b_ref, dil):
        nb = S // dil // blk

        def nat_rows(p):
            c, n = divmod(p, nb)
            start = n * (blk * dil) + c
            return pl.ds(start, blk) if dil == 1 else pl.ds(start, blk, stride=dil)

        def key_rows(p):
            has_prev = p % nb != 0
            return (slice((p - 1) * blk, (p + 1) * blk) if has_prev else slice(p * blk, (p + 1) * blk),
                    slice(0, 2 * blk) if has_prev else slice(blk, 2 * blk))

        for p in range(n_blocks):
            kr, cols = key_rows(p)
            s = lax.dot_general(qp[g, p * blk:(p + 1) * blk, :], kp[g, kr, :], nt, preferred_element_type=F32)
            s_scr[p, :, cols] = s + b_ref[:, cols]
        for p in range(n_blocks):
            _, cols = key_rows(p)
            m = jnp.max(s_scr[p, :, cols], axis=-1, keepdims=True)
            m_scr[g, nat_rows(p), :] = jnp.broadcast_to(m, (blk, HEAD_DIM))
            p_scr[p, :, cols] = jnp.exp2(s_scr[p, :, cols] - m).astype(BF16)
        for p in range(n_blocks):
            kr, cols = key_rows(p)
            od = jnp.dot(p_scr[p, :, cols], vp[g, kr, :], preferred_element_type=F32)
            o_scr[g, nat_rows(p), :] = od[:, 0:HEAD_DIM]
            d_scr[g, nat_rows(p), :] = od[:, HEAD_DIM:2 * HEAD_DIM]

    run_group(0, b1_ref, DILATED_CONFIGS[0][1])
    run_group(1, b2_ref, DILATED_CONFIGS[1][1])
    run_group(2, b3_ref, DILATED_CONFIGS[2][1])

    chunk = 256

    def merge(i, carry):
        r = pl.ds(pl.multiple_of(i * chunk, chunk), chunk)
        m0, m1, m2 = m_scr[0, r, :], m_scr[1, r, :], m_scr[2, r, :]
        mx = jnp.maximum(jnp.maximum(m0, m1), m2)
        w0, w1, w2 = jnp.exp2(m0 - mx), jnp.exp2(m1 - mx), jnp.exp2(m2 - mx)
        num = w0 * o_scr[0, r, :] + w1 * o_scr[1, r, :] + w2 * o_scr[2, r, :]
        z = w0 * d_scr[0, r, :] + w1 * d_scr[1, r, :] + w2 * d_scr[2, r, :]
        o_ref[r, :] = (num / z).astype(o_ref.dtype)
        return carry

    lax.fori_loop(0, S // chunk, merge, 0)


def _dilated_attention(qkv3, bias_tab):
    B, S, _ = qkv3.shape
    head_spec = lambda col0: pl.BlockSpec((None, S, HEAD_DIM), lambda b, h: (b, 0, col0 + h))
    bias_spec = lambda g: pl.BlockSpec(
        (None, ATTN_BLOCK, 2 * ATTN_BLOCK), lambda b, h: (g * KV_HEADS + h, 0, 0))
    return pl.pallas_call(
        _dilated_kernel,
        grid=(B, KV_HEADS),
        in_specs=[
            head_spec(0), head_spec(KV_HEADS), head_spec(2 * KV_HEADS),
            head_spec(Q_HEADS), head_spec(Q_HEADS + KV_HEADS),
            bias_spec(0), bias_spec(1), bias_spec(2),
        ],
        out_specs=pl.BlockSpec((None, S, HEAD_DIM), lambda b, h: (b, 0, h)),
        out_shape=jax.ShapeDtypeStruct((B, S, KV_HEADS * HEAD_DIM), BF16),
        scratch_shapes=[
            pltpu.VMEM((N_GROUPS, S, HEAD_DIM), BF16),
            pltpu.VMEM((N_GROUPS, S, HEAD_DIM), BF16),
            pltpu.VMEM((N_GROUPS, S, 2 * HEAD_DIM), BF16),
            pltpu.VMEM((S, HEAD_DIM), F32),
            pltpu.VMEM((S, HEAD_DIM), F32),
            pltpu.VMEM((S, HEAD_DIM), F32),
            pltpu.VMEM((S // ATTN_BLOCK, ATTN_BLOCK, 2 * ATTN_BLOCK), F32),
            pltpu.VMEM((S // ATTN_BLOCK, ATTN_BLOCK, 2 * ATTN_BLOCK), BF16),
            pltpu.VMEM((N_GROUPS, S, HEAD_DIM), F32),
            pltpu.VMEM((N_GROUPS, S, HEAD_DIM), F32),
            pltpu.VMEM((N_GROUPS, S, HEAD_DIM), F32),
        ],
        compiler_params=pltpu.CompilerParams(
            dimension_semantics=("parallel", "parallel"), vmem_limit_bytes=VMEM_LIMIT_BYTES),
        name="dilated_attn",
    )(qkv3, qkv3, qkv3, qkv3, qkv3, bias_tab, bias_tab, bias_tab)


def _merge_kernel(y_ref, o_ref, gate_ref, x_ref, wbr_ref, wba_ref, wout_ref, g_ref, h_ref):
    br = jnp.dot(y_ref[...], wbr_ref[...], preferred_element_type=F32)
    ba = jnp.dot(o_ref[...], wba_ref[...], preferred_element_type=F32)
    g_rnn = _sigmoid(gate_ref[:, 0:D_MODEL].astype(F32))
    g_att = _sigmoid(gate_ref[:, D_MODEL:GATE_WIDTH].astype(F32))
    merged = g_rnn * br + g_att * ba
    mix = jnp.dot(merged.astype(BF16), wout_ref[...], preferred_element_type=F32)
    h_ref[...] = x_ref[...] + _rms_norm(mix, g_ref[...])


def _merge(y2, o2, gates, x2, wbr, wba, wout, g_post):
    T = x2.shape[0]
    const = lambda i: (0, 0)
    row = lambda i: (i, 0)
    return pl.pallas_call(
        _merge_kernel,
        grid=(T // MG_TM,),
        in_specs=[
            pl.BlockSpec((MG_TM, RNN_WIDTH), row),
            pl.BlockSpec((MG_TM, KV_HEADS * HEAD_DIM), row),
            pl.BlockSpec((MG_TM, GATE_WIDTH), row),
            pl.BlockSpec((MG_TM, D_MODEL), row),
            pl.BlockSpec((RNN_WIDTH, D_MODEL), const),
            pl.BlockSpec((KV_HEADS * HEAD_DIM, D_MODEL), const),
            pl.BlockSpec((D_MODEL, D_MODEL), const),
            pl.BlockSpec((1, D_MODEL), const),
        ],
        out_specs=pl.BlockSpec((MG_TM, D_MODEL), row),
        out_shape=jax.ShapeDtypeStruct((T, D_MODEL), F32),
        compiler_params=pltpu.CompilerParams(
            dimension_semantics=("parallel",), vmem_limit_bytes=VMEM_LIMIT_BYTES),
        name="merge_out",
    )(y2, o2, gates, x2, wbr, wba, wout, g_post)


def _gelu_tanh(x):
    return 0.5 * x * (1.0 + jnp.tanh(math.sqrt(2.0 / math.pi) * (x + 0.044715 * (x * x * x))))


def _ffn_kernel(h_ref, gpre_ref, wg_ref, wu_ref, cw_ref, cb_ref, wd_ref, gpost_ref, out_ref,
                hn_scr, acc_scr, gbuf, carry_scr, *, tiles_per_seq):
    i = pl.program_id(0)
    j = pl.program_id(1)
    tm = FF_TM

    @pl.when(j == 0)
    def _():
        hn_scr[...] = _rms_norm(h_ref[...], gpre_ref[...]).astype(BF16)
        acc_scr[...] = jnp.zeros_like(acc_scr)

    hn = hn_scr[...]
    gate = jnp.dot(hn, wg_ref[...], preferred_element_type=F32)
    up = jnp.dot(hn, wu_ref[...], preferred_element_type=F32)

    seq_start = (i % tiles_per_seq) == 0
    gbuf[0:SUBLANES, :] = jnp.where(seq_start, 0.0, carry_scr[j])
    gbuf[SUBLANES:SUBLANES + tm, :] = gate
    carry_scr[j] = gate[tm - SUBLANES:tm, :]

    conv = cb_ref[...] + cw_ref[FFN_CONV - 1:FFN_CONV, :] * gate
    for k in range(FFN_CONV - 1):
        back = FFN_CONV - 1 - k
        conv = conv + cw_ref[k:k + 1, :] * gbuf[SUBLANES - back:SUBLANES - back + tm, :]
    act = (_gelu_tanh(conv) * up).astype(BF16)
    acc_scr[...] += jnp.dot(act, wd_ref[...], preferred_element_type=F32)

    @pl.when(j == pl.num_programs(1) - 1)
    def _():
        out_ref[...] = h_ref[...] + _rms_norm(acc_scr[...], gpost_ref[...])


def _ffn(h2, g_pre, wg, wu, conv_w, conv_b, wd, g_post, seq_len):
    T = h2.shape[0]
    n_j = FFN_WIDTH // FF_TF
    return pl.pallas_call(
        functools.partial(_ffn_kernel, tiles_per_seq=seq_len // FF_TM),
        grid=(T // FF_TM, n_j),
        in_specs=[
            pl.BlockSpec((FF_TM, D_MODEL), lambda i, j: (i, 0)),
            pl.BlockSpec((1, D_MODEL), lambda i, j: (0, 0)),
            pl.BlockSpec((D_MODEL, FF_TF), lambda i, j: (0, j)),
            pl.BlockSpec((D_MODEL, FF_TF), lambda i, j: (0, j)),
            pl.BlockSpec((FFN_CONV, FF_TF), lambda i, j: (0, j)),
            pl.BlockSpec((1, FF_TF), lambda i, j: (0, j)),
            pl.BlockSpec((FF_TF, D_MODEL), lambda i, j: (j, 0)),
            pl.BlockSpec((1, D_MODEL), lambda i, j: (0, 0)),
        ],
        out_specs=pl.BlockSpec((FF_TM, D_MODEL), lambda i, j: (i, 0)),
        out_shape=jax.ShapeDtypeStruct((T, D_MODEL), F32),
        scratch_shapes=[
            pltpu.VMEM((FF_TM, D_MODEL), BF16),
            pltpu.VMEM((FF_TM, D_MODEL), F32),
            pltpu.VMEM((SUBLANES + FF_TM, FF_TF), F32),
            pltpu.VMEM((n_j, SUBLANES, FF_TF), F32),
        ],
        compiler_params=pltpu.CompilerParams(
            dimension_semantics=("arbitrary", "arbitrary"), vmem_limit_bytes=VMEM_LIMIT_BYTES),
        name="ffn",
    )(h2, g_pre, wg, wu, conv_w, conv_b, wd, g_post)


def kernel(x, rel_bias, norm_mix_pre, norm_mix_post, w_in, conv_rnn_w, conv_rnn_b, w_rg_a, b_rg_a,
           w_rg_x, b_rg_x, lru_lambda, w_branch_rnn, w_branch_att, w_out, norm_ffn_pre, norm_ffn_post,
           w_ffn_gate, w_ffn_up, conv_ffn_w, conv_ffn_b, w_ffn_down):
    B, S, D = x.shape
    assert D == D_MODEL and S % FF_TM == 0 and S % RG_TS == 0 and B % RG_NB == 0
    assert S == DILATED_CONFIGS[-1][0], "attention block structure assumes window == sequence for the widest group"
    depth = w_in.shape[0]
    T = B * S
    bias_tab = _bias_tables(rel_bias)
    h = x.reshape(T, D)
    for l in range(depth):
        xr, qkv, gates = _in_proj(h, norm_mix_pre[l][None], w_in[l].astype(BF16), S)
        wax = jnp.concatenate([w_rg_a[l], w_rg_x[l]], axis=-1).astype(BF16)
        bax = jnp.stack([b_rg_a[l], b_rg_x[l]], axis=0)
        y_rnn = _rglru(xr, conv_rnn_w[l], conv_rnn_b[l][None], wax, bax, lru_lambda[l][None])
        o_att = _dilated_attention(qkv.reshape(B, S, QKV_WIDTH), bias_tab)
        h = _merge(y_rnn.reshape(T, RNN_WIDTH), o_att.reshape(T, KV_HEADS * HEAD_DIM), gates, h,
                   w_branch_rnn[l].astype(BF16), w_branch_att[l].astype(BF16), w_out[l].astype(BF16),
                   norm_mix_post[l][None])
        h = _ffn(h, norm_ffn_pre[l][None], w_ffn_gate[l].astype(BF16), w_ffn_up[l].astype(BF16),
                 conv_ffn_w[l], conv_ffn_b[l][None], w_ffn_down[l].astype(BF16), norm_ffn_post[l][None],
                 S)
    return h.reshape(B, S, D)
```

```python
import functools
import math

import numpy as np
import jax
import jax.numpy as jnp
from jax import lax
from jax.experimental import pallas as pl
from jax.experimental.pallas import tpu as pltpu

F32 = jnp.float32
BF16 = jnp.bfloat16

D_MODEL = 1024
RNN_WIDTH = 1280
RNN_BLOCKS = 10
RNN_BLOCK = 128
RNN_CONV = 4
LRU_C = 8.0
HEAD_DIM = 128
KV_HEADS = 4
DILATED_CONFIGS = ((128, 1), (512, 4), (2048, 16))
N_GROUPS = 3
Q_HEADS = 12
ATTN_BLOCK = 128
REL_BUCKETS = 32
REL_MAX_DIST = 2048
FFN_WIDTH = 3072
FFN_CONV = 3
EPS = 1e-6
QKV_WIDTH = (Q_HEADS + 2 * KV_HEADS) * HEAD_DIM
GATE_WIDTH = 2 * D_MODEL
IN_WIDTH = RNN_WIDTH + QKV_WIDTH + GATE_WIDTH

SUBLANES = 8
LANES = 128
VMEM_LIMIT_BYTES = 56 * 1024 * 1024

MASK_VALUE = -1e30
LOG2E = math.log2(math.e)
QK_SCALE_LOG2 = HEAD_DIM ** -0.5 * LOG2E

RG_TS = 64
RG_NB = 8
RG_PITCH = RG_TS + 8
RG_TC = 16
MIX_TN = 256
MG_TM = 512
FF_TM = 1024
FF_TF = 512


def _rms_norm(x, g):
    ms = jnp.mean(x * x, axis=-1, keepdims=True)
    return x * lax.rsqrt(ms + EPS) * g


def _sigmoid(x):
    return 0.5 + 0.5 * jnp.tanh(0.5 * x)


def _t5_bucket(dist):
    max_exact = REL_BUCKETS // 2
    d = np.maximum(dist, 1).astype(np.float32)
    large = max_exact + np.log(d / max_exact) / math.log(REL_MAX_DIST / max_exact) * (REL_BUCKETS - max_exact)
    large = np.minimum(large.astype(np.int32), REL_BUCKETS - 1)
    return np.where(dist < max_exact, dist, large).astype(np.int32)


def _bucket_maps():
    qi = np.arange(ATTN_BLOCK)[:, None]
    kj = np.arange(2 * ATTN_BLOCK)[None, :]
    delta = ATTN_BLOCK + qi - kj
    maps = []
    for window, dilation in DILATED_CONFIGS:
        n_back = window // dilation
        valid = (delta >= 0) & (delta <= n_back)
        bucket = _t5_bucket(np.maximum(delta, 0) * dilation)
        maps.append(np.where(valid, bucket, -1).astype(np.int32))
    return np.stack(maps, axis=0)


def _bias_kernel(rb_ref, bucket_ref, out_ref):
    head = pl.program_id(0)
    bk = bucket_ref[0]
    acc = jnp.full(bk.shape, MASK_VALUE, F32)
    for b in range(REL_BUCKETS):
        acc = jnp.where(bk == b, rb_ref[b, head] * LOG2E, acc)
    out_ref[0] = acc


def _bias_tables(rel_bias):
    buckets = jnp.asarray(_bucket_maps())
    return pl.pallas_call(
        _bias_kernel,
        grid=(Q_HEADS,),
        in_specs=[
            pl.BlockSpec(memory_space=pltpu.SMEM),
            pl.BlockSpec((1, ATTN_BLOCK, 2 * ATTN_BLOCK), lambda h: (h // KV_HEADS, 0, 0)),
        ],
        out_specs=pl.BlockSpec((1, ATTN_BLOCK, 2 * ATTN_BLOCK), lambda h: (h, 0, 0)),
        out_shape=jax.ShapeDtypeStruct((Q_HEADS, ATTN_BLOCK, 2 * ATTN_BLOCK), F32),
        name="bias_table",
    )(rel_bias, buckets)


def _rglru_chunk(n, t0, xr_scr, y_scr, carry, cw_ref, cb_ref, wax_ref, bax_ref, lam_ref):
    pitch, tc = RG_PITCH, RG_TC
    cols = slice(n * RNN_BLOCK, (n + 1) * RNN_BLOCK)
    bcast = lambda row: jnp.broadcast_to(row, (RG_NB, RNN_BLOCK))
    w = [bcast(cw_ref[k:k + 1, cols]) for k in range(RNN_CONV)]
    cb = bcast(cb_ref[:, cols])
    xs = [carry[n, k] for k in range(RNN_CONV - 1)]
    xs += [xr_scr[n, pl.ds(t0 + j, RG_NB, stride=pitch), :] for j in range(tc)]
    xc = jnp.concatenate(
        [cb + w[3] * xs[j + 3] + w[2] * xs[j + 2] + w[1] * xs[j + 1] + w[0] * xs[j] for j in range(tc)],
        axis=0)
    g = jnp.dot(xc.astype(BF16), wax_ref[n], preferred_element_type=F32)
    r = _sigmoid(g[:, 0:RNN_BLOCK] + bax_ref[0:1, cols])
    i = _sigmoid(g[:, RNN_BLOCK:2 * RNN_BLOCK] + bax_ref[1:2, cols])
    neg_lam = -lam_ref[:, cols]
    softplus = jnp.maximum(neg_lam, 0.0) + jnp.log1p(jnp.exp(-jnp.abs(neg_lam)))
    log_a = (-LRU_C * softplus) * r
    a = jnp.exp(log_a)
    th = jnp.tanh(log_a)
    u = jnp.sqrt(-2.0 * th / (1.0 - th)) * (i * xc)
    h = carry[n, RNN_CONV - 1]
    for j in range(tc):
        h = a[j * RG_NB:(j + 1) * RG_NB] * h + u[j * RG_NB:(j + 1) * RG_NB]
        y_scr[n, pl.ds(t0 + j, RG_NB, stride=pitch), :] = h
    for k in range(RNN_CONV - 1):
        carry[n, k] = xs[tc + k]
    carry[n, RNN_CONV - 1] = h


def _mix_in_kernel(x_ref, g_ref, w_ref, cw_ref, cb_ref, wax_ref, bax_ref, lam_ref,
                   y_ref, qkv_ref, gate_ref, xr_scr, y_scr, carry):
    s = pl.program_id(1)
    ts, pitch = RG_TS, RG_PITCH
    rows = RG_NB * ts

    @pl.when(s == 0)
    def _():
        carry[...] = jnp.zeros_like(carry)

    hn = _rms_norm(x_ref[...].reshape(rows, D_MODEL), g_ref[...]).astype(BF16)
    c0, c1 = RNN_WIDTH, RNN_WIDTH + QKV_WIDTH

    def project(col):
        out = jnp.dot(hn, w_ref[:, col:col + MIX_TN], preferred_element_type=F32)
        if col < c0:
            for n in range(col // RNN_BLOCK, (col + MIX_TN) // RNN_BLOCK):
                lanes = slice(n * RNN_BLOCK - col, (n + 1) * RNN_BLOCK - col)
                for b in range(RG_NB):
                    xr_scr[n, b * pitch:b * pitch + ts, :] = out[b * ts:(b + 1) * ts, lanes]
        elif col < c1:
            qkv_ref[:, :, col - c0:col - c0 + MIX_TN] = out.reshape(RG_NB, ts, MIX_TN)
        else:
            gate_ref[:, :, col - c1:col - c1 + MIX_TN] = out.astype(BF16).reshape(RG_NB, ts, MIX_TN)

    def recur(n, ci):
        _rglru_chunk(n, ci * RG_TC, xr_scr, y_scr, carry, cw_ref, cb_ref, wax_ref, bax_ref, lam_ref)
        if ci == ts // RG_TC - 1:
            for b in range(RG_NB):
                y_ref[b, :, n * RNN_BLOCK:(n + 1) * RNN_BLOCK] = (
                    y_scr[n, b * pitch:b * pitch + ts, :].astype(BF16))

    slabs_per_block = MIX_TN // RNN_BLOCK
    units = [(n, ci) for n0 in range(0, RNN_BLOCKS, slabs_per_block) for ci in range(ts // RG_TC)
             for n in range(n0, n0 + slabs_per_block)]
    cols = list(range(0, IN_WIDTH, MIX_TN))
    project(cols[0])
    done = 1
    for k, (n, ci) in enumerate(units):
        recur(n, ci)
        want = 1 + ((k + 1) * (len(cols) - 1) + len(units) - 1) // len(units)
        while done < want:
            project(cols[done])
            done += 1


def _mix_in(x3, g, w_bf, conv_w, conv_b, wax, bax, lam):
    B, S, _ = x3.shape
    const2 = lambda b, s: (0, 0)
    tile = lambda width: pl.BlockSpec((RG_NB, RG_TS, width), lambda b, s: (b, s, 0))
    scan_scratch = pltpu.VMEM((RNN_BLOCKS, RG_NB * RG_PITCH, RNN_BLOCK), F32)
    return pl.pallas_call(
        _mix_in_kernel,
        grid=(B // RG_NB, S // RG_TS),
        in_specs=[
            tile(D_MODEL),
            pl.BlockSpec((1, D_MODEL), const2),
            pl.BlockSpec((D_MODEL, IN_WIDTH), const2, pipeline_mode=pl.Buffered(1)),
            pl.BlockSpec((RNN_CONV, RNN_WIDTH), const2),
            pl.BlockSpec((1, RNN_WIDTH), const2),
            pl.BlockSpec((RNN_BLOCKS, RNN_BLOCK, 2 * RNN_BLOCK), lambda b, s: (0, 0, 0)),
            pl.BlockSpec((2, RNN_WIDTH), const2),
            pl.BlockSpec((1, RNN_WIDTH), const2),
        ],
        out_specs=[tile(RNN_WIDTH), tile(QKV_WIDTH), tile(GATE_WIDTH)],
        out_shape=[
            jax.ShapeDtypeStruct((B, S, RNN_WIDTH), BF16),
            jax.ShapeDtypeStruct((B, S, QKV_WIDTH), F32),
            jax.ShapeDtypeStruct((B, S, GATE_WIDTH), BF16),
        ],
        scratch_shapes=[
            scan_scratch, scan_scratch,
            pltpu.VMEM((RNN_BLOCKS, RNN_CONV, RG_NB, RNN_BLOCK), F32),
        ],
        compiler_params=pltpu.CompilerParams(
            dimension_semantics=("parallel", "arbitrary"), vmem_limit_bytes=VMEM_LIMIT_BYTES),
        name="mix_in",
    )(x3, g, w_bf, conv_w, conv_b, wax, bax, lam)


def _dilated_kernel(q1_ref, q2_ref, q3_ref, k_ref, v_ref, b1_ref, b2_ref, b3_ref, o_ref,
                    qp, kp, vp, tq, tk, tv, s_scr, p_scr, o_scr, m_scr, d_scr):
    S = k_ref.shape[0]
    blk = ATTN_BLOCK
    n_blocks = S // blk
    quarter = S // 4
    nt = (((1,), (1,)), ((), ()))

    def put(g, r0, n, q, k, v):
        qp[g, r0:r0 + n, :] = (q * QK_SCALE_LOG2).astype(BF16)
        kp[g, r0:r0 + n, :] = k.astype(BF16)
        vp[g, r0:r0 + n, 0:HEAD_DIM] = v.astype(BF16)

    for g in range(N_GROUPS):
        vp[g, :, HEAD_DIM:2 * HEAD_DIM] = jnp.ones((S, HEAD_DIM), BF16)
    for c in range(4):
        r = slice(c * quarter, (c + 1) * quarter)
        put(0, c * quarter, quarter, q1_ref[r, :], k_ref[r, :], v_ref[r, :])
    for c in range(4):
        sr = pl.ds(c, quarter, stride=4)
        r = slice(c * quarter, (c + 1) * quarter)
        k4, v4 = k_ref[sr, :], v_ref[sr, :]
        put(1, c * quarter, quarter, q2_ref[sr, :], k4, v4)
        tk[r, :] = k4
        tv[r, :] = v4
        tq[r, :] = q3_ref[sr, :]
    for c4 in range(4):
        for j in range(4):
            sr = pl.ds(c4 * quarter + j, blk, stride=4)
            put(2, (4 * j + c4) * blk, blk, tq[sr, :], tk[sr, :], tv[sr, :])

    def run_group(g, b_ref, dil):
        nb = S // dil // blk

        def nat_rows(p):
            c, n = divmod(p, nb)
            start = n * (blk * dil) + c
            return pl.ds(start, blk) if dil == 1 else pl.ds(start, blk, stride=dil)

        def key_rows(p):
            has_prev = p % nb != 0
            return (slice((p - 1) * blk, (p + 1) * blk) if has_prev else slice(p * blk, (p + 1) * blk),
                    slice(0, 2 * blk) if has_prev else slice(blk, 2 * blk))

        for p in range(n_blocks):
            kr, cols = key_rows(p)
            s = lax.dot_general(qp[g, p * blk:(p + 1) * blk, :], kp[g, kr, :], nt, preferred_element_type=F32)
            s_scr[p, :, cols] = s + b_ref[:, cols]
        for p in range(n_blocks):
            _, cols = key_rows(p)
            m = jnp.max(s_scr[p, :, cols], axis=-1, keepdims=True)
            m_scr[g, nat_rows(p), :] = jnp.broadcast_to(m, (blk, HEAD_DIM))
            p_scr[p, :, cols] = jnp.exp2(s_scr[p, :, cols] - m).astype(BF16)
        for p in range(n_blocks):
            kr, cols = key_rows(p)
            od = jnp.dot(p_scr[p, :, cols], vp[g, kr, :], preferred_element_type=F32)
            o_scr[g, nat_rows(p), :] = od[:, 0:HEAD_DIM]
            d_scr[g, nat_rows(p), :] = od[:, HEAD_DIM:2 * HEAD_DIM]

    run_group(0, b1_ref, DILATED_CONFIGS[0][1])
    run_group(1, b2_ref, DILATED_CONFIGS[1][1])
    run_group(2, b3_ref, DILATED_CONFIGS[2][1])

    chunk = 256

    def merge(i, carry):
        r = pl.ds(pl.multiple_of(i * chunk, chunk), chunk)
        m0, m1, m2 = m_scr[0, r, :], m_scr[1, r, :], m_scr[2, r, :]
        mx = jnp.maximum(jnp.maximum(m0, m1), m2)
        w0, w1, w2 = jnp.exp2(m0 - mx), jnp.exp2(m1 - mx), jnp.exp2(m2 - mx)
        num = w0 * o_scr[0, r, :] + w1 * o_scr[1, r, :] + w2 * o_scr[2, r, :]
        z = w0 * d_scr[0, r, :] + w1 * d_scr[1, r, :] + w2 * d_scr[2, r, :]
        o_ref[r, :] = (num / z).astype(o_ref.dtype)
        return carry

    lax.fori_loop(0, S // chunk, merge, 0)


def _dilated_attention(qkv3, bias_tab):
    B, S, _ = qkv3.shape
    head_spec = lambda col0: pl.BlockSpec((None, S, HEAD_DIM), lambda b, h: (b, 0, col0 + h))
    bias_spec = lambda g: pl.BlockSpec(
        (None, ATTN_BLOCK, 2 * ATTN_BLOCK), lambda b, h: (g * KV_HEADS + h, 0, 0))
    return pl.pallas_call(
        _dilated_kernel,
        grid=(B, KV_HEADS),
        in_specs=[
            head_spec(0), head_spec(KV_HEADS), head_spec(2 * KV_HEADS),
            head_spec(Q_HEADS), head_spec(Q_HEADS + KV_HEADS),
            bias_spec(0), bias_spec(1), bias_spec(2),
        ],
        out_specs=pl.BlockSpec((None, S, HEAD_DIM), lambda b, h: (b, 0, h)),
        out_shape=jax.ShapeDtypeStruct((B, S, KV_HEADS * HEAD_DIM), BF16),
        scratch_shapes=[
            pltpu.VMEM((N_GROUPS, S, HEAD_DIM), BF16),
            pltpu.VMEM((N_GROUPS, S, HEAD_DIM), BF16),
            pltpu.VMEM((N_GROUPS, S, 2 * HEAD_DIM), BF16),
            pltpu.VMEM((S, HEAD_DIM), F32),
            pltpu.VMEM((S, HEAD_DIM), F32),
            pltpu.VMEM((S, HEAD_DIM), F32),
            pltpu.VMEM((S // ATTN_BLOCK, ATTN_BLOCK, 2 * ATTN_BLOCK), F32),
            pltpu.VMEM((S // ATTN_BLOCK, ATTN_BLOCK, 2 * ATTN_BLOCK), BF16),
            pltpu.VMEM((N_GROUPS, S, HEAD_DIM), F32),
            pltpu.VMEM((N_GROUPS, S, HEAD_DIM), F32),
            pltpu.VMEM((N_GROUPS, S, HEAD_DIM), F32),
        ],
        compiler_params=pltpu.CompilerParams(
            dimension_semantics=("parallel", "parallel"), vmem_limit_bytes=VMEM_LIMIT_BYTES),
        name="dilated_attn",
    )(qkv3, qkv3, qkv3, qkv3, qkv3, bias_tab, bias_tab, bias_tab)


def _merge_kernel(y_ref, o_ref, gate_ref, x_ref, wbr_ref, wba_ref, wout_ref, g_ref, h_ref):
    br = jnp.dot(y_ref[...], wbr_ref[...], preferred_element_type=F32)
    ba = jnp.dot(o_ref[...], wba_ref[...], preferred_element_type=F32)
    g_rnn = _sigmoid(gate_ref[:, 0:D_MODEL].astype(F32))
    g_att = _sigmoid(gate_ref[:, D_MODEL:GATE_WIDTH].astype(F32))
    merged = g_rnn * br + g_att * ba
    mix = jnp.dot(merged.astype(BF16), wout_ref[...], preferred_element_type=F32)
    h_ref[...] = x_ref[...] + _rms_norm(mix, g_ref[...])


def _merge(y2, o2, gates, x2, wbr, wba, wout, g_post):
    T = x2.shape[0]
    const = lambda i: (0, 0)
    row = lambda i: (i, 0)
    return pl.pallas_call(
        _merge_kernel,
        grid=(T // MG_TM,),
        in_specs=[
            pl.BlockSpec((MG_TM, RNN_WIDTH), row),
            pl.BlockSpec((MG_TM, KV_HEADS * HEAD_DIM), row),
            pl.BlockSpec((MG_TM, GATE_WIDTH), row),
            pl.BlockSpec((MG_TM, D_MODEL), row),
            pl.BlockSpec((RNN_WIDTH, D_MODEL), const),
            pl.BlockSpec((KV_HEADS * HEAD_DIM, D_MODEL), const),
            pl.BlockSpec((D_MODEL, D_MODEL), const),
            pl.BlockSpec((1, D_MODEL), const),
        ],
        out_specs=pl.BlockSpec((MG_TM, D_MODEL), row),
        out_shape=jax.ShapeDtypeStruct((T, D_MODEL), F32),
        compiler_params=pltpu.CompilerParams(
            dimension_semantics=("parallel",), vmem_limit_bytes=VMEM_LIMIT_BYTES),
        name="merge_out",
    )(y2, o2, gates, x2, wbr, wba, wout, g_post)


def _gelu_tanh(x):
    return 0.5 * x * (1.0 + jnp.tanh(math.sqrt(2.0 / math.pi) * (x + 0.044715 * (x * x * x))))


def _ffn_kernel(h_ref, gpre_ref, wg_ref, wu_ref, cw_ref, cb_ref, wd_ref, gpost_ref, out_ref,
                hn_scr, acc_scr, gbuf, carry_scr, *, tiles_per_seq):
    i = pl.program_id(0)
    j = pl.program_id(1)
    tm = FF_TM

    @pl.when(j == 0)
    def _():
        hn_scr[...] = _rms_norm(h_ref[...], gpre_ref[...]).astype(BF16)
        acc_scr[...] = jnp.zeros_like(acc_scr)

    hn = hn_scr[...]
    gate = jnp.dot(hn, wg_ref[...], preferred_element_type=F32)
    up = jnp.dot(hn, wu_ref[...], preferred_element_type=F32)

    seq_start = (i % tiles_per_seq) == 0
    gbuf[0:SUBLANES, :] = jnp.where(seq_start, 0.0, carry_scr[j])
    gbuf[SUBLANES:SUBLANES + tm, :] = gate
    carry_scr[j] = gate[tm - SUBLANES:tm, :]

    conv = cb_ref[...] + cw_ref[FFN_CONV - 1:FFN_CONV, :] * gate
    for k in range(FFN_CONV - 1):
        back = FFN_CONV - 1 - k
        conv = conv + cw_ref[k:k + 1, :] * gbuf[SUBLANES - back:SUBLANES - back + tm, :]
    act = (_gelu_tanh(conv) * up).astype(BF16)
    acc_scr[...] += jnp.dot(act, wd_ref[...], preferred_element_type=F32)

    @pl.when(j == pl.num_programs(1) - 1)
    def _():
        out_ref[...] = h_ref[...] + _rms_norm(acc_scr[...], gpost_ref[...])


def _ffn(h2, g_pre, wg, wu, conv_w, conv_b, wd, g_post, seq_len):
    T = h2.shape[0]
    n_j = FFN_WIDTH // FF_TF
    return pl.pallas_call(
        functools.partial(_ffn_kernel, tiles_per_seq=seq_len // FF_TM),
        grid=(T // FF_TM, n_j),
        in_specs=[
            pl.BlockSpec((FF_TM, D_MODEL), lambda i, j: (i, 0)),
            pl.BlockSpec((1, D_MODEL), lambda i, j: (0, 0)),
            pl.BlockSpec((D_MODEL, FF_TF), lambda i, j: (0, j)),
            pl.BlockSpec((D_MODEL, FF_TF), lambda i, j: (0, j)),
            pl.BlockSpec((FFN_CONV, FF_TF), lambda i, j: (0, j)),
            pl.BlockSpec((1, FF_TF), lambda i, j: (0, j)),
            pl.BlockSpec((FF_TF, D_MODEL), lambda i, j: (j, 0)),
            pl.BlockSpec((1, D_MODEL), lambda i, j: (0, 0)),
        ],
        out_specs=pl.BlockSpec((FF_TM, D_MODEL), lambda i, j: (i, 0)),
        out_shape=jax.ShapeDtypeStruct((T, D_MODEL), F32),
        scratch_shapes=[
            pltpu.VMEM((FF_TM, D_MODEL), BF16),
            pltpu.VMEM((FF_TM, D_MODEL), F32),
            pltpu.VMEM((SUBLANES + FF_TM, FF_TF), F32),
            pltpu.VMEM((n_j, SUBLANES, FF_TF), F32),
        ],
        compiler_params=pltpu.CompilerParams(
            dimension_semantics=("arbitrary", "arbitrary"), vmem_limit_bytes=VMEM_LIMIT_BYTES),
        name="ffn",
    )(h2, g_pre, wg, wu, conv_w, conv_b, wd, g_post)


def kernel(x, rel_bias, norm_mix_pre, norm_mix_post, w_in, conv_rnn_w, conv_rnn_b, w_rg_a, b_rg_a,
           w_rg_x, b_rg_x, lru_lambda, w_branch_rnn, w_branch_att, w_out, norm_ffn_pre, norm_ffn_post,
           w_ffn_gate, w_ffn_up, conv_ffn_w, conv_ffn_b, w_ffn_down):
    B, S, D = x.shape
    assert D == D_MODEL and S % FF_TM == 0 and S % RG_TS == 0 and B % RG_NB == 0
    assert S == DILATED_CONFIGS[-1][0], "attention block structure assumes window == sequence for the widest group"
    depth = w_in.shape[0]
    T = B * S
    bias_tab = _bias_tables(rel_bias)
    h = x.reshape(T, D)
    for l in range(depth):
        wax = jnp.concatenate([w_rg_a[l], w_rg_x[l]], axis=-1).astype(BF16)
        bax = jnp.stack([b_rg_a[l], b_rg_x[l]], axis=0)
        y_rnn, qkv, gates = _mix_in(h.reshape(B, S, D), norm_mix_pre[l][None], w_in[l].astype(BF16),
                                    conv_rnn_w[l], conv_rnn_b[l][None], wax, bax, lru_lambda[l][None])
        o_att = _dilated_attention(qkv, bias_tab)
        h = _merge(y_rnn.reshape(T, RNN_WIDTH), o_att.reshape(T, KV_HEADS * HEAD_DIM),
                   gates.reshape(T, GATE_WIDTH), h,
                   w_branch_rnn[l].astype(BF16), w_branch_att[l].astype(BF16), w_out[l].astype(BF16),
                   norm_mix_post[l][None])
        h = _ffn(h, norm_ffn_pre[l][None], w_ffn_gate[l].astype(BF16), w_ffn_up[l].astype(BF16),
                 conv_ffn_w[l], conv_ffn_b[l][None], w_ffn_down[l].astype(BF16), norm_ffn_post[l][None],
                 S)
    return h.reshape(B, S, D)
```

```python
import functools
import math

import numpy as np
import jax
import jax.numpy as jnp
from jax import lax
from jax.experimental import pallas as pl
from jax.experimental.pallas import tpu as pltpu

F32 = jnp.float32
BF16 = jnp.bfloat16

D_MODEL = 1024
RNN_WIDTH = 1280
RNN_BLOCKS = 10
RNN_BLOCK = 128
RNN_CONV = 4
LRU_C = 8.0
HEAD_DIM = 128
KV_HEADS = 4
DILATED_CONFIGS = ((128, 1), (512, 4), (2048, 16))
N_GROUPS = 3
Q_HEADS = 12
ATTN_BLOCK = 128
REL_BUCKETS = 32
REL_MAX_DIST = 2048
FFN_WIDTH = 3072
FFN_CONV = 3
EPS = 1e-6
QKV_HEADS = Q_HEADS + 2 * KV_HEADS
QKV_WIDTH = QKV_HEADS * HEAD_DIM
GATE_WIDTH = 2 * D_MODEL
IN_WIDTH = RNN_WIDTH + QKV_WIDTH + GATE_WIDTH

SUBLANES = 8
LANES = 128
VMEM_LIMIT_BYTES = 56 * 1024 * 1024

MASK_VALUE = -1e30
LOG2E = math.log2(math.e)
QK_SCALE_LOG2 = HEAD_DIM ** -0.5 * LOG2E

RG_TS = 64
RG_NB = 8
RG_PITCH = RG_TS + 8
RG_TC = 16
MIX_TN = 256
MG_TM = 512
FF_TM = 1024
FF_TF = 512


def _rms_norm(x, g):
    ms = jnp.mean(x * x, axis=-1, keepdims=True)
    return x * lax.rsqrt(ms + EPS) * g


def _sigmoid(x):
    return 0.5 + 0.5 * jnp.tanh(0.5 * x)


def _t5_bucket(dist):
    max_exact = REL_BUCKETS // 2
    d = np.maximum(dist, 1).astype(np.float32)
    large = max_exact + np.log(d / max_exact) / math.log(REL_MAX_DIST / max_exact) * (REL_BUCKETS - max_exact)
    large = np.minimum(large.astype(np.int32), REL_BUCKETS - 1)
    return np.where(dist < max_exact, dist, large).astype(np.int32)


def _bucket_maps():
    qi = np.arange(ATTN_BLOCK)[:, None]
    kj = np.arange(2 * ATTN_BLOCK)[None, :]
    delta = ATTN_BLOCK + qi - kj
    maps = []
    for window, dilation in DILATED_CONFIGS:
        n_back = window // dilation
        valid = (delta >= 0) & (delta <= n_back)
        bucket = _t5_bucket(np.maximum(delta, 0) * dilation)
        maps.append(np.where(valid, bucket, -1).astype(np.int32))
    return np.stack(maps, axis=0)


def _bias_kernel(rb_ref, bucket_ref, out_ref):
    head = pl.program_id(0)
    bk = bucket_ref[0]
    acc = jnp.full(bk.shape, MASK_VALUE, F32)
    for b in range(REL_BUCKETS):
        acc = jnp.where(bk == b, rb_ref[b, head] * LOG2E, acc)
    out_ref[0] = acc


def _bias_tables(rel_bias):
    buckets = jnp.asarray(_bucket_maps())
    return pl.pallas_call(
        _bias_kernel,
        grid=(Q_HEADS,),
        in_specs=[
            pl.BlockSpec(memory_space=pltpu.SMEM),
            pl.BlockSpec((1, ATTN_BLOCK, 2 * ATTN_BLOCK), lambda h: (h // KV_HEADS, 0, 0)),
        ],
        out_specs=pl.BlockSpec((1, ATTN_BLOCK, 2 * ATTN_BLOCK), lambda h: (h, 0, 0)),
        out_shape=jax.ShapeDtypeStruct((Q_HEADS, ATTN_BLOCK, 2 * ATTN_BLOCK), F32),
        name="bias_table",
    )(rel_bias, buckets)


def _rglru_chunk(n, t0, xr_scr, y_scr, carry, cw_ref, cb_ref, wax_ref, bax_ref, lam_ref):
    pitch, tc = RG_PITCH, RG_TC
    cols = slice(n * RNN_BLOCK, (n + 1) * RNN_BLOCK)
    bcast = lambda row: jnp.broadcast_to(row, (RG_NB, RNN_BLOCK))
    w = [bcast(cw_ref[k:k + 1, cols]) for k in range(RNN_CONV)]
    cb = bcast(cb_ref[:, cols])
    xs = [carry[n, k] for k in range(RNN_CONV - 1)]
    xs += [xr_scr[n, pl.ds(t0 + j, RG_NB, stride=pitch), :] for j in range(tc)]
    xc = jnp.concatenate(
        [cb + w[3] * xs[j + 3] + w[2] * xs[j + 2] + w[1] * xs[j + 1] + w[0] * xs[j] for j in range(tc)],
        axis=0)
    g = jnp.dot(xc.astype(BF16), wax_ref[n], preferred_element_type=F32)
    r = _sigmoid(g[:, 0:RNN_BLOCK] + bax_ref[0:1, cols])
    i = _sigmoid(g[:, RNN_BLOCK:2 * RNN_BLOCK] + bax_ref[1:2, cols])
    neg_lam = -lam_ref[:, cols]
    softplus = jnp.maximum(neg_lam, 0.0) + jnp.log1p(jnp.exp(-jnp.abs(neg_lam)))
    log_a = (-LRU_C * softplus) * r
    a = jnp.exp(log_a)
    th = jnp.tanh(log_a)
    u = jnp.sqrt(-2.0 * th / (1.0 - th)) * (i * xc)
    h = carry[n, RNN_CONV - 1]
    for j in range(tc):
        h = a[j * RG_NB:(j + 1) * RG_NB] * h + u[j * RG_NB:(j + 1) * RG_NB]
        y_scr[n, pl.ds(t0 + j, RG_NB, stride=pitch), :] = h
    for k in range(RNN_CONV - 1):
        carry[n, k] = xs[tc + k]
    carry[n, RNN_CONV - 1] = h


def _mix_in_kernel(x_ref, g_ref, w_ref, cw_ref, cb_ref, wax_ref, bax_ref, lam_ref,
                   y_ref, qkv_ref, gate_ref, xr_scr, y_scr, carry):
    s = pl.program_id(1)
    ts, pitch = RG_TS, RG_PITCH
    rows = RG_NB * ts

    @pl.when(s == 0)
    def _():
        carry[...] = jnp.zeros_like(carry)

    hn = _rms_norm(x_ref[...].reshape(rows, D_MODEL), g_ref[...]).astype(BF16)
    c0, c1 = RNN_WIDTH, RNN_WIDTH + QKV_WIDTH

    def project(col):
        out = jnp.dot(hn, w_ref[:, col:col + MIX_TN], preferred_element_type=F32)
        if col < c0:
            for n in range(col // RNN_BLOCK, (col + MIX_TN) // RNN_BLOCK):
                lanes = slice(n * RNN_BLOCK - col, (n + 1) * RNN_BLOCK - col)
                for b in range(RG_NB):
                    xr_scr[n, b * pitch:b * pitch + ts, :] = out[b * ts:(b + 1) * ts, lanes]
        elif col < c1:
            for hd in range((col - c0) // HEAD_DIM, (col - c0 + MIX_TN) // HEAD_DIM):
                lanes = slice(c0 + hd * HEAD_DIM - col, c0 + (hd + 1) * HEAD_DIM - col)
                qkv_ref[:, hd, :, :] = out[:, lanes].reshape(RG_NB, ts, HEAD_DIM)
        else:
            gate_ref[:, :, col - c1:col - c1 + MIX_TN] = out.astype(BF16).reshape(RG_NB, ts, MIX_TN)

    def recur(n, ci):
        _rglru_chunk(n, ci * RG_TC, xr_scr, y_scr, carry, cw_ref, cb_ref, wax_ref, bax_ref, lam_ref)
        if ci == ts // RG_TC - 1:
            for b in range(RG_NB):
                y_ref[b, :, n * RNN_BLOCK:(n + 1) * RNN_BLOCK] = (
                    y_scr[n, b * pitch:b * pitch + ts, :].astype(BF16))

    slabs_per_block = MIX_TN // RNN_BLOCK
    units = [(n, ci) for n0 in range(0, RNN_BLOCKS, slabs_per_block) for ci in range(ts // RG_TC)
             for n in range(n0, n0 + slabs_per_block)]
    cols = list(range(0, IN_WIDTH, MIX_TN))
    project(cols[0])
    done = 1
    for k, (n, ci) in enumerate(units):
        recur(n, ci)
        want = 1 + ((k + 1) * (len(cols) - 1) + len(units) - 1) // len(units)
        while done < want:
            project(cols[done])
            done += 1


def _mix_in(x3, g, w_bf, conv_w, conv_b, wax, bax, lam):
    B, S, _ = x3.shape
    const2 = lambda b, s: (0, 0)
    tile = lambda width: pl.BlockSpec((RG_NB, RG_TS, width), lambda b, s: (b, s, 0))
    scan_scratch = pltpu.VMEM((RNN_BLOCKS, RG_NB * RG_PITCH, RNN_BLOCK), F32)
    return pl.pallas_call(
        _mix_in_kernel,
        grid=(B // RG_NB, S // RG_TS),
        in_specs=[
            tile(D_MODEL),
            pl.BlockSpec((1, D_MODEL), const2),
            pl.BlockSpec((D_MODEL, IN_WIDTH), const2, pipeline_mode=pl.Buffered(1)),
            pl.BlockSpec((RNN_CONV, RNN_WIDTH), const2),
            pl.BlockSpec((1, RNN_WIDTH), const2),
            pl.BlockSpec((RNN_BLOCKS, RNN_BLOCK, 2 * RNN_BLOCK), lambda b, s: (0, 0, 0)),
            pl.BlockSpec((2, RNN_WIDTH), const2),
            pl.BlockSpec((1, RNN_WIDTH), const2),
        ],
        out_specs=[
            tile(RNN_WIDTH),
            pl.BlockSpec((RG_NB, QKV_HEADS, RG_TS, HEAD_DIM), lambda b, s: (b, 0, s, 0)),
            tile(GATE_WIDTH),
        ],
        out_shape=[
            jax.ShapeDtypeStruct((B, S, RNN_WIDTH), BF16),
            jax.ShapeDtypeStruct((B, QKV_HEADS, S, HEAD_DIM), F32),
            jax.ShapeDtypeStruct((B, S, GATE_WIDTH), BF16),
        ],
        scratch_shapes=[
            scan_scratch, scan_scratch,
            pltpu.VMEM((RNN_BLOCKS, RNN_CONV, RG_NB, RNN_BLOCK), F32),
        ],
        compiler_params=pltpu.CompilerParams(
            dimension_semantics=("parallel", "arbitrary"), vmem_limit_bytes=VMEM_LIMIT_BYTES),
        name="mix_in",
    )(x3, g, w_bf, conv_w, conv_b, wax, bax, lam)


def _dilated_kernel(q1_ref, q2_ref, q3_ref, k_ref, v_ref, b1_ref, b2_ref, b3_ref, o_ref,
                    qp, kp, vp, tq, tk, tv, s_scr, p_scr, o_scr, m_scr, d_scr):
    S = k_ref.shape[0]
    blk = ATTN_BLOCK
    n_blocks = S // blk
    quarter = S // 4
    nt = (((1,), (1,)), ((), ()))

    def put(g, r0, n, q, k, v):
        qp[g, r0:r0 + n, :] = (q * QK_SCALE_LOG2).astype(BF16)
        kp[g, r0:r0 + n, :] = k.astype(BF16)
        vp[g, r0:r0 + n, 0:HEAD_DIM] = v.astype(BF16)

    for g in range(N_GROUPS):
        vp[g, :, HEAD_DIM:2 * HEAD_DIM] = jnp.ones((S, HEAD_DIM), BF16)
    for c in range(4):
        r = slice(c * quarter, (c + 1) * quarter)
        put(0, c * quarter, quarter, q1_ref[r, :], k_ref[r, :], v_ref[r, :])
    for c in range(4):
        sr = pl.ds(c, quarter, stride=4)
        r = slice(c * quarter, (c + 1) * quarter)
        k4, v4 = k_ref[sr, :], v_ref[sr, :]
        put(1, c * quarter, quarter, q2_ref[sr, :], k4, v4)
        tk[r, :] = k4
        tv[r, :] = v4
        tq[r, :] = q3_ref[sr, :]
    for c4 in range(4):
        for j in range(4):
            sr = pl.ds(c4 * quarter + j, blk, stride=4)
            put(2, (4 * j + c4) * blk, blk, tq[sr, :], tk[sr, :], tv[sr, :])

    def run_group(g, b_ref, dil):
        nb = S // dil // blk

        def nat_rows(p):
            c, n = divmod(p, nb)
            start = n * (blk * dil) + c
            return pl.ds(start, blk) if dil == 1 else pl.ds(start, blk, stride=dil)

        def key_rows(p):
            has_prev = p % nb != 0
            return (slice((p - 1) * blk, (p + 1) * blk) if has_prev else slice(p * blk, (p + 1) * blk),
                    slice(0, 2 * blk) if has_prev else slice(blk, 2 * blk))

        for p in range(n_blocks):
            kr, cols = key_rows(p)
            s = lax.dot_general(qp[g, p * blk:(p + 1) * blk, :], kp[g, kr, :], nt, preferred_element_type=F32)
            s_scr[p, :, cols] = s + b_ref[:, cols]
        for p in range(n_blocks):
            _, cols = key_rows(p)
            m = jnp.max(s_scr[p, :, cols], axis=-1, keepdims=True)
            m_scr[g, nat_rows(p), :] = jnp.broadcast_to(m, (blk, HEAD_DIM))
            p_scr[p, :, cols] = jnp.exp2(s_scr[p, :, cols] - m).astype(BF16)
        for p in range(n_blocks):
            kr, cols = key_rows(p)
            od = jnp.dot(p_scr[p, :, cols], vp[g, kr, :], preferred_element_type=F32)
            o_scr[g, nat_rows(p), :] = od[:, 0:HEAD_DIM]
            d_scr[g, nat_rows(p), :] = od[:, HEAD_DIM:2 * HEAD_DIM]

    run_group(0, b1_ref, DILATED_CONFIGS[0][1])
    run_group(1, b2_ref, DILATED_CONFIGS[1][1])
    run_group(2, b3_ref, DILATED_CONFIGS[2][1])

    chunk = 256

    def merge(i, carry):
        r = pl.ds(pl.multiple_of(i * chunk, chunk), chunk)
        m0, m1, m2 = m_scr[0, r, :], m_scr[1, r, :], m_scr[2, r, :]
        mx = jnp.maximum(jnp.maximum(m0, m1), m2)
        w0, w1, w2 = jnp.exp2(m0 - mx), jnp.exp2(m1 - mx), jnp.exp2(m2 - mx)
        num = w0 * o_scr[0, r, :] + w1 * o_scr[1, r, :] + w2 * o_scr[2, r, :]
        z = w0 * d_scr[0, r, :] + w1 * d_scr[1, r, :] + w2 * d_scr[2, r, :]
        o_ref[r, :] = (num / z).astype(o_ref.dtype)
        return carry

    lax.fori_loop(0, S // chunk, merge, 0)


def _dilated_attention(qkv3, bias_tab):
    B, _, S, _ = qkv3.shape
    head_spec = lambda col0: pl.BlockSpec((None, None, S, HEAD_DIM), lambda b, h: (b, col0 + h, 0, 0))
    bias_spec = lambda g: pl.BlockSpec(
        (None, ATTN_BLOCK, 2 * ATTN_BLOCK), lambda b, h: (g * KV_HEADS + h, 0, 0))
    return pl.pallas_call(
        _dilated_kernel,
        grid=(B, KV_HEADS),
        in_specs=[
            head_spec(0), head_spec(KV_HEADS), head_spec(2 * KV_HEADS),
            head_spec(Q_HEADS), head_spec(Q_HEADS + KV_HEADS),
            bias_spec(0), bias_spec(1), bias_spec(2),
        ],
        out_specs=pl.BlockSpec((None, S, HEAD_DIM), lambda b, h: (b, 0, h)),
        out_shape=jax.ShapeDtypeStruct((B, S, KV_HEADS * HEAD_DIM), BF16),
        scratch_shapes=[
            pltpu.VMEM((N_GROUPS, S, HEAD_DIM), BF16),
            pltpu.VMEM((N_GROUPS, S, HEAD_DIM), BF16),
            pltpu.VMEM((N_GROUPS, S, 2 * HEAD_DIM), BF16),
            pltpu.VMEM((S, HEAD_DIM), F32),
            pltpu.VMEM((S, HEAD_DIM), F32),
            pltpu.VMEM((S, HEAD_DIM), F32),
            pltpu.VMEM((S // ATTN_BLOCK, ATTN_BLOCK, 2 * ATTN_BLOCK), F32),
            pltpu.VMEM((S // ATTN_BLOCK, ATTN_BLOCK, 2 * ATTN_BLOCK), BF16),
            pltpu.VMEM((N_GROUPS, S, HEAD_DIM), F32),
            pltpu.VMEM((N_GROUPS, S, HEAD_DIM), F32),
            pltpu.VMEM((N_GROUPS, S, HEAD_DIM), F32),
        ],
        compiler_params=pltpu.CompilerParams(
            dimension_semantics=("parallel", "parallel"), vmem_limit_bytes=VMEM_LIMIT_BYTES),
        name="dilated_attn",
    )(qkv3, qkv3, qkv3, qkv3, qkv3, bias_tab, bias_tab, bias_tab)


def _merge_kernel(y_ref, o_ref, gate_ref, x_ref, wbr_ref, wba_ref, wout_ref, g_ref, h_ref):
    br = jnp.dot(y_ref[...], wbr_ref[...], preferred_element_type=F32)
    ba = jnp.dot(o_ref[...], wba_ref[...], preferred_element_type=F32)
    g_rnn = _sigmoid(gate_ref[:, 0:D_MODEL].astype(F32))
    g_att = _sigmoid(gate_ref[:, D_MODEL:GATE_WIDTH].astype(F32))
    merged = g_rnn * br + g_att * ba
    mix = jnp.dot(merged.astype(BF16), wout_ref[...], preferred_element_type=F32)
    h_ref[...] = x_ref[...] + _rms_norm(mix, g_ref[...])


def _merge(y2, o2, gates, x2, wbr, wba, wout, g_post):
    T = x2.shape[0]
    const = lambda i: (0, 0)
    row = lambda i: (i, 0)
    return pl.pallas_call(
        _merge_kernel,
        grid=(T // MG_TM,),
        in_specs=[
            pl.BlockSpec((MG_TM, RNN_WIDTH), row),
            pl.BlockSpec((MG_TM, KV_HEADS * HEAD_DIM), row),
            pl.BlockSpec((MG_TM, GATE_WIDTH), row),
            pl.BlockSpec((MG_TM, D_MODEL), row),
            pl.BlockSpec((RNN_WIDTH, D_MODEL), const),
            pl.BlockSpec((KV_HEADS * HEAD_DIM, D_MODEL), const),
            pl.BlockSpec((D_MODEL, D_MODEL), const),
            pl.BlockSpec((1, D_MODEL), const),
        ],
        out_specs=pl.BlockSpec((MG_TM, D_MODEL), row),
        out_shape=jax.ShapeDtypeStruct((T, D_MODEL), F32),
        compiler_params=pltpu.CompilerParams(
            dimension_semantics=("parallel",), vmem_limit_bytes=VMEM_LIMIT_BYTES),
        name="merge_out",
    )(y2, o2, gates, x2, wbr, wba, wout, g_post)


def _gelu_tanh(x):
    return 0.5 * x * (1.0 + jnp.tanh(math.sqrt(2.0 / math.pi) * (x + 0.044715 * (x * x * x))))


def _ffn_kernel(h_ref, gpre_ref, wg_ref, wu_ref, cw_ref, cb_ref, wd_ref, gpost_ref, out_ref,
                hn_scr, acc_scr, gbuf, carry_scr, *, tiles_per_seq):
    i = pl.program_id(0)
    j = pl.program_id(1)
    tm = FF_TM

    @pl.when(j == 0)
    def _():
        hn_scr[...] = _rms_norm(h_ref[...], gpre_ref[...]).astype(BF16)
        acc_scr[...] = jnp.zeros_like(acc_scr)

    hn = hn_scr[...]
    gate = jnp.dot(hn, wg_ref[...], preferred_element_type=F32)
    up = jnp.dot(hn, wu_ref[...], preferred_element_type=F32)

    seq_start = (i % tiles_per_seq) == 0
    gbuf[0:SUBLANES, :] = jnp.where(seq_start, 0.0, carry_scr[j])
    gbuf[SUBLANES:SUBLANES + tm, :] = gate
    carry_scr[j] = gate[tm - SUBLANES:tm, :]

    conv = cb_ref[...] + cw_ref[FFN_CONV - 1:FFN_CONV, :] * gate
    for k in range(FFN_CONV - 1):
        back = FFN_CONV - 1 - k
        conv = conv + cw_ref[k:k + 1, :] * gbuf[SUBLANES - back:SUBLANES - back + tm, :]
    act = (_gelu_tanh(conv) * up).astype(BF16)
    acc_scr[...] += jnp.dot(act, wd_ref[...], preferred_element_type=F32)

    @pl.when(j == pl.num_programs(1) - 1)
    def _():
        out_ref[...] = h_ref[...] + _rms_norm(acc_scr[...], gpost_ref[...])


def _ffn(h2, g_pre, wg, wu, conv_w, conv_b, wd, g_post, seq_len):
    T = h2.shape[0]
    n_j = FFN_WIDTH // FF_TF
    return pl.pallas_call(
        functools.partial(_ffn_kernel, tiles_per_seq=seq_len // FF_TM),
        grid=(T // FF_TM, n_j),
        in_specs=[
            pl.BlockSpec((FF_TM, D_MODEL), lambda i, j: (i, 0)),
            pl.BlockSpec((1, D_MODEL), lambda i, j: (0, 0)),
            pl.BlockSpec((D_MODEL, FF_TF), lambda i, j: (0, j)),
            pl.BlockSpec((D_MODEL, FF_TF), lambda i, j: (0, j)),
            pl.BlockSpec((FFN_CONV, FF_TF), lambda i, j: (0, j)),
            pl.BlockSpec((1, FF_TF), lambda i, j: (0, j)),
            pl.BlockSpec((FF_TF, D_MODEL), lambda i, j: (j, 0)),
            pl.BlockSpec((1, D_MODEL), lambda i, j: (0, 0)),
        ],
        out_specs=pl.BlockSpec((FF_TM, D_MODEL), lambda i, j: (i, 0)),
        out_shape=jax.ShapeDtypeStruct((T, D_MODEL), F32),
        scratch_shapes=[
            pltpu.VMEM((FF_TM, D_MODEL), BF16),
            pltpu.VMEM((FF_TM, D_MODEL), F32),
            pltpu.VMEM((SUBLANES + FF_TM, FF_TF), F32),
            pltpu.VMEM((n_j, SUBLANES, FF_TF), F32),
        ],
        compiler_params=pltpu.CompilerParams(
            dimension_semantics=("arbitrary", "arbitrary"), vmem_limit_bytes=VMEM_LIMIT_BYTES),
        name="ffn",
    )(h2, g_pre, wg, wu, conv_w, conv_b, wd, g_post)


def kernel(x, rel_bias, norm_mix_pre, norm_mix_post, w_in, conv_rnn_w, conv_rnn_b, w_rg_a, b_rg_a,
           w_rg_x, b_rg_x, lru_lambda, w_branch_rnn, w_branch_att, w_out, norm_ffn_pre, norm_ffn_post,
           w_ffn_gate, w_ffn_up, conv_ffn_w, conv_ffn_b, w_ffn_down):
    B, S, D = x.shape
    assert D == D_MODEL and S % FF_TM == 0 and S % RG_TS == 0 and B % RG_NB == 0
    assert S == DILATED_CONFIGS[-1][0], "attention block structure assumes window == sequence for the widest group"
    depth = w_in.shape[0]
    T = B * S
    bias_tab = _bias_tables(rel_bias)
    h = x.reshape(T, D)
    for l in range(depth):
        wax = jnp.concatenate([w_rg_a[l], w_rg_x[l]], axis=-1).astype(BF16)
        bax = jnp.stack([b_rg_a[l], b_rg_x[l]], axis=0)
        y_rnn, qkv, gates = _mix_in(h.reshape(B, S, D), norm_mix_pre[l][None], w_in[l].astype(BF16),
                                    conv_rnn_w[l], conv_rnn_b[l][None], wax, bax, lru_lambda[l][None])
        o_att = _dilated_attention(qkv, bias_tab)
        h = _merge(y_rnn.reshape(T, RNN_WIDTH), o_att.reshape(T, KV_HEADS * HEAD_DIM),
                   gates.reshape(T, GATE_WIDTH), h,
                   w_branch_rnn[l].astype(BF16), w_branch_att[l].astype(BF16), w_out[l].astype(BF16),
                   norm_mix_post[l][None])
        h = _ffn(h, norm_ffn_pre[l][None], w_ffn_gate[l].astype(BF16), w_ffn_up[l].astype(BF16),
                 conv_ffn_w[l], conv_ffn_b[l][None], w_ffn_down[l].astype(BF16), norm_ffn_post[l][None],
                 S)
    return h.reshape(B, S, D)
```

```python
import functools
import math

import numpy as np
import jax
import jax.numpy as jnp
from jax import lax
from jax.experimental import pallas as pl
from jax.experimental.pallas import tpu as pltpu

F32 = jnp.float32
BF16 = jnp.bfloat16

D_MODEL = 1024
RNN_WIDTH = 1280
RNN_BLOCKS = 10
RNN_BLOCK = 128
RNN_CONV = 4
LRU_C = 8.0
HEAD_DIM = 128
KV_HEADS = 4
DILATED_CONFIGS = ((128, 1), (512, 4), (2048, 16))
N_GROUPS = 3
Q_HEADS = 12
ATTN_BLOCK = 128
REL_BUCKETS = 32
REL_MAX_DIST = 2048
FFN_WIDTH = 3072
FFN_CONV = 3
EPS = 1e-6
QKV_HEADS = Q_HEADS + 2 * KV_HEADS
QKV_WIDTH = QKV_HEADS * HEAD_DIM
GATE_WIDTH = 2 * D_MODEL
IN_WIDTH = RNN_WIDTH + QKV_WIDTH + GATE_WIDTH

SUBLANES = 8
LANES = 128
VMEM_LIMIT_BYTES = 56 * 1024 * 1024

MASK_VALUE = -1e30
LOG2E = math.log2(math.e)
QK_SCALE_LOG2 = HEAD_DIM ** -0.5 * LOG2E

RG_TS = 64
RG_NB = 8
RG_PITCH = RG_TS + 8
RG_TC = 16
MIX_TN = 256
MG_TM = 512
FF_TM = 1024
FF_TF = 1024


def _rms_norm(x, g):
    ms = jnp.mean(x * x, axis=-1, keepdims=True)
    return x * lax.rsqrt(ms + EPS) * g


def _sigmoid(x):
    return 0.5 + 0.5 * jnp.tanh(0.5 * x)


def _t5_bucket(dist):
    max_exact = REL_BUCKETS // 2
    d = np.maximum(dist, 1).astype(np.float32)
    large = max_exact + np.log(d / max_exact) / math.log(REL_MAX_DIST / max_exact) * (REL_BUCKETS - max_exact)
    large = np.minimum(large.astype(np.int32), REL_BUCKETS - 1)
    return np.where(dist < max_exact, dist, large).astype(np.int32)


def _bucket_maps():
    qi = np.arange(ATTN_BLOCK)[:, None]
    kj = np.arange(2 * ATTN_BLOCK)[None, :]
    delta = ATTN_BLOCK + qi - kj
    maps = []
    for window, dilation in DILATED_CONFIGS:
        n_back = window // dilation
        valid = (delta >= 0) & (delta <= n_back)
        bucket = _t5_bucket(np.maximum(delta, 0) * dilation)
        maps.append(np.where(valid, bucket, -1).astype(np.int32))
    return np.stack(maps, axis=0)


def _bias_kernel(rb_ref, bucket_ref, out_ref):
    head = pl.program_id(0)
    bk = bucket_ref[0]
    acc = jnp.full(bk.shape, MASK_VALUE, F32)
    for b in range(REL_BUCKETS):
        acc = jnp.where(bk == b, rb_ref[b, head] * LOG2E, acc)
    out_ref[0] = acc


def _bias_tables(rel_bias):
    buckets = jnp.asarray(_bucket_maps())
    return pl.pallas_call(
        _bias_kernel,
        grid=(Q_HEADS,),
        in_specs=[
            pl.BlockSpec(memory_space=pltpu.SMEM),
            pl.BlockSpec((1, ATTN_BLOCK, 2 * ATTN_BLOCK), lambda h: (h // KV_HEADS, 0, 0)),
        ],
        out_specs=pl.BlockSpec((1, ATTN_BLOCK, 2 * ATTN_BLOCK), lambda h: (h, 0, 0)),
        out_shape=jax.ShapeDtypeStruct((Q_HEADS, ATTN_BLOCK, 2 * ATTN_BLOCK), F32),
        name="bias_table",
    )(rel_bias, buckets)


def _rglru_chunk(n, t0, xr_scr, y_scr, carry, cw_ref, cb_ref, wax_ref, bax_ref, lam_ref):
    pitch, tc = RG_PITCH, RG_TC
    cols = slice(n * RNN_BLOCK, (n + 1) * RNN_BLOCK)
    bcast = lambda row: jnp.broadcast_to(row, (RG_NB, RNN_BLOCK))
    w = [bcast(cw_ref[k:k + 1, cols]) for k in range(RNN_CONV)]
    cb = bcast(cb_ref[:, cols])
    xs = [carry[n, k] for k in range(RNN_CONV - 1)]
    xs += [xr_scr[n, pl.ds(t0 + j, RG_NB, stride=pitch), :] for j in range(tc)]
    xc = jnp.concatenate(
        [cb + w[3] * xs[j + 3] + w[2] * xs[j + 2] + w[1] * xs[j + 1] + w[0] * xs[j] for j in range(tc)],
        axis=0)
    g = jnp.dot(xc.astype(BF16), wax_ref[n], preferred_element_type=F32)
    r = _sigmoid(g[:, 0:RNN_BLOCK] + bax_ref[0:1, cols])
    i = _sigmoid(g[:, RNN_BLOCK:2 * RNN_BLOCK] + bax_ref[1:2, cols])
    neg_lam = -lam_ref[:, cols]
    softplus = jnp.maximum(neg_lam, 0.0) + jnp.log1p(jnp.exp(-jnp.abs(neg_lam)))
    log_a = (-LRU_C * softplus) * r
    a = jnp.exp(log_a)
    th = jnp.tanh(log_a)
    u = jnp.sqrt(-2.0 * th / (1.0 - th)) * (i * xc)
    h = carry[n, RNN_CONV - 1]
    for j in range(tc):
        h = a[j * RG_NB:(j + 1) * RG_NB] * h + u[j * RG_NB:(j + 1) * RG_NB]
        y_scr[n, pl.ds(t0 + j, RG_NB, stride=pitch), :] = h
    for k in range(RNN_CONV - 1):
        carry[n, k] = xs[tc + k]
    carry[n, RNN_CONV - 1] = h


def _mix_in_kernel(x_ref, g_ref, w_ref, cw_ref, cb_ref, wax_ref, bax_ref, lam_ref,
                   y_ref, qkv_ref, gate_ref, xr_scr, y_scr, carry):
    s = pl.program_id(1)
    ts, pitch = RG_TS, RG_PITCH
    rows = RG_NB * ts

    @pl.when(s == 0)
    def _():
        carry[...] = jnp.zeros_like(carry)

    hn = _rms_norm(x_ref[...].reshape(rows, D_MODEL), g_ref[...]).astype(BF16)
    c0, c1 = RNN_WIDTH, RNN_WIDTH + QKV_WIDTH

    def project(col):
        out = jnp.dot(hn, w_ref[:, col:col + MIX_TN], preferred_element_type=F32)
        if col < c0:
            for n in range(col // RNN_BLOCK, (col + MIX_TN) // RNN_BLOCK):
                lanes = slice(n * RNN_BLOCK - col, (n + 1) * RNN_BLOCK - col)
                for b in range(RG_NB):
                    xr_scr[n, b * pitch:b * pitch + ts, :] = out[b * ts:(b + 1) * ts, lanes]
        elif col < c1:
            for hd in range((col - c0) // HEAD_DIM, (col - c0 + MIX_TN) // HEAD_DIM):
                lanes = slice(c0 + hd * HEAD_DIM - col, c0 + (hd + 1) * HEAD_DIM - col)
                qkv_ref[:, hd, :, :] = out[:, lanes].reshape(RG_NB, ts, HEAD_DIM)
        else:
            gate_ref[:, :, col - c1:col - c1 + MIX_TN] = out.astype(BF16).reshape(RG_NB, ts, MIX_TN)

    def recur(n, ci):
        _rglru_chunk(n, ci * RG_TC, xr_scr, y_scr, carry, cw_ref, cb_ref, wax_ref, bax_ref, lam_ref)
        if ci == ts // RG_TC - 1:
            for b in range(RG_NB):
                y_ref[b, :, n * RNN_BLOCK:(n + 1) * RNN_BLOCK] = (
                    y_scr[n, b * pitch:b * pitch + ts, :].astype(BF16))

    slabs_per_block = MIX_TN // RNN_BLOCK
    units = [(n, ci) for n0 in range(0, RNN_BLOCKS, slabs_per_block) for ci in range(ts // RG_TC)
             for n in range(n0, n0 + slabs_per_block)]
    cols = list(range(0, IN_WIDTH, MIX_TN))
    project(cols[0])
    done = 1
    for k, (n, ci) in enumerate(units):
        recur(n, ci)
        want = 1 + ((k + 1) * (len(cols) - 1) + len(units) - 1) // len(units)
        while done < want:
            project(cols[done])
            done += 1


def _mix_in(x3, g, w_bf, conv_w, conv_b, wax, bax, lam):
    B, S, _ = x3.shape
    const2 = lambda b, s: (0, 0)
    tile = lambda width: pl.BlockSpec((RG_NB, RG_TS, width), lambda b, s: (b, s, 0))
    scan_scratch = pltpu.VMEM((RNN_BLOCKS, RG_NB * RG_PITCH, RNN_BLOCK), F32)
    return pl.pallas_call(
        _mix_in_kernel,
        grid=(B // RG_NB, S // RG_TS),
        in_specs=[
            tile(D_MODEL),
            pl.BlockSpec((1, D_MODEL), const2),
            pl.BlockSpec((D_MODEL, IN_WIDTH), const2, pipeline_mode=pl.Buffered(1)),
            pl.BlockSpec((RNN_CONV, RNN_WIDTH), const2),
            pl.BlockSpec((1, RNN_WIDTH), const2),
            pl.BlockSpec((RNN_BLOCKS, RNN_BLOCK, 2 * RNN_BLOCK), lambda b, s: (0, 0, 0)),
            pl.BlockSpec((2, RNN_WIDTH), const2),
            pl.BlockSpec((1, RNN_WIDTH), const2),
        ],
        out_specs=[
            tile(RNN_WIDTH),
            pl.BlockSpec((RG_NB, QKV_HEADS, RG_TS, HEAD_DIM), lambda b, s: (b, 0, s, 0)),
            tile(GATE_WIDTH),
        ],
        out_shape=[
            jax.ShapeDtypeStruct((B, S, RNN_WIDTH), BF16),
            jax.ShapeDtypeStruct((B, QKV_HEADS, S, HEAD_DIM), F32),
            jax.ShapeDtypeStruct((B, S, GATE_WIDTH), BF16),
        ],
        scratch_shapes=[
            scan_scratch, scan_scratch,
            pltpu.VMEM((RNN_BLOCKS, RNN_CONV, RG_NB, RNN_BLOCK), F32),
        ],
        compiler_params=pltpu.CompilerParams(
            dimension_semantics=("parallel", "arbitrary"), vmem_limit_bytes=VMEM_LIMIT_BYTES),
        name="mix_in",
    )(x3, g, w_bf, conv_w, conv_b, wax, bax, lam)


def _dilated_kernel(q1_ref, q2_ref, q3_ref, k_ref, v_ref, b1_ref, b2_ref, b3_ref, o_ref,
                    qp, kp, vp, tq, tk, tv, s_scr, p_scr, o_scr, m_scr, d_scr):
    S = k_ref.shape[0]
    blk = ATTN_BLOCK
    n_blocks = S // blk
    quarter = S // 4
    nt = (((1,), (1,)), ((), ()))

    def put(g, r0, n, q, k, v):
        qp[g, r0:r0 + n, :] = (q * QK_SCALE_LOG2).astype(BF16)
        kp[g, r0:r0 + n, :] = k.astype(BF16)
        vp[g, r0:r0 + n, 0:HEAD_DIM] = v.astype(BF16)

    for g in range(N_GROUPS):
        vp[g, :, HEAD_DIM:2 * HEAD_DIM] = jnp.ones((S, HEAD_DIM), BF16)
    for c in range(4):
        r = slice(c * quarter, (c + 1) * quarter)
        put(0, c * quarter, quarter, q1_ref[r, :], k_ref[r, :], v_ref[r, :])
    for c in range(4):
        sr = pl.ds(c, quarter, stride=4)
        r = slice(c * quarter, (c + 1) * quarter)
        k4, v4 = k_ref[sr, :], v_ref[sr, :]
        put(1, c * quarter, quarter, q2_ref[sr, :], k4, v4)
        tk[r, :] = k4
        tv[r, :] = v4
        tq[r, :] = q3_ref[sr, :]
    for c4 in range(4):
        for j in range(4):
            sr = pl.ds(c4 * quarter + j, blk, stride=4)
            put(2, (4 * j + c4) * blk, blk, tq[sr, :], tk[sr, :], tv[sr, :])

    def run_group(g, b_ref, dil):
        nb = S // dil // blk

        def nat_rows(p):
            c, n = divmod(p, nb)
            start = n * (blk * dil) + c
            return pl.ds(start, blk) if dil == 1 else pl.ds(start, blk, stride=dil)

        def key_rows(p):
            has_prev = p % nb != 0
            return (slice((p - 1) * blk, (p + 1) * blk) if has_prev else slice(p * blk, (p + 1) * blk),
                    slice(0, 2 * blk) if has_prev else slice(blk, 2 * blk))

        for p in range(n_blocks):
            kr, cols = key_rows(p)
            s = lax.dot_general(qp[g, p * blk:(p + 1) * blk, :], kp[g, kr, :], nt, preferred_element_type=F32)
            s_scr[p, :, cols] = s + b_ref[:, cols]
        for p in range(n_blocks):
            _, cols = key_rows(p)
            m = jnp.max(s_scr[p, :, cols], axis=-1, keepdims=True)
            m_scr[g, nat_rows(p), :] = jnp.broadcast_to(m, (blk, HEAD_DIM))
            p_scr[p, :, cols] = jnp.exp2(s_scr[p, :, cols] - m).astype(BF16)
        for p in range(n_blocks):
            kr, cols = key_rows(p)
            od = jnp.dot(p_scr[p, :, cols], vp[g, kr, :], preferred_element_type=F32)
            o_scr[g, nat_rows(p), :] = od[:, 0:HEAD_DIM]
            d_scr[g, nat_rows(p), :] = od[:, HEAD_DIM:2 * HEAD_DIM]

    run_group(0, b1_ref, DILATED_CONFIGS[0][1])
    run_group(1, b2_ref, DILATED_CONFIGS[1][1])
    run_group(2, b3_ref, DILATED_CONFIGS[2][1])

    chunk = 256

    def merge(i, carry):
        r = pl.ds(pl.multiple_of(i * chunk, chunk), chunk)
        m0, m1, m2 = m_scr[0, r, :], m_scr[1, r, :], m_scr[2, r, :]
        mx = jnp.maximum(jnp.maximum(m0, m1), m2)
        w0, w1, w2 = jnp.exp2(m0 - mx), jnp.exp2(m1 - mx), jnp.exp2(m2 - mx)
        num = w0 * o_scr[0, r, :] + w1 * o_scr[1, r, :] + w2 * o_scr[2, r, :]
        z = w0 * d_scr[0, r, :] + w1 * d_scr[1, r, :] + w2 * d_scr[2, r, :]
        o_ref[r, :] = (num / z).astype(o_ref.dtype)
        return carry

    lax.fori_loop(0, S // chunk, merge, 0)


def _dilated_attention(qkv3, bias_tab):
    B, _, S, _ = qkv3.shape
    head_spec = lambda col0: pl.BlockSpec((None, None, S, HEAD_DIM), lambda b, h: (b, col0 + h, 0, 0))
    bias_spec = lambda g: pl.BlockSpec(
        (None, ATTN_BLOCK, 2 * ATTN_BLOCK), lambda b, h: (g * KV_HEADS + h, 0, 0))
    return pl.pallas_call(
        _dilated_kernel,
        grid=(B, KV_HEADS),
        in_specs=[
            head_spec(0), head_spec(KV_HEADS), head_spec(2 * KV_HEADS),
            head_spec(Q_HEADS), head_spec(Q_HEADS + KV_HEADS),
            bias_spec(0), bias_spec(1), bias_spec(2),
        ],
        out_specs=pl.BlockSpec((None, S, HEAD_DIM), lambda b, h: (b, 0, h)),
        out_shape=jax.ShapeDtypeStruct((B, S, KV_HEADS * HEAD_DIM), BF16),
        scratch_shapes=[
            pltpu.VMEM((N_GROUPS, S, HEAD_DIM), BF16),
            pltpu.VMEM((N_GROUPS, S, HEAD_DIM), BF16),
            pltpu.VMEM((N_GROUPS, S, 2 * HEAD_DIM), BF16),
            pltpu.VMEM((S, HEAD_DIM), F32),
            pltpu.VMEM((S, HEAD_DIM), F32),
            pltpu.VMEM((S, HEAD_DIM), F32),
            pltpu.VMEM((S // ATTN_BLOCK, ATTN_BLOCK, 2 * ATTN_BLOCK), F32),
            pltpu.VMEM((S // ATTN_BLOCK, ATTN_BLOCK, 2 * ATTN_BLOCK), BF16),
            pltpu.VMEM((N_GROUPS, S, HEAD_DIM), F32),
            pltpu.VMEM((N_GROUPS, S, HEAD_DIM), F32),
            pltpu.VMEM((N_GROUPS, S, HEAD_DIM), F32),
        ],
        compiler_params=pltpu.CompilerParams(
            dimension_semantics=("parallel", "parallel"), vmem_limit_bytes=VMEM_LIMIT_BYTES),
        name="dilated_attn",
    )(qkv3, qkv3, qkv3, qkv3, qkv3, bias_tab, bias_tab, bias_tab)


def _merge_kernel(y_ref, o_ref, gate_ref, x_ref, wbr_ref, wba_ref, wout_ref, g_ref, h_ref):
    br = jnp.dot(y_ref[...], wbr_ref[...], preferred_element_type=F32)
    ba = jnp.dot(o_ref[...], wba_ref[...], preferred_element_type=F32)
    g_rnn = _sigmoid(gate_ref[:, 0:D_MODEL].astype(F32))
    g_att = _sigmoid(gate_ref[:, D_MODEL:GATE_WIDTH].astype(F32))
    merged = g_rnn * br + g_att * ba
    mix = jnp.dot(merged.astype(BF16), wout_ref[...], preferred_element_type=F32)
    h_ref[...] = x_ref[...] + _rms_norm(mix, g_ref[...])


def _merge(y2, o2, gates, x2, wbr, wba, wout, g_post):
    T = x2.shape[0]
    const = lambda i: (0, 0)
    row = lambda i: (i, 0)
    return pl.pallas_call(
        _merge_kernel,
        grid=(T // MG_TM,),
        in_specs=[
            pl.BlockSpec((MG_TM, RNN_WIDTH), row),
            pl.BlockSpec((MG_TM, KV_HEADS * HEAD_DIM), row),
            pl.BlockSpec((MG_TM, GATE_WIDTH), row),
            pl.BlockSpec((MG_TM, D_MODEL), row),
            pl.BlockSpec((RNN_WIDTH, D_MODEL), const),
            pl.BlockSpec((KV_HEADS * HEAD_DIM, D_MODEL), const),
            pl.BlockSpec((D_MODEL, D_MODEL), const),
            pl.BlockSpec((1, D_MODEL), const),
        ],
        out_specs=pl.BlockSpec((MG_TM, D_MODEL), row),
        out_shape=jax.ShapeDtypeStruct((T, D_MODEL), F32),
        compiler_params=pltpu.CompilerParams(
            dimension_semantics=("parallel",), vmem_limit_bytes=VMEM_LIMIT_BYTES),
        name="merge_out",
    )(y2, o2, gates, x2, wbr, wba, wout, g_post)


def _gelu_tanh(x):
    return 0.5 * x * (1.0 + jnp.tanh(math.sqrt(2.0 / math.pi) * (x + 0.044715 * (x * x * x))))


def _ffn_kernel(h_ref, gpre_ref, wg_ref, wu_ref, cw_ref, cb_ref, wd_ref, gpost_ref, out_ref,
                hn_scr, acc_scr, gbuf, carry_scr, *, tiles_per_seq):
    i = pl.program_id(0)
    j = pl.program_id(1)
    tm = FF_TM

    @pl.when(j == 0)
    def _():
        hn_scr[...] = _rms_norm(h_ref[...], gpre_ref[...]).astype(BF16)
        acc_scr[...] = jnp.zeros_like(acc_scr)

    hn = hn_scr[...]
    gate = jnp.dot(hn, wg_ref[...], preferred_element_type=F32)
    up = jnp.dot(hn, wu_ref[...], preferred_element_type=F32)

    seq_start = (i % tiles_per_seq) == 0
    gbuf[0:SUBLANES, :] = jnp.where(seq_start, 0.0, carry_scr[j])
    gbuf[SUBLANES:SUBLANES + tm, :] = gate
    carry_scr[j] = gate[tm - SUBLANES:tm, :]

    conv = cb_ref[...] + cw_ref[FFN_CONV - 1:FFN_CONV, :] * gate
    for k in range(FFN_CONV - 1):
        back = FFN_CONV - 1 - k
        conv = conv + cw_ref[k:k + 1, :] * gbuf[SUBLANES - back:SUBLANES - back + tm, :]
    act = (_gelu_tanh(conv) * up).astype(BF16)
    acc_scr[...] += jnp.dot(act, wd_ref[...], preferred_element_type=F32)

    @pl.when(j == pl.num_programs(1) - 1)
    def _():
        out_ref[...] = h_ref[...] + _rms_norm(acc_scr[...], gpost_ref[...])


def _ffn(h2, g_pre, wg, wu, conv_w, conv_b, wd, g_post, seq_len):
    T = h2.shape[0]
    n_j = FFN_WIDTH // FF_TF
    return pl.pallas_call(
        functools.partial(_ffn_kernel, tiles_per_seq=seq_len // FF_TM),
        grid=(T // FF_TM, n_j),
        in_specs=[
            pl.BlockSpec((FF_TM, D_MODEL), lambda i, j: (i, 0)),
            pl.BlockSpec((1, D_MODEL), lambda i, j: (0, 0)),
            pl.BlockSpec((D_MODEL, FF_TF), lambda i, j: (0, j)),
            pl.BlockSpec((D_MODEL, FF_TF), lambda i, j: (0, j)),
            pl.BlockSpec((FFN_CONV, FF_TF), lambda i, j: (0, j)),
            pl.BlockSpec((1, FF_TF), lambda i, j: (0, j)),
            pl.BlockSpec((FF_TF, D_MODEL), lambda i, j: (j, 0)),
            pl.BlockSpec((1, D_MODEL), lambda i, j: (0, 0)),
        ],
        out_specs=pl.BlockSpec((FF_TM, D_MODEL), lambda i, j: (i, 0)),
        out_shape=jax.ShapeDtypeStruct((T, D_MODEL), F32),
        scratch_shapes=[
            pltpu.VMEM((FF_TM, D_MODEL), BF16),
            pltpu.VMEM((FF_TM, D_MODEL), F32),
            pltpu.VMEM((SUBLANES + FF_TM, FF_TF), F32),
            pltpu.VMEM((n_j, SUBLANES, FF_TF), F32),
        ],
        compiler_params=pltpu.CompilerParams(
            dimension_semantics=("arbitrary", "arbitrary"), vmem_limit_bytes=VMEM_LIMIT_BYTES),
        name="ffn",
    )(h2, g_pre, wg, wu, conv_w, conv_b, wd, g_post)


def kernel(x, rel_bias, norm_mix_pre, norm_mix_post, w_in, conv_rnn_w, conv_rnn_b, w_rg_a, b_rg_a,
           w_rg_x, b_rg_x, lru_lambda, w_branch_rnn, w_branch_att, w_out, norm_ffn_pre, norm_ffn_post,
           w_ffn_gate, w_ffn_up, conv_ffn_w, conv_ffn_b, w_ffn_down):
    B, S, D = x.shape
    assert D == D_MODEL and S % FF_TM == 0 and S % RG_TS == 0 and B % RG_NB == 0
    assert S == DILATED_CONFIGS[-1][0], "attention block structure assumes window == sequence for the widest group"
    depth = w_in.shape[0]
    T = B * S
    bias_tab = _bias_tables(rel_bias)
    h = x.reshape(T, D)
    for l in range(depth):
        wax = jnp.concatenate([w_rg_a[l], w_rg_x[l]], axis=-1).astype(BF16)
        bax = jnp.stack([b_rg_a[l], b_rg_x[l]], axis=0)
        y_rnn, qkv, gates = _mix_in(h.reshape(B, S, D), norm_mix_pre[l][None], w_in[l].astype(BF16),
                                    conv_rnn_w[l], conv_rnn_b[l][None], wax, bax, lru_lambda[l][None])
        o_att = _dilated_attention(qkv, bias_tab)
        h = _merge(y_rnn.reshape(T, RNN_WIDTH), o_att.reshape(T, KV_HEADS * HEAD_DIM),
                   gates.reshape(T, GATE_WIDTH), h,
                   w_branch_rnn[l].astype(BF16), w_branch_att[l].astype(BF16), w_out[l].astype(BF16),
                   norm_mix_post[l][None])
        h = _ffn(h, norm_ffn_pre[l][None], w_ffn_gate[l].astype(BF16), w_ffn_up[l].astype(BF16),
                 conv_ffn_w[l], conv_ffn_b[l][None], w_ffn_down[l].astype(BF16), norm_ffn_post[l][None],
                 S)
    return h.reshape(B, S, D)
```

```python
import functools
import math

import numpy as np
import jax
import jax.numpy as jnp
from jax import lax
from jax.experimental import pallas as pl
from jax.experimental.pallas import tpu as pltpu

F32 = jnp.float32
BF16 = jnp.bfloat16

D_MODEL = 1024
RNN_WIDTH = 1280
RNN_BLOCKS = 10
RNN_BLOCK = 128
RNN_CONV = 4
LRU_C = 8.0
HEAD_DIM = 128
KV_HEADS = 4
DILATED_CONFIGS = ((128, 1), (512, 4), (2048, 16))
N_GROUPS = 3
Q_HEADS = 12
ATTN_BLOCK = 128
REL_BUCKETS = 32
REL_MAX_DIST = 2048
FFN_WIDTH = 3072
FFN_CONV = 3
EPS = 1e-6
QKV_HEADS = Q_HEADS + 2 * KV_HEADS
QKV_WIDTH = QKV_HEADS * HEAD_DIM
GATE_WIDTH = 2 * D_MODEL
IN_WIDTH = RNN_WIDTH + QKV_WIDTH + GATE_WIDTH

SUBLANES = 8
LANES = 128
VMEM_LIMIT_BYTES = 56 * 1024 * 1024

MASK_VALUE = -1e30
LOG2E = math.log2(math.e)
QK_SCALE_LOG2 = HEAD_DIM ** -0.5 * LOG2E

RG_TS = 64
RG_NB = 8
RG_PITCH = RG_TS + 8
RG_TC = 16
MIX_TN = 256
MG_TM = 512
FF_TM = 512
FF_SR = 256
FF_TC = 512
FF_DN = 256


def _rms_norm(x, g):
    ms = jnp.mean(x * x, axis=-1, keepdims=True)
    return x * lax.rsqrt(ms + EPS) * g


def _sigmoid(x):
    return 0.5 + 0.5 * jnp.tanh(0.5 * x)


def _t5_bucket(dist):
    max_exact = REL_BUCKETS // 2
    d = np.maximum(dist, 1).astype(np.float32)
    large = max_exact + np.log(d / max_exact) / math.log(REL_MAX_DIST / max_exact) * (REL_BUCKETS - max_exact)
    large = np.minimum(large.astype(np.int32), REL_BUCKETS - 1)
    return np.where(dist < max_exact, dist, large).astype(np.int32)


def _bucket_maps():
    qi = np.arange(ATTN_BLOCK)[:, None]
    kj = np.arange(2 * ATTN_BLOCK)[None, :]
    delta = ATTN_BLOCK + qi - kj
    maps = []
    for window, dilation in DILATED_CONFIGS:
        n_back = window // dilation
        valid = (delta >= 0) & (delta <= n_back)
        bucket = _t5_bucket(np.maximum(delta, 0) * dilation)
        maps.append(np.where(valid, bucket, -1).astype(np.int32))
    return np.stack(maps, axis=0)


def _bias_kernel(rb_ref, bucket_ref, out_ref):
    head = pl.program_id(0)
    bk = bucket_ref[0]
    acc = jnp.full(bk.shape, MASK_VALUE, F32)
    for b in range(REL_BUCKETS):
        acc = jnp.where(bk == b, rb_ref[b, head] * LOG2E, acc)
    out_ref[0] = acc


def _bias_tables(rel_bias):
    buckets = jnp.asarray(_bucket_maps())
    return pl.pallas_call(
        _bias_kernel,
        grid=(Q_HEADS,),
        in_specs=[
            pl.BlockSpec(memory_space=pltpu.SMEM),
            pl.BlockSpec((1, ATTN_BLOCK, 2 * ATTN_BLOCK), lambda h: (h // KV_HEADS, 0, 0)),
        ],
        out_specs=pl.BlockSpec((1, ATTN_BLOCK, 2 * ATTN_BLOCK), lambda h: (h, 0, 0)),
        out_shape=jax.ShapeDtypeStruct((Q_HEADS, ATTN_BLOCK, 2 * ATTN_BLOCK), F32),
        name="bias_table",
    )(rel_bias, buckets)


def _rglru_chunk(n, t0, xr_scr, y_scr, carry, cw_ref, cb_ref, wax_ref, bax_ref, lam_ref):
    pitch, tc = RG_PITCH, RG_TC
    cols = slice(n * RNN_BLOCK, (n + 1) * RNN_BLOCK)
    bcast = lambda row: jnp.broadcast_to(row, (RG_NB, RNN_BLOCK))
    w = [bcast(cw_ref[k:k + 1, cols]) for k in range(RNN_CONV)]
    cb = bcast(cb_ref[:, cols])
    xs = [carry[n, k] for k in range(RNN_CONV - 1)]
    xs += [xr_scr[n, pl.ds(t0 + j, RG_NB, stride=pitch), :] for j in range(tc)]
    xc = jnp.concatenate(
        [cb + w[3] * xs[j + 3] + w[2] * xs[j + 2] + w[1] * xs[j + 1] + w[0] * xs[j] for j in range(tc)],
        axis=0)
    g = jnp.dot(xc.astype(BF16), wax_ref[n], preferred_element_type=F32)
    r = _sigmoid(g[:, 0:RNN_BLOCK] + bax_ref[0:1, cols])
    i = _sigmoid(g[:, RNN_BLOCK:2 * RNN_BLOCK] + bax_ref[1:2, cols])
    neg_lam = -lam_ref[:, cols]
    softplus = jnp.maximum(neg_lam, 0.0) + jnp.log1p(jnp.exp(-jnp.abs(neg_lam)))
    log_a = (-LRU_C * softplus) * r
    a = jnp.exp(log_a)
    th = jnp.tanh(log_a)
    u = jnp.sqrt(-2.0 * th / (1.0 - th)) * (i * xc)
    h = carry[n, RNN_CONV - 1]
    for j in range(tc):
        h = a[j * RG_NB:(j + 1) * RG_NB] * h + u[j * RG_NB:(j + 1) * RG_NB]
        y_scr[n, pl.ds(t0 + j, RG_NB, stride=pitch), :] = h
    for k in range(RNN_CONV - 1):
        carry[n, k] = xs[tc + k]
    carry[n, RNN_CONV - 1] = h


def _mix_in_kernel(x_ref, g_ref, w_ref, cw_ref, cb_ref, wax_ref, bax_ref, lam_ref,
                   y_ref, qkv_ref, gate_ref, xr_scr, y_scr, carry):
    s = pl.program_id(1)
    ts, pitch = RG_TS, RG_PITCH
    rows = RG_NB * ts

    @pl.when(s == 0)
    def _():
        carry[...] = jnp.zeros_like(carry)

    hn = _rms_norm(x_ref[...].reshape(rows, D_MODEL), g_ref[...]).astype(BF16)
    c0, c1 = RNN_WIDTH, RNN_WIDTH + QKV_WIDTH

    def project(col):
        out = jnp.dot(hn, w_ref[:, col:col + MIX_TN], preferred_element_type=F32)
        if col < c0:
            for n in range(col // RNN_BLOCK, (col + MIX_TN) // RNN_BLOCK):
                lanes = slice(n * RNN_BLOCK - col, (n + 1) * RNN_BLOCK - col)
                for b in range(RG_NB):
                    xr_scr[n, b * pitch:b * pitch + ts, :] = out[b * ts:(b + 1) * ts, lanes]
        elif col < c1:
            for hd in range((col - c0) // HEAD_DIM, (col - c0 + MIX_TN) // HEAD_DIM):
                lanes = slice(c0 + hd * HEAD_DIM - col, c0 + (hd + 1) * HEAD_DIM - col)
                qkv_ref[:, hd, :, :] = out[:, lanes].reshape(RG_NB, ts, HEAD_DIM)
        else:
            gate_ref[:, :, col - c1:col - c1 + MIX_TN] = out.astype(BF16).reshape(RG_NB, ts, MIX_TN)

    def recur(n, ci):
        _rglru_chunk(n, ci * RG_TC, xr_scr, y_scr, carry, cw_ref, cb_ref, wax_ref, bax_ref, lam_ref)
        if ci == ts // RG_TC - 1:
            for b in range(RG_NB):
                y_ref[b, :, n * RNN_BLOCK:(n + 1) * RNN_BLOCK] = (
                    y_scr[n, b * pitch:b * pitch + ts, :].astype(BF16))

    slabs_per_block = MIX_TN // RNN_BLOCK
    units = [(n, ci) for n0 in range(0, RNN_BLOCKS, slabs_per_block) for ci in range(ts // RG_TC)
             for n in range(n0, n0 + slabs_per_block)]
    cols = list(range(0, IN_WIDTH, MIX_TN))
    project(cols[0])
    done = 1
    for k, (n, ci) in enumerate(units):
        recur(n, ci)
        want = 1 + ((k + 1) * (len(cols) - 1) + len(units) - 1) // len(units)
        while done < want:
            project(cols[done])
            done += 1


def _mix_in(x3, g, w_bf, conv_w, conv_b, wax, bax, lam):
    B, S, _ = x3.shape
    const2 = lambda b, s: (0, 0)
    tile = lambda width: pl.BlockSpec((RG_NB, RG_TS, width), lambda b, s: (b, s, 0))
    scan_scratch = pltpu.VMEM((RNN_BLOCKS, RG_NB * RG_PITCH, RNN_BLOCK), F32)
    return pl.pallas_call(
        _mix_in_kernel,
        grid=(B // RG_NB, S // RG_TS),
        in_specs=[
            tile(D_MODEL),
            pl.BlockSpec((1, D_MODEL), const2),
            pl.BlockSpec((D_MODEL, IN_WIDTH), const2, pipeline_mode=pl.Buffered(1)),
            pl.BlockSpec((RNN_CONV, RNN_WIDTH), const2),
            pl.BlockSpec((1, RNN_WIDTH), const2),
            pl.BlockSpec((RNN_BLOCKS, RNN_BLOCK, 2 * RNN_BLOCK), lambda b, s: (0, 0, 0)),
            pl.BlockSpec((2, RNN_WIDTH), const2),
            pl.BlockSpec((1, RNN_WIDTH), const2),
        ],
        out_specs=[
            tile(RNN_WIDTH),
            pl.BlockSpec((RG_NB, QKV_HEADS, RG_TS, HEAD_DIM), lambda b, s: (b, 0, s, 0)),
            tile(GATE_WIDTH),
        ],
        out_shape=[
            jax.ShapeDtypeStruct((B, S, RNN_WIDTH), BF16),
            jax.ShapeDtypeStruct((B, QKV_HEADS, S, HEAD_DIM), F32),
            jax.ShapeDtypeStruct((B, S, GATE_WIDTH), BF16),
        ],
        scratch_shapes=[
            scan_scratch, scan_scratch,
            pltpu.VMEM((RNN_BLOCKS, RNN_CONV, RG_NB, RNN_BLOCK), F32),
        ],
        compiler_params=pltpu.CompilerParams(
            dimension_semantics=("parallel", "arbitrary"), vmem_limit_bytes=VMEM_LIMIT_BYTES),
        name="mix_in",
    )(x3, g, w_bf, conv_w, conv_b, wax, bax, lam)


def _dilated_kernel(q1_ref, q2_ref, q3_ref, k_ref, v_ref, b1_ref, b2_ref, b3_ref, o_ref,
                    qp, kp, vp, tq, tk, tv, s_scr, p_scr, o_scr, m_scr, d_scr):
    S = k_ref.shape[0]
    blk = ATTN_BLOCK
    n_blocks = S // blk
    quarter = S // 4
    nt = (((1,), (1,)), ((), ()))

    def put(g, r0, n, q, k, v):
        qp[g, r0:r0 + n, :] = (q * QK_SCALE_LOG2).astype(BF16)
        kp[g, r0:r0 + n, :] = k.astype(BF16)
        vp[g, r0:r0 + n, 0:HEAD_DIM] = v.astype(BF16)

    for g in range(N_GROUPS):
        vp[g, :, HEAD_DIM:2 * HEAD_DIM] = jnp.ones((S, HEAD_DIM), BF16)
    for c in range(4):
        r = slice(c * quarter, (c + 1) * quarter)
        put(0, c * quarter, quarter, q1_ref[r, :], k_ref[r, :], v_ref[r, :])
    for c in range(4):
        sr = pl.ds(c, quarter, stride=4)
        r = slice(c * quarter, (c + 1) * quarter)
        k4, v4 = k_ref[sr, :], v_ref[sr, :]
        put(1, c * quarter, quarter, q2_ref[sr, :], k4, v4)
        tk[r, :] = k4
        tv[r, :] = v4
        tq[r, :] = q3_ref[sr, :]
    for c4 in range(4):
        for j in range(4):
            sr = pl.ds(c4 * quarter + j, blk, stride=4)
            put(2, (4 * j + c4) * blk, blk, tq[sr, :], tk[sr, :], tv[sr, :])

    def run_group(g, b_ref, dil):
        nb = S // dil // blk

        def nat_rows(p):
            c, n = divmod(p, nb)
            start = n * (blk * dil) + c
            return pl.ds(start, blk) if dil == 1 else pl.ds(start, blk, stride=dil)

        def key_rows(p):
            has_prev = p % nb != 0
            return (slice((p - 1) * blk, (p + 1) * blk) if has_prev else slice(p * blk, (p + 1) * blk),
                    slice(0, 2 * blk) if has_prev else slice(blk, 2 * blk))

        for p in range(n_blocks):
            kr, cols = key_rows(p)
            s = lax.dot_general(qp[g, p * blk:(p + 1) * blk, :], kp[g, kr, :], nt, preferred_element_type=F32)
            s_scr[p, :, cols] = s + b_ref[:, cols]
        for p in range(n_blocks):
            _, cols = key_rows(p)
            m = jnp.max(s_scr[p, :, cols], axis=-1, keepdims=True)
            m_scr[g, nat_rows(p), :] = jnp.broadcast_to(m, (blk, HEAD_DIM))
            p_scr[p, :, cols] = jnp.exp2(s_scr[p, :, cols] - m).astype(BF16)
        for p in range(n_blocks):
            kr, cols = key_rows(p)
            od = jnp.dot(p_scr[p, :, cols], vp[g, kr, :], preferred_element_type=F32)
            o_scr[g, nat_rows(p), :] = od[:, 0:HEAD_DIM]
            d_scr[g, nat_rows(p), :] = od[:, HEAD_DIM:2 * HEAD_DIM]

    run_group(0, b1_ref, DILATED_CONFIGS[0][1])
    run_group(1, b2_ref, DILATED_CONFIGS[1][1])
    run_group(2, b3_ref, DILATED_CONFIGS[2][1])

    chunk = 256

    def merge(i, carry):
        r = pl.ds(pl.multiple_of(i * chunk, chunk), chunk)
        m0, m1, m2 = m_scr[0, r, :], m_scr[1, r, :], m_scr[2, r, :]
        mx = jnp.maximum(jnp.maximum(m0, m1), m2)
        w0, w1, w2 = jnp.exp2(m0 - mx), jnp.exp2(m1 - mx), jnp.exp2(m2 - mx)
        num = w0 * o_scr[0, r, :] + w1 * o_scr[1, r, :] + w2 * o_scr[2, r, :]
        z = w0 * d_scr[0, r, :] + w1 * d_scr[1, r, :] + w2 * d_scr[2, r, :]
        o_ref[r, :] = (num / z).astype(o_ref.dtype)
        return carry

    lax.fori_loop(0, S // chunk, merge, 0)


def _dilated_attention(qkv3, bias_tab):
    B, _, S, _ = qkv3.shape
    head_spec = lambda col0: pl.BlockSpec((None, None, S, HEAD_DIM), lambda b, h: (b, col0 + h, 0, 0))
    bias_spec = lambda g: pl.BlockSpec(
        (None, ATTN_BLOCK, 2 * ATTN_BLOCK), lambda b, h: (g * KV_HEADS + h, 0, 0))
    return pl.pallas_call(
        _dilated_kernel,
        grid=(B, KV_HEADS),
        in_specs=[
            head_spec(0), head_spec(KV_HEADS), head_spec(2 * KV_HEADS),
            head_spec(Q_HEADS), head_spec(Q_HEADS + KV_HEADS),
            bias_spec(0), bias_spec(1), bias_spec(2),
        ],
        out_specs=pl.BlockSpec((None, S, HEAD_DIM), lambda b, h: (b, 0, h)),
        out_shape=jax.ShapeDtypeStruct((B, S, KV_HEADS * HEAD_DIM), BF16),
        scratch_shapes=[
            pltpu.VMEM((N_GROUPS, S, HEAD_DIM), BF16),
            pltpu.VMEM((N_GROUPS, S, HEAD_DIM), BF16),
            pltpu.VMEM((N_GROUPS, S, 2 * HEAD_DIM), BF16),
            pltpu.VMEM((S, HEAD_DIM), F32),
            pltpu.VMEM((S, HEAD_DIM), F32),
            pltpu.VMEM((S, HEAD_DIM), F32),
            pltpu.VMEM((S // ATTN_BLOCK, ATTN_BLOCK, 2 * ATTN_BLOCK), F32),
            pltpu.VMEM((S // ATTN_BLOCK, ATTN_BLOCK, 2 * ATTN_BLOCK), BF16),
            pltpu.VMEM((N_GROUPS, S, HEAD_DIM), F32),
            pltpu.VMEM((N_GROUPS, S, HEAD_DIM), F32),
            pltpu.VMEM((N_GROUPS, S, HEAD_DIM), F32),
        ],
        compiler_params=pltpu.CompilerParams(
            dimension_semantics=("parallel", "parallel"), vmem_limit_bytes=VMEM_LIMIT_BYTES),
        name="dilated_attn",
    )(qkv3, qkv3, qkv3, qkv3, qkv3, bias_tab, bias_tab, bias_tab)


def _merge_kernel(y_ref, o_ref, gate_ref, x_ref, wbr_ref, wba_ref, wout_ref, g_ref, h_ref):
    br = jnp.dot(y_ref[...], wbr_ref[...], preferred_element_type=F32)
    ba = jnp.dot(o_ref[...], wba_ref[...], preferred_element_type=F32)
    g_rnn = _sigmoid(gate_ref[:, 0:D_MODEL].astype(F32))
    g_att = _sigmoid(gate_ref[:, D_MODEL:GATE_WIDTH].astype(F32))
    merged = g_rnn * br + g_att * ba
    mix = jnp.dot(merged.astype(BF16), wout_ref[...], preferred_element_type=F32)
    h_ref[...] = x_ref[...] + _rms_norm(mix, g_ref[...])


def _merge(y2, o2, gates, x2, wbr, wba, wout, g_post):
    T = x2.shape[0]
    const = lambda i: (0, 0)
    row = lambda i: (i, 0)
    return pl.pallas_call(
        _merge_kernel,
        grid=(T // MG_TM,),
        in_specs=[
            pl.BlockSpec((MG_TM, RNN_WIDTH), row),
            pl.BlockSpec((MG_TM, KV_HEADS * HEAD_DIM), row),
            pl.BlockSpec((MG_TM, GATE_WIDTH), row),
            pl.BlockSpec((MG_TM, D_MODEL), row),
            pl.BlockSpec((RNN_WIDTH, D_MODEL), const),
            pl.BlockSpec((KV_HEADS * HEAD_DIM, D_MODEL), const),
            pl.BlockSpec((D_MODEL, D_MODEL), const),
            pl.BlockSpec((1, D_MODEL), const),
        ],
        out_specs=pl.BlockSpec((MG_TM, D_MODEL), row),
        out_shape=jax.ShapeDtypeStruct((T, D_MODEL), F32),
        compiler_params=pltpu.CompilerParams(
            dimension_semantics=("parallel",), vmem_limit_bytes=VMEM_LIMIT_BYTES),
        name="merge_out",
    )(y2, o2, gates, x2, wbr, wba, wout, g_post)


def _gelu_tanh(x):
    return 0.5 * x * (1.0 + jnp.tanh(math.sqrt(2.0 / math.pi) * (x + 0.044715 * (x * x * x))))


def _ffn_kernel(h_ref, gpre_ref, wg_ref, wu_ref, cw_ref, cb_ref, wd_ref, gpost_ref, out_ref,
                hn_scr, gate_scr, up_scr, act_scr, ff_scr, *, tiles_per_seq):
    i = pl.program_id(0)
    sr, n_sub = FF_SR, FF_TM // FF_SR
    n_c = FFN_WIDTH // FF_TC

    @pl.when(i % tiles_per_seq == 0)
    def _():
        gate_scr[0, 0:SUBLANES, :] = jnp.zeros((SUBLANES, FFN_WIDTH), F32)

    def rows(r):
        return slice(r * sr, (r + 1) * sr)

    def stage_p(r):
        hn_scr[r % 2] = _rms_norm(h_ref[rows(r), :], gpre_ref[...]).astype(BF16)

    def stage_g(r, c, w_ref):
        cols = slice(c * FF_TC, (c + 1) * FF_TC)
        res = jnp.dot(hn_scr[r % 2], w_ref[:, cols], preferred_element_type=F32)
        if w_ref is wg_ref:
            gate_scr[r % 2, SUBLANES:SUBLANES + sr, cols] = res
            gate_scr[(r + 1) % 2, 0:SUBLANES, cols] = res[sr - SUBLANES:sr, :]
        else:
            up_scr[r % 2, :, cols] = res

    def stage_v(r, c):
        cols = slice(c * FF_TC, (c + 1) * FF_TC)
        conv = cb_ref[:, cols] + cw_ref[FFN_CONV - 1:FFN_CONV, cols] * gate_scr[r % 2, SUBLANES:SUBLANES + sr, cols]
        for k in range(FFN_CONV - 1):
            back = FFN_CONV - 1 - k
            conv = conv + cw_ref[k:k + 1, cols] * gate_scr[r % 2, SUBLANES - back:SUBLANES - back + sr, cols]
        act_scr[r % 2, :, cols] = (_gelu_tanh(conv) * up_scr[r % 2, :, cols]).astype(BF16)

    def stage_d(r, c):
        cols = slice(c * FF_DN, (c + 1) * FF_DN)
        ff_scr[r % 2, :, cols] = jnp.dot(act_scr[r % 2], wd_ref[:, cols], preferred_element_type=F32)

    def stage_e(r):
        out_ref[rows(r), :] = h_ref[rows(r), :] + _rms_norm(ff_scr[r % 2], gpost_ref[...])

    n_d = D_MODEL // FF_DN
    for t in range(n_sub + 4):
        live = lambda r: 0 <= r < n_sub
        if live(t):
            stage_p(t)
        for c in range(n_c):
            if live(t - 2):
                stage_v(t - 2, c)
            if live(t - 1):
                stage_g(t - 1, c, wg_ref)
                stage_g(t - 1, c, wu_ref)
            if live(t - 3):
                for d in range(c * n_d // n_c, (c + 1) * n_d // n_c):
                    stage_d(t - 3, d)
        if live(t - 4):
            stage_e(t - 4)


def _ffn(h2, g_pre, wg, wu, conv_w, conv_b, wd, g_post, seq_len):
    T = h2.shape[0]
    const = lambda i: (0, 0)
    resident = lambda shape: pl.BlockSpec(shape, const, pipeline_mode=pl.Buffered(1))
    return pl.pallas_call(
        functools.partial(_ffn_kernel, tiles_per_seq=seq_len // FF_TM),
        grid=(T // FF_TM,),
        in_specs=[
            pl.BlockSpec((FF_TM, D_MODEL), lambda i: (i, 0)),
            pl.BlockSpec((1, D_MODEL), const),
            resident((D_MODEL, FFN_WIDTH)),
            resident((D_MODEL, FFN_WIDTH)),
            pl.BlockSpec((FFN_CONV, FFN_WIDTH), const),
            pl.BlockSpec((1, FFN_WIDTH), const),
            resident((FFN_WIDTH, D_MODEL)),
            pl.BlockSpec((1, D_MODEL), const),
        ],
        out_specs=pl.BlockSpec((FF_TM, D_MODEL), lambda i: (i, 0)),
        out_shape=jax.ShapeDtypeStruct((T, D_MODEL), F32),
        scratch_shapes=[
            pltpu.VMEM((2, FF_SR, D_MODEL), BF16),
            pltpu.VMEM((2, SUBLANES + FF_SR, FFN_WIDTH), F32),
            pltpu.VMEM((2, FF_SR, FFN_WIDTH), F32),
            pltpu.VMEM((2, FF_SR, FFN_WIDTH), BF16),
            pltpu.VMEM((2, FF_SR, D_MODEL), F32),
        ],
        compiler_params=pltpu.CompilerParams(
            dimension_semantics=("arbitrary",), vmem_limit_bytes=VMEM_LIMIT_BYTES),
        name="ffn",
    )(h2, g_pre, wg, wu, conv_w, conv_b, wd, g_post)


def kernel(x, rel_bias, norm_mix_pre, norm_mix_post, w_in, conv_rnn_w, conv_rnn_b, w_rg_a, b_rg_a,
           w_rg_x, b_rg_x, lru_lambda, w_branch_rnn, w_branch_att, w_out, norm_ffn_pre, norm_ffn_post,
           w_ffn_gate, w_ffn_up, conv_ffn_w, conv_ffn_b, w_ffn_down):
    B, S, D = x.shape
    assert D == D_MODEL and S % FF_TM == 0 and S % RG_TS == 0 and B % RG_NB == 0
    assert (FF_TM // FF_SR) % 2 == 0, "the conv halo hand-off between grid steps relies on an even sub-tile count"
    assert S == DILATED_CONFIGS[-1][0], "attention block structure assumes window == sequence for the widest group"
    depth = w_in.shape[0]
    T = B * S
    bias_tab = _bias_tables(rel_bias)
    h = x.reshape(T, D)
    for l in range(depth):
        wax = jnp.concatenate([w_rg_a[l], w_rg_x[l]], axis=-1).astype(BF16)
        bax = jnp.stack([b_rg_a[l], b_rg_x[l]], axis=0)
        y_rnn, qkv, gates = _mix_in(h.reshape(B, S, D), norm_mix_pre[l][None], w_in[l].astype(BF16),
                                    conv_rnn_w[l], conv_rnn_b[l][None], wax, bax, lru_lambda[l][None])
        o_att = _dilated_attention(qkv, bias_tab)
        h = _merge(y_rnn.reshape(T, RNN_WIDTH), o_att.reshape(T, KV_HEADS * HEAD_DIM),
                   gates.reshape(T, GATE_WIDTH), h,
                   w_branch_rnn[l].astype(BF16), w_branch_att[l].astype(BF16), w_out[l].astype(BF16),
                   norm_mix_post[l][None])
        h = _ffn(h, norm_ffn_pre[l][None], w_ffn_gate[l].astype(BF16), w_ffn_up[l].astype(BF16),
                 conv_ffn_w[l], conv_ffn_b[l][None], w_ffn_down[l].astype(BF16), norm_ffn_post[l][None],
                 S)
    return h.reshape(B, S, D)
```

```python
import functools
import math

import numpy as np
import jax
import jax.numpy as jnp
from jax import lax
from jax.experimental import pallas as pl
from jax.experimental.pallas import tpu as pltpu

F32 = jnp.float32
BF16 = jnp.bfloat16

D_MODEL = 1024
RNN_WIDTH = 1280
RNN_BLOCKS = 10
RNN_BLOCK = 128
RNN_CONV = 4
LRU_C = 8.0
HEAD_DIM = 128
KV_HEADS = 4
DILATED_CONFIGS = ((128, 1), (512, 4), (2048, 16))
N_GROUPS = 3
Q_HEADS = 12
ATTN_BLOCK = 128
ATTN_PITCH = ATTN_BLOCK + 8
REL_BUCKETS = 32
REL_MAX_DIST = 2048
FFN_WIDTH = 3072
FFN_CONV = 3
EPS = 1e-6
QKV_HEADS = Q_HEADS + 2 * KV_HEADS
QKV_WIDTH = QKV_HEADS * HEAD_DIM
GATE_WIDTH = 2 * D_MODEL
IN_WIDTH = RNN_WIDTH + QKV_WIDTH + GATE_WIDTH

SUBLANES = 8
LANES = 128
VMEM_LIMIT_BYTES = 56 * 1024 * 1024

MASK_VALUE = -1e30
LOG2E = math.log2(math.e)
QK_SCALE_LOG2 = HEAD_DIM ** -0.5 * LOG2E

RG_TS = 64
RG_NB = 8
RG_PITCH = RG_TS + 8
RG_TC = 16
MIX_TN = 256
MG_TM = 512
FF_TM = 512
FF_SR = 256
FF_TC = 256
FF_DN = 256


def _rms_norm(x, g):
    ms = jnp.mean(x * x, axis=-1, keepdims=True)
    return x * lax.rsqrt(ms + EPS) * g


def _sigmoid(x):
    return 0.5 + 0.5 * jnp.tanh(0.5 * x)


def _t5_bucket(dist):
    max_exact = REL_BUCKETS // 2
    d = np.maximum(dist, 1).astype(np.float32)
    large = max_exact + np.log(d / max_exact) / math.log(REL_MAX_DIST / max_exact) * (REL_BUCKETS - max_exact)
    large = np.minimum(large.astype(np.int32), REL_BUCKETS - 1)
    return np.where(dist < max_exact, dist, large).astype(np.int32)


def _bucket_maps():
    qi = np.arange(ATTN_BLOCK)[:, None]
    kj = np.arange(2 * ATTN_BLOCK)[None, :]
    delta = ATTN_BLOCK + qi - kj
    maps = []
    for window, dilation in DILATED_CONFIGS:
        n_back = window // dilation
        valid = (delta >= 0) & (delta <= n_back)
        bucket = _t5_bucket(np.maximum(delta, 0) * dilation)
        maps.append(np.where(valid, bucket, -1).astype(np.int32))
    return np.stack(maps, axis=0)


def _bias_kernel(rb_ref, bucket_ref, out_ref):
    head = pl.program_id(0)
    bk = bucket_ref[0]
    acc = jnp.full(bk.shape, MASK_VALUE, F32)
    for b in range(REL_BUCKETS):
        acc = jnp.where(bk == b, rb_ref[b, head] * LOG2E, acc)
    out_ref[0] = acc


def _bias_tables(rel_bias):
    buckets = jnp.asarray(_bucket_maps())
    return pl.pallas_call(
        _bias_kernel,
        grid=(Q_HEADS,),
        in_specs=[
            pl.BlockSpec(memory_space=pltpu.SMEM),
            pl.BlockSpec((1, ATTN_BLOCK, 2 * ATTN_BLOCK), lambda h: (h // KV_HEADS, 0, 0)),
        ],
        out_specs=pl.BlockSpec((1, ATTN_BLOCK, 2 * ATTN_BLOCK), lambda h: (h, 0, 0)),
        out_shape=jax.ShapeDtypeStruct((Q_HEADS, ATTN_BLOCK, 2 * ATTN_BLOCK), F32),
        name="bias_table",
    )(rel_bias, buckets)


def _rglru_chunk(n, t0, xr_scr, y_scr, carry, cw_ref, cb_ref, wax_ref, bax_ref, lam_ref):
    pitch, tc = RG_PITCH, RG_TC
    cols = slice(n * RNN_BLOCK, (n + 1) * RNN_BLOCK)
    bcast = lambda row: jnp.broadcast_to(row, (RG_NB, RNN_BLOCK))
    w = [bcast(cw_ref[k:k + 1, cols]) for k in range(RNN_CONV)]
    cb = bcast(cb_ref[:, cols])
    xs = [carry[n, k] for k in range(RNN_CONV - 1)]
    xs += [xr_scr[n, pl.ds(t0 + j, RG_NB, stride=pitch), :] for j in range(tc)]
    xc = jnp.concatenate(
        [cb + w[3] * xs[j + 3] + w[2] * xs[j + 2] + w[1] * xs[j + 1] + w[0] * xs[j] for j in range(tc)],
        axis=0)
    g = jnp.dot(xc.astype(BF16), wax_ref[n], preferred_element_type=F32)
    r = _sigmoid(g[:, 0:RNN_BLOCK] + bax_ref[0:1, cols])
    i = _sigmoid(g[:, RNN_BLOCK:2 * RNN_BLOCK] + bax_ref[1:2, cols])
    neg_lam = -lam_ref[:, cols]
    softplus = jnp.maximum(neg_lam, 0.0) + jnp.log1p(jnp.exp(-jnp.abs(neg_lam)))
    log_a = (-LRU_C * softplus) * r
    a = jnp.exp(log_a)
    th = jnp.tanh(log_a)
    u = jnp.sqrt(-2.0 * th / (1.0 - th)) * (i * xc)
    h = carry[n, RNN_CONV - 1]
    for j in range(tc):
        h = a[j * RG_NB:(j + 1) * RG_NB] * h + u[j * RG_NB:(j + 1) * RG_NB]
        y_scr[n, pl.ds(t0 + j, RG_NB, stride=pitch), :] = h
    for k in range(RNN_CONV - 1):
        carry[n, k] = xs[tc + k]
    carry[n, RNN_CONV - 1] = h


def _mix_in_kernel(x_ref, g_ref, w_ref, cw_ref, cb_ref, wax_ref, bax_ref, lam_ref,
                   y_ref, qkv_ref, gate_ref, xr_scr, y_scr, carry):
    s = pl.program_id(1)
    ts, pitch = RG_TS, RG_PITCH
    rows = RG_NB * ts

    @pl.when(s == 0)
    def _():
        carry[...] = jnp.zeros_like(carry)

    hn = _rms_norm(x_ref[...].reshape(rows, D_MODEL), g_ref[...]).astype(BF16)
    c0, c1 = RNN_WIDTH, RNN_WIDTH + QKV_WIDTH

    def project(col):
        out = jnp.dot(hn, w_ref[:, col:col + MIX_TN], preferred_element_type=F32)
        if col < c0:
            for n in range(col // RNN_BLOCK, (col + MIX_TN) // RNN_BLOCK):
                lanes = slice(n * RNN_BLOCK - col, (n + 1) * RNN_BLOCK - col)
                for b in range(RG_NB):
                    xr_scr[n, b * pitch:b * pitch + ts, :] = out[b * ts:(b + 1) * ts, lanes]
        elif col < c1:
            for hd in range((col - c0) // HEAD_DIM, (col - c0 + MIX_TN) // HEAD_DIM):
                lanes = slice(c0 + hd * HEAD_DIM - col, c0 + (hd + 1) * HEAD_DIM - col)
                qkv_ref[:, hd, :, :] = out[:, lanes].reshape(RG_NB, ts, HEAD_DIM)
        else:
            gate_ref[:, :, col - c1:col - c1 + MIX_TN] = out.astype(BF16).reshape(RG_NB, ts, MIX_TN)

    def recur(n, ci):
        _rglru_chunk(n, ci * RG_TC, xr_scr, y_scr, carry, cw_ref, cb_ref, wax_ref, bax_ref, lam_ref)
        if ci == ts // RG_TC - 1:
            for b in range(RG_NB):
                y_ref[b, :, n * RNN_BLOCK:(n + 1) * RNN_BLOCK] = (
                    y_scr[n, b * pitch:b * pitch + ts, :].astype(BF16))

    slabs_per_block = MIX_TN // RNN_BLOCK
    units = [(n, ci) for n0 in range(0, RNN_BLOCKS, slabs_per_block) for ci in range(ts // RG_TC)
             for n in range(n0, n0 + slabs_per_block)]
    cols = list(range(0, IN_WIDTH, MIX_TN))
    project(cols[0])
    done = 1
    for k, (n, ci) in enumerate(units):
        recur(n, ci)
        want = 1 + ((k + 1) * (len(cols) - 1) + len(units) - 1) // len(units)
        while done < want:
            project(cols[done])
            done += 1


def _mix_in(x3, g, w_bf, conv_w, conv_b, wax, bax, lam):
    B, S, _ = x3.shape
    const2 = lambda b, s: (0, 0)
    tile = lambda width: pl.BlockSpec((RG_NB, RG_TS, width), lambda b, s: (b, s, 0))
    scan_scratch = pltpu.VMEM((RNN_BLOCKS, RG_NB * RG_PITCH, RNN_BLOCK), F32)
    return pl.pallas_call(
        _mix_in_kernel,
        grid=(B // RG_NB, S // RG_TS),
        in_specs=[
            tile(D_MODEL),
            pl.BlockSpec((1, D_MODEL), const2),
            pl.BlockSpec((D_MODEL, IN_WIDTH), const2, pipeline_mode=pl.Buffered(1)),
            pl.BlockSpec((RNN_CONV, RNN_WIDTH), const2),
            pl.BlockSpec((1, RNN_WIDTH), const2),
            pl.BlockSpec((RNN_BLOCKS, RNN_BLOCK, 2 * RNN_BLOCK), lambda b, s: (0, 0, 0)),
            pl.BlockSpec((2, RNN_WIDTH), const2),
            pl.BlockSpec((1, RNN_WIDTH), const2),
        ],
        out_specs=[
            tile(RNN_WIDTH),
            pl.BlockSpec((RG_NB, QKV_HEADS, RG_TS, HEAD_DIM), lambda b, s: (b, 0, s, 0)),
            tile(GATE_WIDTH),
        ],
        out_shape=[
            jax.ShapeDtypeStruct((B, S, RNN_WIDTH), BF16),
            jax.ShapeDtypeStruct((B, QKV_HEADS, S, HEAD_DIM), F32),
            jax.ShapeDtypeStruct((B, S, GATE_WIDTH), BF16),
        ],
        scratch_shapes=[
            scan_scratch, scan_scratch,
            pltpu.VMEM((RNN_BLOCKS, RNN_CONV, RG_NB, RNN_BLOCK), F32),
        ],
        compiler_params=pltpu.CompilerParams(
            dimension_semantics=("parallel", "arbitrary"), vmem_limit_bytes=VMEM_LIMIT_BYTES),
        name="mix_in",
    )(x3, g, w_bf, conv_w, conv_b, wax, bax, lam)


def _dilated_kernel(q1_ref, q2_ref, q3_ref, k_ref, v_ref, b1_ref, b2_ref, b3_ref, o_ref,
                    qp, kp, vp, tq, tk, tv, s_scr, p_scr, o_scr, m_scr, d_scr):
    S = k_ref.shape[0]
    blk = ATTN_BLOCK
    n_blocks = S // blk
    quarter = S // 4
    nt = (((1,), (1,)), ((), ()))

    def put(g, r0, n, q, k, v):
        qp[g, r0:r0 + n, :] = (q * QK_SCALE_LOG2).astype(BF16)
        kp[g, r0:r0 + n, :] = k.astype(BF16)
        vp[g, r0:r0 + n, 0:HEAD_DIM] = v.astype(BF16)

    for g in range(N_GROUPS):
        vp[g, :, HEAD_DIM:2 * HEAD_DIM] = jnp.ones((S, HEAD_DIM), BF16)
    for c in range(4):
        r = slice(c * quarter, (c + 1) * quarter)
        put(0, c * quarter, quarter, q1_ref[r, :], k_ref[r, :], v_ref[r, :])
    for c in range(4):
        sr = pl.ds(c, quarter, stride=4)
        r = slice(c * quarter, (c + 1) * quarter)
        k4, v4 = k_ref[sr, :], v_ref[sr, :]
        put(1, c * quarter, quarter, q2_ref[sr, :], k4, v4)
        tk[r, :] = k4
        tv[r, :] = v4
        tq[r, :] = q3_ref[sr, :]
    for c4 in range(4):
        for j in range(4):
            sr = pl.ds(c4 * quarter + j, blk, stride=4)
            put(2, (4 * j + c4) * blk, blk, tq[sr, :], tk[sr, :], tv[sr, :])

    def run_group(g, b_ref, dil):
        nb = S // dil // blk

        def nat_rows(p):
            c, n = divmod(p, nb)
            if nb == 1:
                return slice(c * ATTN_PITCH, c * ATTN_PITCH + blk)
            start = n * (blk * dil) + c
            return pl.ds(start, blk) if dil == 1 else pl.ds(start, blk, stride=dil)

        def key_rows(p):
            has_prev = p % nb != 0
            return (slice((p - 1) * blk, (p + 1) * blk) if has_prev else slice(p * blk, (p + 1) * blk),
                    slice(0, 2 * blk) if has_prev else slice(blk, 2 * blk))

        for p in range(n_blocks):
            kr, cols = key_rows(p)
            s = lax.dot_general(qp[g, p * blk:(p + 1) * blk, :], kp[g, kr, :], nt, preferred_element_type=F32)
            s_scr[p, :, cols] = s + b_ref[:, cols]
        for p in range(n_blocks):
            _, cols = key_rows(p)
            m = jnp.max(s_scr[p, :, cols], axis=-1, keepdims=True)
            m_scr[g, nat_rows(p), :] = jnp.broadcast_to(m, (blk, HEAD_DIM))
            p_scr[p, :, cols] = jnp.exp2(s_scr[p, :, cols] - m).astype(BF16)
        for p in range(n_blocks):
            kr, cols = key_rows(p)
            od = jnp.dot(p_scr[p, :, cols], vp[g, kr, :], preferred_element_type=F32)
            o_scr[g, nat_rows(p), :] = od[:, 0:HEAD_DIM]
            d_scr[g, nat_rows(p), :] = od[:, HEAD_DIM:2 * HEAD_DIM]

    run_group(0, b1_ref, DILATED_CONFIGS[0][1])
    run_group(1, b2_ref, DILATED_CONFIGS[1][1])
    run_group(2, b3_ref, DILATED_CONFIGS[2][1])

    chunk = 256

    last_dil = DILATED_CONFIGS[2][1]

    def widest(scr, i):
        per = chunk // last_dil
        return jnp.concatenate(
            [scr[2, pl.ds(i * per + j, last_dil, stride=ATTN_PITCH), :] for j in range(per)], axis=0)

    def merge(i, carry):
        r = pl.ds(pl.multiple_of(i * chunk, chunk), chunk)
        m0, m1, m2 = m_scr[0, r, :], m_scr[1, r, :], widest(m_scr, i)
        mx = jnp.maximum(jnp.maximum(m0, m1), m2)
        w0, w1, w2 = jnp.exp2(m0 - mx), jnp.exp2(m1 - mx), jnp.exp2(m2 - mx)
        num = w0 * o_scr[0, r, :] + w1 * o_scr[1, r, :] + w2 * widest(o_scr, i)
        z = w0 * d_scr[0, r, :] + w1 * d_scr[1, r, :] + w2 * widest(d_scr, i)
        o_ref[r, :] = (num / z).astype(o_ref.dtype)
        return carry

    lax.fori_loop(0, S // chunk, merge, 0)


def _dilated_attention(qkv3, bias_tab):
    B, _, S, _ = qkv3.shape
    head_spec = lambda col0: pl.BlockSpec((None, None, S, HEAD_DIM), lambda b, h: (b, col0 + h, 0, 0))
    bias_spec = lambda g: pl.BlockSpec(
        (None, ATTN_BLOCK, 2 * ATTN_BLOCK), lambda b, h: (g * KV_HEADS + h, 0, 0))
    return pl.pallas_call(
        _dilated_kernel,
        grid=(B, KV_HEADS),
        in_specs=[
            head_spec(0), head_spec(KV_HEADS), head_spec(2 * KV_HEADS),
            head_spec(Q_HEADS), head_spec(Q_HEADS + KV_HEADS),
            bias_spec(0), bias_spec(1), bias_spec(2),
        ],
        out_specs=pl.BlockSpec((None, S, HEAD_DIM), lambda b, h: (b, 0, h)),
        out_shape=jax.ShapeDtypeStruct((B, S, KV_HEADS * HEAD_DIM), BF16),
        scratch_shapes=[
            pltpu.VMEM((N_GROUPS, S, HEAD_DIM), BF16),
            pltpu.VMEM((N_GROUPS, S, HEAD_DIM), BF16),
            pltpu.VMEM((N_GROUPS, S, 2 * HEAD_DIM), BF16),
            pltpu.VMEM((S, HEAD_DIM), F32),
            pltpu.VMEM((S, HEAD_DIM), F32),
            pltpu.VMEM((S, HEAD_DIM), F32),
            pltpu.VMEM((S // ATTN_BLOCK, ATTN_BLOCK, 2 * ATTN_BLOCK), F32),
            pltpu.VMEM((S // ATTN_BLOCK, ATTN_BLOCK, 2 * ATTN_BLOCK), BF16),
            pltpu.VMEM((N_GROUPS, S // ATTN_BLOCK * ATTN_PITCH, HEAD_DIM), F32),
            pltpu.VMEM((N_GROUPS, S // ATTN_BLOCK * ATTN_PITCH, HEAD_DIM), F32),
            pltpu.VMEM((N_GROUPS, S // ATTN_BLOCK * ATTN_PITCH, HEAD_DIM), F32),
        ],
        compiler_params=pltpu.CompilerParams(
            dimension_semantics=("parallel", "parallel"), vmem_limit_bytes=VMEM_LIMIT_BYTES),
        name="dilated_attn",
    )(qkv3, qkv3, qkv3, qkv3, qkv3, bias_tab, bias_tab, bias_tab)


def _merge_kernel(y_ref, o_ref, gate_ref, x_ref, wbr_ref, wba_ref, wout_ref, g_ref, h_ref):
    br = jnp.dot(y_ref[...], wbr_ref[...], preferred_element_type=F32)
    ba = jnp.dot(o_ref[...], wba_ref[...], preferred_element_type=F32)
    g_rnn = _sigmoid(gate_ref[:, 0:D_MODEL].astype(F32))
    g_att = _sigmoid(gate_ref[:, D_MODEL:GATE_WIDTH].astype(F32))
    merged = g_rnn * br + g_att * ba
    mix = jnp.dot(merged.astype(BF16), wout_ref[...], preferred_element_type=F32)
    h_ref[...] = x_ref[...] + _rms_norm(mix, g_ref[...])


def _merge(y2, o2, gates, x2, wbr, wba, wout, g_post):
    T = x2.shape[0]
    const = lambda i: (0, 0)
    row = lambda i: (i, 0)
    return pl.pallas_call(
        _merge_kernel,
        grid=(T // MG_TM,),
        in_specs=[
            pl.BlockSpec((MG_TM, RNN_WIDTH), row),
            pl.BlockSpec((MG_TM, KV_HEADS * HEAD_DIM), row),
            pl.BlockSpec((MG_TM, GATE_WIDTH), row),
            pl.BlockSpec((MG_TM, D_MODEL), row),
            pl.BlockSpec((RNN_WIDTH, D_MODEL), const),
            pl.BlockSpec((KV_HEADS * HEAD_DIM, D_MODEL), const),
            pl.BlockSpec((D_MODEL, D_MODEL), const),
            pl.BlockSpec((1, D_MODEL), const),
        ],
        out_specs=pl.BlockSpec((MG_TM, D_MODEL), row),
        out_shape=jax.ShapeDtypeStruct((T, D_MODEL), F32),
        compiler_params=pltpu.CompilerParams(
            dimension_semantics=("parallel",), vmem_limit_bytes=VMEM_LIMIT_BYTES),
        name="merge_out",
    )(y2, o2, gates, x2, wbr, wba, wout, g_post)


def _gelu_tanh(x):
    return 0.5 * x * (1.0 + jnp.tanh(math.sqrt(2.0 / math.pi) * (x + 0.044715 * (x * x * x))))


def _ffn_kernel(h_ref, gpre_ref, wg_ref, wu_ref, cw_ref, cb_ref, wd_ref, gpost_ref, out_ref,
                hn_scr, gate_scr, up_scr, act_scr, ff_scr, *, tiles_per_seq):
    i = pl.program_id(0)
    sr, n_sub = FF_SR, FF_TM // FF_SR
    n_c = FFN_WIDTH // FF_TC

    @pl.when(i % tiles_per_seq == 0)
    def _():
        gate_scr[0, 0:SUBLANES, :] = jnp.zeros((SUBLANES, FFN_WIDTH), F32)

    def rows(r):
        return slice(r * sr, (r + 1) * sr)

    def stage_p(r):
        hn_scr[r % 2] = _rms_norm(h_ref[rows(r), :], gpre_ref[...]).astype(BF16)

    def stage_g(r, c, w_ref):
        cols = slice(c * FF_TC, (c + 1) * FF_TC)
        res = jnp.dot(hn_scr[r % 2], w_ref[:, cols], preferred_element_type=F32)
        if w_ref is wg_ref:
            gate_scr[r % 2, SUBLANES:SUBLANES + sr, cols] = res
            gate_scr[(r + 1) % 2, 0:SUBLANES, cols] = res[sr - SUBLANES:sr, :]
        else:
            up_scr[r % 2, :, cols] = res

    def stage_v(r, c):
        cols = slice(c * FF_TC, (c + 1) * FF_TC)
        conv = cb_ref[:, cols] + cw_ref[FFN_CONV - 1:FFN_CONV, cols] * gate_scr[r % 2, SUBLANES:SUBLANES + sr, cols]
        for k in range(FFN_CONV - 1):
            back = FFN_CONV - 1 - k
            conv = conv + cw_ref[k:k + 1, cols] * gate_scr[r % 2, SUBLANES - back:SUBLANES - back + sr, cols]
        act_scr[r % 2, :, cols] = (_gelu_tanh(conv) * up_scr[r % 2, :, cols]).astype(BF16)

    def stage_d(r, c):
        cols = slice(c * FF_DN, (c + 1) * FF_DN)
        ff_scr[r % 2, :, cols] = jnp.dot(act_scr[r % 2], wd_ref[:, cols], preferred_element_type=F32)

    def stage_e(r):
        out_ref[rows(r), :] = h_ref[rows(r), :] + _rms_norm(ff_scr[r % 2], gpost_ref[...])

    n_d = D_MODEL // FF_DN
    for t in range(n_sub + 4):
        live = lambda r: 0 <= r < n_sub
        if live(t):
            stage_p(t)
        for c in range(n_c):
            if live(t - 2):
                stage_v(t - 2, c)
            if live(t - 1):
                stage_g(t - 1, c, wg_ref)
                stage_g(t - 1, c, wu_ref)
            if live(t - 3):
                for d in range(c * n_d // n_c, (c + 1) * n_d // n_c):
                    stage_d(t - 3, d)
        if live(t - 4):
            stage_e(t - 4)


def _ffn(h2, g_pre, wg, wu, conv_w, conv_b, wd, g_post, seq_len):
    T = h2.shape[0]
    const = lambda i: (0, 0)
    resident = lambda shape: pl.BlockSpec(shape, const, pipeline_mode=pl.Buffered(1))
    return pl.pallas_call(
        functools.partial(_ffn_kernel, tiles_per_seq=seq_len // FF_TM),
        grid=(T // FF_TM,),
        in_specs=[
            pl.BlockSpec((FF_TM, D_MODEL), lambda i: (i, 0)),
            pl.BlockSpec((1, D_MODEL), const),
            resident((D_MODEL, FFN_WIDTH)),
            resident((D_MODEL, FFN_WIDTH)),
            pl.BlockSpec((FFN_CONV, FFN_WIDTH), const),
            pl.BlockSpec((1, FFN_WIDTH), const),
            resident((FFN_WIDTH, D_MODEL)),
            pl.BlockSpec((1, D_MODEL), const),
        ],
        out_specs=pl.BlockSpec((FF_TM, D_MODEL), lambda i: (i, 0)),
        out_shape=jax.ShapeDtypeStruct((T, D_MODEL), F32),
        scratch_shapes=[
            pltpu.VMEM((2, FF_SR, D_MODEL), BF16),
            pltpu.VMEM((2, SUBLANES + FF_SR, FFN_WIDTH), F32),
            pltpu.VMEM((2, FF_SR, FFN_WIDTH), F32),
            pltpu.VMEM((2, FF_SR, FFN_WIDTH), BF16),
            pltpu.VMEM((2, FF_SR, D_MODEL), F32),
        ],
        compiler_params=pltpu.CompilerParams(
            dimension_semantics=("arbitrary",), vmem_limit_bytes=VMEM_LIMIT_BYTES),
        name="ffn",
    )(h2, g_pre, wg, wu, conv_w, conv_b, wd, g_post)


def kernel(x, rel_bias, norm_mix_pre, norm_mix_post, w_in, conv_rnn_w, conv_rnn_b, w_rg_a, b_rg_a,
           w_rg_x, b_rg_x, lru_lambda, w_branch_rnn, w_branch_att, w_out, norm_ffn_pre, norm_ffn_post,
           w_ffn_gate, w_ffn_up, conv_ffn_w, conv_ffn_b, w_ffn_down):
    B, S, D = x.shape
    assert D == D_MODEL and S % FF_TM == 0 and S % RG_TS == 0 and B % RG_NB == 0
    assert (FF_TM // FF_SR) % 2 == 0, "the conv halo hand-off between grid steps relies on an even sub-tile count"
    assert S == DILATED_CONFIGS[-1][0], "attention block structure assumes window == sequence for the widest group"
    depth = w_in.shape[0]
    T = B * S
    bias_tab = _bias_tables(rel_bias)
    h = x.reshape(T, D)
    for l in range(depth):
        wax = jnp.concatenate([w_rg_a[l], w_rg_x[l]], axis=-1).astype(BF16)
        bax = jnp.stack([b_rg_a[l], b_rg_x[l]], axis=0)
        y_rnn, qkv, gates = _mix_in(h.reshape(B, S, D), norm_mix_pre[l][None], w_in[l].astype(BF16),
                                    conv_rnn_w[l], conv_rnn_b[l][None], wax, bax, lru_lambda[l][None])
        o_att = _dilated_attention(qkv, bias_tab)
        h = _merge(y_rnn.reshape(T, RNN_WIDTH), o_att.reshape(T, KV_HEADS * HEAD_DIM),
                   gates.reshape(T, GATE_WIDTH), h,
                   w_branch_rnn[l].astype(BF16), w_branch_att[l].astype(BF16), w_out[l].astype(BF16),
                   norm_mix_post[l][None])
        h = _ffn(h, norm_ffn_pre[l][None], w_ffn_gate[l].astype(BF16), w_ffn_up[l].astype(BF16),
                 conv_ffn_w[l], conv_ffn_b[l][None], w_ffn_down[l].astype(BF16), norm_ffn_post[l][None],
                 S)
    return h.reshape(B, S, D)
```

```python
import functools
import math

import numpy as np
import jax
import jax.numpy as jnp
from jax import lax
from jax.experimental import pallas as pl
from jax.experimental.pallas import tpu as pltpu

F32 = jnp.float32
BF16 = jnp.bfloat16

D_MODEL = 1024
RNN_WIDTH = 1280
RNN_BLOCKS = 10
RNN_BLOCK = 128
RNN_CONV = 4
LRU_C = 8.0
HEAD_DIM = 128
KV_HEADS = 4
DILATED_CONFIGS = ((128, 1), (512, 4), (2048, 16))
N_GROUPS = 3
Q_HEADS = 12
ATTN_BLOCK = 128
ATTN_PITCH = ATTN_BLOCK + 8
REL_BUCKETS = 32
REL_MAX_DIST = 2048
FFN_WIDTH = 3072
FFN_CONV = 3
EPS = 1e-6
QKV_HEADS = Q_HEADS + 2 * KV_HEADS
QKV_WIDTH = QKV_HEADS * HEAD_DIM
GATE_WIDTH = 2 * D_MODEL
IN_WIDTH = RNN_WIDTH + QKV_WIDTH + GATE_WIDTH

SUBLANES = 8
LANES = 128
VMEM_LIMIT_BYTES = 56 * 1024 * 1024

MASK_VALUE = -1e30
LOG2E = math.log2(math.e)
QK_SCALE_LOG2 = HEAD_DIM ** -0.5 * LOG2E

RG_TS = 64
RG_NB = 8
RG_PITCH = RG_TS + 8
RG_TC = 16
MIX_TN = 256
FF_TM = 512
FF_SR = 256
FF_TC = 256
FF_DN = 256


def _rms_norm(x, g):
    ms = jnp.mean(x * x, axis=-1, keepdims=True)
    return x * lax.rsqrt(ms + EPS) * g


def _sigmoid(x):
    return 0.5 + 0.5 * jnp.tanh(0.5 * x)


def _t5_bucket(dist):
    max_exact = REL_BUCKETS // 2
    d = np.maximum(dist, 1).astype(np.float32)
    large = max_exact + np.log(d / max_exact) / math.log(REL_MAX_DIST / max_exact) * (REL_BUCKETS - max_exact)
    large = np.minimum(large.astype(np.int32), REL_BUCKETS - 1)
    return np.where(dist < max_exact, dist, large).astype(np.int32)


def _bucket_maps():
    qi = np.arange(ATTN_BLOCK)[:, None]
    kj = np.arange(2 * ATTN_BLOCK)[None, :]
    delta = ATTN_BLOCK + qi - kj
    maps = []
    for window, dilation in DILATED_CONFIGS:
        n_back = window // dilation
        valid = (delta >= 0) & (delta <= n_back)
        bucket = _t5_bucket(np.maximum(delta, 0) * dilation)
        maps.append(np.where(valid, bucket, -1).astype(np.int32))
    return np.stack(maps, axis=0)


def _bias_kernel(rb_ref, bucket_ref, out_ref):
    head = pl.program_id(0)
    bk = bucket_ref[0]
    acc = jnp.full(bk.shape, MASK_VALUE, F32)
    for b in range(REL_BUCKETS):
        acc = jnp.where(bk == b, rb_ref[b, head] * LOG2E, acc)
    out_ref[0] = acc


def _bias_tables(rel_bias):
    buckets = jnp.asarray(_bucket_maps())
    return pl.pallas_call(
        _bias_kernel,
        grid=(Q_HEADS,),
        in_specs=[
            pl.BlockSpec(memory_space=pltpu.SMEM),
            pl.BlockSpec((1, ATTN_BLOCK, 2 * ATTN_BLOCK), lambda h: (h // KV_HEADS, 0, 0)),
        ],
        out_specs=pl.BlockSpec((1, ATTN_BLOCK, 2 * ATTN_BLOCK), lambda h: (h, 0, 0)),
        out_shape=jax.ShapeDtypeStruct((Q_HEADS, ATTN_BLOCK, 2 * ATTN_BLOCK), F32),
        name="bias_table",
    )(rel_bias, buckets)


def _rglru_chunk(n, t0, xr_scr, y_scr, carry, cw_ref, cb_ref, wax_ref, bax_ref, lam_ref):
    pitch, tc = RG_PITCH, RG_TC
    cols = slice(n * RNN_BLOCK, (n + 1) * RNN_BLOCK)
    bcast = lambda row: jnp.broadcast_to(row, (RG_NB, RNN_BLOCK))
    w = [bcast(cw_ref[k:k + 1, cols]) for k in range(RNN_CONV)]
    cb = bcast(cb_ref[:, cols])
    xs = [carry[n, k] for k in range(RNN_CONV - 1)]
    xs += [xr_scr[n, pl.ds(t0 + j, RG_NB, stride=pitch), :] for j in range(tc)]
    xc = jnp.concatenate(
        [cb + w[3] * xs[j + 3] + w[2] * xs[j + 2] + w[1] * xs[j + 1] + w[0] * xs[j] for j in range(tc)],
        axis=0)
    g = jnp.dot(xc.astype(BF16), wax_ref[n], preferred_element_type=F32)
    r = _sigmoid(g[:, 0:RNN_BLOCK] + bax_ref[0:1, cols])
    i = _sigmoid(g[:, RNN_BLOCK:2 * RNN_BLOCK] + bax_ref[1:2, cols])
    neg_lam = -lam_ref[:, cols]
    softplus = jnp.maximum(neg_lam, 0.0) + jnp.log1p(jnp.exp(-jnp.abs(neg_lam)))
    log_a = (-LRU_C * softplus) * r
    a = jnp.exp(log_a)
    th = jnp.tanh(log_a)
    u = jnp.sqrt(-2.0 * th / (1.0 - th)) * (i * xc)
    h = carry[n, RNN_CONV - 1]
    for j in range(tc):
        h = a[j * RG_NB:(j + 1) * RG_NB] * h + u[j * RG_NB:(j + 1) * RG_NB]
        y_scr[n, pl.ds(t0 + j, RG_NB, stride=pitch), :] = h
    for k in range(RNN_CONV - 1):
        carry[n, k] = xs[tc + k]
    carry[n, RNN_CONV - 1] = h


def _mix_in_kernel(x_ref, g_ref, w_ref, cw_ref, cb_ref, wax_ref, bax_ref, lam_ref,
                   y_ref, qkv_ref, gate_ref, xr_scr, y_scr, carry):
    s = pl.program_id(1)
    ts, pitch = RG_TS, RG_PITCH
    rows = RG_NB * ts

    @pl.when(s == 0)
    def _():
        carry[...] = jnp.zeros_like(carry)

    hn = _rms_norm(x_ref[...].reshape(rows, D_MODEL), g_ref[...]).astype(BF16)
    c0, c1 = RNN_WIDTH, RNN_WIDTH + QKV_WIDTH

    def project(col):
        out = jnp.dot(hn, w_ref[:, col:col + MIX_TN], preferred_element_type=F32)
        if col < c0:
            for n in range(col // RNN_BLOCK, (col + MIX_TN) // RNN_BLOCK):
                lanes = slice(n * RNN_BLOCK - col, (n + 1) * RNN_BLOCK - col)
                for b in range(RG_NB):
                    xr_scr[n, b * pitch:b * pitch + ts, :] = out[b * ts:(b + 1) * ts, lanes]
        elif col < c1:
            for hd in range((col - c0) // HEAD_DIM, (col - c0 + MIX_TN) // HEAD_DIM):
                lanes = slice(c0 + hd * HEAD_DIM - col, c0 + (hd + 1) * HEAD_DIM - col)
                qkv_ref[:, hd, :, :] = out[:, lanes].reshape(RG_NB, ts, HEAD_DIM)
        else:
            gate_ref[:, :, col - c1:col - c1 + MIX_TN] = out.astype(BF16).reshape(RG_NB, ts, MIX_TN)

    def recur(n, ci):
        _rglru_chunk(n, ci * RG_TC, xr_scr, y_scr, carry, cw_ref, cb_ref, wax_ref, bax_ref, lam_ref)
        if ci == ts // RG_TC - 1:
            for b in range(RG_NB):
                y_ref[b, :, n * RNN_BLOCK:(n + 1) * RNN_BLOCK] = (
                    y_scr[n, b * pitch:b * pitch + ts, :].astype(BF16))

    slabs_per_block = MIX_TN // RNN_BLOCK
    units = [(n, ci) for n0 in range(0, RNN_BLOCKS, slabs_per_block) for ci in range(ts // RG_TC)
             for n in range(n0, n0 + slabs_per_block)]
    cols = list(range(0, IN_WIDTH, MIX_TN))
    project(cols[0])
    done = 1
    for k, (n, ci) in enumerate(units):
        recur(n, ci)
        want = 1 + ((k + 1) * (len(cols) - 1) + len(units) - 1) // len(units)
        while done < want:
            project(cols[done])
            done += 1


def _mix_in(x3, g, w_bf, conv_w, conv_b, wax, bax, lam):
    B, S, _ = x3.shape
    const2 = lambda b, s: (0, 0)
    tile = lambda width: pl.BlockSpec((RG_NB, RG_TS, width), lambda b, s: (b, s, 0))
    scan_scratch = pltpu.VMEM((RNN_BLOCKS, RG_NB * RG_PITCH, RNN_BLOCK), F32)
    return pl.pallas_call(
        _mix_in_kernel,
        grid=(B // RG_NB, S // RG_TS),
        in_specs=[
            tile(D_MODEL),
            pl.BlockSpec((1, D_MODEL), const2),
            pl.BlockSpec((D_MODEL, IN_WIDTH), const2, pipeline_mode=pl.Buffered(1)),
            pl.BlockSpec((RNN_CONV, RNN_WIDTH), const2),
            pl.BlockSpec((1, RNN_WIDTH), const2),
            pl.BlockSpec((RNN_BLOCKS, RNN_BLOCK, 2 * RNN_BLOCK), lambda b, s: (0, 0, 0)),
            pl.BlockSpec((2, RNN_WIDTH), const2),
            pl.BlockSpec((1, RNN_WIDTH), const2),
        ],
        out_specs=[
            tile(RNN_WIDTH),
            pl.BlockSpec((RG_NB, QKV_HEADS, RG_TS, HEAD_DIM), lambda b, s: (b, 0, s, 0)),
            tile(GATE_WIDTH),
        ],
        out_shape=[
            jax.ShapeDtypeStruct((B, S, RNN_WIDTH), BF16),
            jax.ShapeDtypeStruct((B, QKV_HEADS, S, HEAD_DIM), F32),
            jax.ShapeDtypeStruct((B, S, GATE_WIDTH), BF16),
        ],
        scratch_shapes=[
            scan_scratch, scan_scratch,
            pltpu.VMEM((RNN_BLOCKS, RNN_CONV, RG_NB, RNN_BLOCK), F32),
        ],
        compiler_params=pltpu.CompilerParams(
            dimension_semantics=("parallel", "arbitrary"), vmem_limit_bytes=VMEM_LIMIT_BYTES),
        name="mix_in",
    )(x3, g, w_bf, conv_w, conv_b, wax, bax, lam)


def _dilated_kernel(q1_ref, q2_ref, q3_ref, k_ref, v_ref, b1_ref, b2_ref, b3_ref, o_ref,
                    qp, kp, vp, tq, tk, tv, s_scr, p_scr, o_scr, m_scr, d_scr):
    S = k_ref.shape[0]
    blk = ATTN_BLOCK
    n_blocks = S // blk
    quarter = S // 4
    nt = (((1,), (1,)), ((), ()))

    def put(g, r0, n, q, k, v):
        qp[g, r0:r0 + n, :] = (q * QK_SCALE_LOG2).astype(BF16)
        kp[g, r0:r0 + n, :] = k.astype(BF16)
        vp[g, r0:r0 + n, 0:HEAD_DIM] = v.astype(BF16)

    for g in range(N_GROUPS):
        vp[g, :, HEAD_DIM:2 * HEAD_DIM] = jnp.ones((S, HEAD_DIM), BF16)
    for c in range(4):
        r = slice(c * quarter, (c + 1) * quarter)
        put(0, c * quarter, quarter, q1_ref[r, :], k_ref[r, :], v_ref[r, :])
    for c in range(4):
        sr = pl.ds(c, quarter, stride=4)
        r = slice(c * quarter, (c + 1) * quarter)
        k4, v4 = k_ref[sr, :], v_ref[sr, :]
        put(1, c * quarter, quarter, q2_ref[sr, :], k4, v4)
        tk[r, :] = k4
        tv[r, :] = v4
        tq[r, :] = q3_ref[sr, :]
    for c4 in range(4):
        for j in range(4):
            sr = pl.ds(c4 * quarter + j, blk, stride=4)
            put(2, (4 * j + c4) * blk, blk, tq[sr, :], tk[sr, :], tv[sr, :])

    def run_group(g, b_ref, dil):
        nb = S // dil // blk

        def nat_rows(p):
            c, n = divmod(p, nb)
            if nb == 1:
                return slice(c * ATTN_PITCH, c * ATTN_PITCH + blk)
            start = n * (blk * dil) + c
            return pl.ds(start, blk) if dil == 1 else pl.ds(start, blk, stride=dil)

        def key_rows(p):
            has_prev = p % nb != 0
            return (slice((p - 1) * blk, (p + 1) * blk) if has_prev else slice(p * blk, (p + 1) * blk),
                    slice(0, 2 * blk) if has_prev else slice(blk, 2 * blk))

        for p in range(n_blocks):
            kr, cols = key_rows(p)
            s = lax.dot_general(qp[g, p * blk:(p + 1) * blk, :], kp[g, kr, :], nt, preferred_element_type=F32)
            s_scr[p, :, cols] = s + b_ref[:, cols]
        for p in range(n_blocks):
            _, cols = key_rows(p)
            m = jnp.max(s_scr[p, :, cols], axis=-1, keepdims=True)
            m_scr[g, nat_rows(p), :] = jnp.broadcast_to(m, (blk, HEAD_DIM))
            p_scr[p, :, cols] = jnp.exp2(s_scr[p, :, cols] - m).astype(BF16)
        for p in range(n_blocks):
            kr, cols = key_rows(p)
            od = jnp.dot(p_scr[p, :, cols], vp[g, kr, :], preferred_element_type=F32)
            o_scr[g, nat_rows(p), :] = od[:, 0:HEAD_DIM]
            d_scr[g, nat_rows(p), :] = od[:, HEAD_DIM:2 * HEAD_DIM]

    run_group(0, b1_ref, DILATED_CONFIGS[0][1])
    run_group(1, b2_ref, DILATED_CONFIGS[1][1])
    run_group(2, b3_ref, DILATED_CONFIGS[2][1])

    chunk = 256

    last_dil = DILATED_CONFIGS[2][1]

    def widest(scr, i):
        per = chunk // last_dil
        return jnp.concatenate(
            [scr[2, pl.ds(i * per + j, last_dil, stride=ATTN_PITCH), :] for j in range(per)], axis=0)

    def merge(i, carry):
        r = pl.ds(pl.multiple_of(i * chunk, chunk), chunk)
        m0, m1, m2 = m_scr[0, r, :], m_scr[1, r, :], widest(m_scr, i)
        mx = jnp.maximum(jnp.maximum(m0, m1), m2)
        w0, w1, w2 = jnp.exp2(m0 - mx), jnp.exp2(m1 - mx), jnp.exp2(m2 - mx)
        num = w0 * o_scr[0, r, :] + w1 * o_scr[1, r, :] + w2 * widest(o_scr, i)
        z = w0 * d_scr[0, r, :] + w1 * d_scr[1, r, :] + w2 * widest(d_scr, i)
        o_ref[r, :] = (num / z).astype(o_ref.dtype)
        return carry

    lax.fori_loop(0, S // chunk, merge, 0)


def _dilated_attention(qkv3, bias_tab):
    B, _, S, _ = qkv3.shape
    head_spec = lambda col0: pl.BlockSpec((None, None, S, HEAD_DIM), lambda b, h: (b, col0 + h, 0, 0))
    bias_spec = lambda g: pl.BlockSpec(
        (None, ATTN_BLOCK, 2 * ATTN_BLOCK), lambda b, h: (g * KV_HEADS + h, 0, 0))
    return pl.pallas_call(
        _dilated_kernel,
        grid=(B, KV_HEADS),
        in_specs=[
            head_spec(0), head_spec(KV_HEADS), head_spec(2 * KV_HEADS),
            head_spec(Q_HEADS), head_spec(Q_HEADS + KV_HEADS),
            bias_spec(0), bias_spec(1), bias_spec(2),
        ],
        out_specs=pl.BlockSpec((None, S, HEAD_DIM), lambda b, h: (b, 0, h)),
        out_shape=jax.ShapeDtypeStruct((B, S, KV_HEADS * HEAD_DIM), BF16),
        scratch_shapes=[
            pltpu.VMEM((N_GROUPS, S, HEAD_DIM), BF16),
            pltpu.VMEM((N_GROUPS, S, HEAD_DIM), BF16),
            pltpu.VMEM((N_GROUPS, S, 2 * HEAD_DIM), BF16),
            pltpu.VMEM((S, HEAD_DIM), F32),
            pltpu.VMEM((S, HEAD_DIM), F32),
            pltpu.VMEM((S, HEAD_DIM), F32),
            pltpu.VMEM((S // ATTN_BLOCK, ATTN_BLOCK, 2 * ATTN_BLOCK), F32),
            pltpu.VMEM((S // ATTN_BLOCK, ATTN_BLOCK, 2 * ATTN_BLOCK), BF16),
            pltpu.VMEM((N_GROUPS, S // ATTN_BLOCK * ATTN_PITCH, HEAD_DIM), F32),
            pltpu.VMEM((N_GROUPS, S // ATTN_BLOCK * ATTN_PITCH, HEAD_DIM), F32),
            pltpu.VMEM((N_GROUPS, S // ATTN_BLOCK * ATTN_PITCH, HEAD_DIM), F32),
        ],
        compiler_params=pltpu.CompilerParams(
            dimension_semantics=("parallel", "parallel"), vmem_limit_bytes=VMEM_LIMIT_BYTES),
        name="dilated_attn",
    )(qkv3, qkv3, qkv3, qkv3, qkv3, bias_tab, bias_tab, bias_tab)


def _gelu_tanh(x):
    return 0.5 * x * (1.0 + jnp.tanh(math.sqrt(2.0 / math.pi) * (x + 0.044715 * (x * x * x))))


def _mix_out_kernel(y_ref, o_ref, mg_ref, x_ref, wbr_ref, wba_ref, wout_ref, gmix_ref,
                    gpre_ref, wg_ref, wu_ref, cw_ref, cb_ref, wd_ref, gpost_ref, out_ref,
                    mrg_scr, mix_scr, h_scr, hn_scr, gate_scr, up_scr, act_scr, ff_scr, *, tiles_per_seq):
    i = pl.program_id(0)
    sr, n_sub = FF_SR, FF_TM // FF_SR
    n_c = FFN_WIDTH // FF_TC

    @pl.when(i % tiles_per_seq == 0)
    def _():
        gate_scr[0, 0:SUBLANES, :] = jnp.zeros((SUBLANES, FFN_WIDTH), F32)

    def rows(r):
        return slice(r * sr, (r + 1) * sr)

    def stage_a(r, c):
        cols = slice(c * FF_DN, (c + 1) * FF_DN)
        br = jnp.dot(y_ref[rows(r), :], wbr_ref[:, cols], preferred_element_type=F32)
        ba = jnp.dot(o_ref[rows(r), :], wba_ref[:, cols], preferred_element_type=F32)
        g_rnn = _sigmoid(mg_ref[rows(r), c * FF_DN:(c + 1) * FF_DN].astype(F32))
        g_att = _sigmoid(mg_ref[rows(r), D_MODEL + c * FF_DN:D_MODEL + (c + 1) * FF_DN].astype(F32))
        mrg_scr[r % 2, :, cols] = (g_rnn * br + g_att * ba).astype(BF16)

    def stage_c(r, c):
        cols = slice(c * FF_DN, (c + 1) * FF_DN)
        mix_scr[r % 2, :, cols] = jnp.dot(mrg_scr[r % 2], wout_ref[:, cols], preferred_element_type=F32)

    def stage_n(r):
        h = x_ref[rows(r), :] + _rms_norm(mix_scr[r % 2], gmix_ref[...])
        h_scr[r] = h
        hn_scr[r % 2] = _rms_norm(h, gpre_ref[...]).astype(BF16)

    def stage_g(r, c, w_ref):
        cols = slice(c * FF_TC, (c + 1) * FF_TC)
        res = jnp.dot(hn_scr[r % 2], w_ref[:, cols], preferred_element_type=F32)
        if w_ref is wg_ref:
            gate_scr[r % 2, SUBLANES:SUBLANES + sr, cols] = res
            gate_scr[(r + 1) % 2, 0:SUBLANES, cols] = res[sr - SUBLANES:sr, :]
        else:
            up_scr[r % 2, :, cols] = res

    def stage_v(r, c):
        cols = slice(c * FF_TC, (c + 1) * FF_TC)
        conv = cb_ref[:, cols] + cw_ref[FFN_CONV - 1:FFN_CONV, cols] * gate_scr[r % 2, SUBLANES:SUBLANES + sr, cols]
        for k in range(FFN_CONV - 1):
            back = FFN_CONV - 1 - k
            conv = conv + cw_ref[k:k + 1, cols] * gate_scr[r % 2, SUBLANES - back:SUBLANES - back + sr, cols]
        act_scr[r % 2, :, cols] = (_gelu_tanh(conv) * up_scr[r % 2, :, cols]).astype(BF16)

    def stage_d(r, c):
        cols = slice(c * FF_DN, (c + 1) * FF_DN)
        ff_scr[r % 2, :, cols] = jnp.dot(act_scr[r % 2], wd_ref[:, cols], preferred_element_type=F32)

    def stage_e(r):
        out_ref[rows(r), :] = h_scr[r] + _rms_norm(ff_scr[r % 2], gpost_ref[...])

    n_d = D_MODEL // FF_DN
    live = lambda r: 0 <= r < n_sub
    for t in range(n_sub + 6):
        for c in range(n_d):
            if live(t):
                stage_a(t, c)
            if live(t - 1):
                stage_c(t - 1, c)
        if live(t - 2):
            stage_n(t - 2)
        for c in range(n_c):
            if live(t - 4):
                stage_v(t - 4, c)
            if live(t - 3):
                stage_g(t - 3, c, wg_ref)
                stage_g(t - 3, c, wu_ref)
            if live(t - 5):
                for d in range(c * n_d // n_c, (c + 1) * n_d // n_c):
                    stage_d(t - 5, d)
        if live(t - 6):
            stage_e(t - 6)


def _mix_out(y2, o2, mgates, x2, wbr, wba, wout, g_mix_post, g_pre, wg, wu, conv_w, conv_b, wd, g_post, seq_len):
    T = x2.shape[0]
    const = lambda i: (0, 0)
    row = lambda width: pl.BlockSpec((FF_TM, width), lambda i: (i, 0))
    resident = lambda shape: pl.BlockSpec(shape, const, pipeline_mode=pl.Buffered(1))
    return pl.pallas_call(
        functools.partial(_mix_out_kernel, tiles_per_seq=seq_len // FF_TM),
        grid=(T // FF_TM,),
        in_specs=[
            row(RNN_WIDTH), row(KV_HEADS * HEAD_DIM), row(GATE_WIDTH), row(D_MODEL),
            resident((RNN_WIDTH, D_MODEL)),
            resident((KV_HEADS * HEAD_DIM, D_MODEL)),
            resident((D_MODEL, D_MODEL)),
            pl.BlockSpec((1, D_MODEL), const),
            pl.BlockSpec((1, D_MODEL), const),
            resident((D_MODEL, FFN_WIDTH)),
            resident((D_MODEL, FFN_WIDTH)),
            pl.BlockSpec((FFN_CONV, FFN_WIDTH), const),
            pl.BlockSpec((1, FFN_WIDTH), const),
            resident((FFN_WIDTH, D_MODEL)),
            pl.BlockSpec((1, D_MODEL), const),
        ],
        out_specs=row(D_MODEL),
        out_shape=jax.ShapeDtypeStruct((T, D_MODEL), F32),
        scratch_shapes=[
            pltpu.VMEM((2, FF_SR, D_MODEL), BF16),
            pltpu.VMEM((2, FF_SR, D_MODEL), F32),
            pltpu.VMEM((FF_TM // FF_SR, FF_SR, D_MODEL), F32),
            pltpu.VMEM((2, FF_SR, D_MODEL), BF16),
            pltpu.VMEM((2, SUBLANES + FF_SR, FFN_WIDTH), F32),
            pltpu.VMEM((2, FF_SR, FFN_WIDTH), F32),
            pltpu.VMEM((2, FF_SR, FFN_WIDTH), BF16),
            pltpu.VMEM((2, FF_SR, D_MODEL), F32),
        ],
        compiler_params=pltpu.CompilerParams(
            dimension_semantics=("arbitrary",), vmem_limit_bytes=VMEM_LIMIT_BYTES),
        name="mix_out",
    )(y2, o2, mgates, x2, wbr, wba, wout, g_mix_post, g_pre, wg, wu, conv_w, conv_b, wd, g_post)


def kernel(x, rel_bias, norm_mix_pre, norm_mix_post, w_in, conv_rnn_w, conv_rnn_b, w_rg_a, b_rg_a,
           w_rg_x, b_rg_x, lru_lambda, w_branch_rnn, w_branch_att, w_out, norm_ffn_pre, norm_ffn_post,
           w_ffn_gate, w_ffn_up, conv_ffn_w, conv_ffn_b, w_ffn_down):
    B, S, D = x.shape
    assert D == D_MODEL and S % FF_TM == 0 and S % RG_TS == 0 and B % RG_NB == 0
    assert (FF_TM // FF_SR) % 2 == 0, "the conv halo hand-off between grid steps relies on an even sub-tile count"
    assert S == DILATED_CONFIGS[-1][0], "attention block structure assumes window == sequence for the widest group"
    depth = w_in.shape[0]
    T = B * S
    bias_tab = _bias_tables(rel_bias)
    h = x.reshape(T, D)
    for l in range(depth):
        wax = jnp.concatenate([w_rg_a[l], w_rg_x[l]], axis=-1).astype(BF16)
        bax = jnp.stack([b_rg_a[l], b_rg_x[l]], axis=0)
        y_rnn, qkv, gates = _mix_in(h.reshape(B, S, D), norm_mix_pre[l][None], w_in[l].astype(BF16),
                                    conv_rnn_w[l], conv_rnn_b[l][None], wax, bax, lru_lambda[l][None])
        o_att = _dilated_attention(qkv, bias_tab)
        h = _mix_out(y_rnn.reshape(T, RNN_WIDTH), o_att.reshape(T, KV_HEADS * HEAD_DIM),
                     gates.reshape(T, GATE_WIDTH), h,
                     w_branch_rnn[l].astype(BF16), w_branch_att[l].astype(BF16), w_out[l].astype(BF16),
                     norm_mix_post[l][None], norm_ffn_pre[l][None],
                     w_ffn_gate[l].astype(BF16), w_ffn_up[l].astype(BF16), conv_ffn_w[l], conv_ffn_b[l][None],
                     w_ffn_down[l].astype(BF16), norm_ffn_post[l][None], S)
    return h.reshape(B, S, D)
```

```python
import functools
import math

import numpy as np
import jax
import jax.numpy as jnp
from jax import lax
from jax.experimental import pallas as pl
from jax.experimental.pallas import tpu as pltpu

F32 = jnp.float32
BF16 = jnp.bfloat16

D_MODEL = 1024
RNN_WIDTH = 1280
RNN_BLOCKS = 10
RNN_BLOCK = 128
RNN_CONV = 4
LRU_C = 8.0
HEAD_DIM = 128
KV_HEADS = 4
DILATED_CONFIGS = ((128, 1), (512, 4), (2048, 16))
N_GROUPS = 3
Q_HEADS = 12
ATTN_BLOCK = 128
ATTN_PITCH = ATTN_BLOCK + 8
REL_BUCKETS = 32
REL_MAX_DIST = 2048
FFN_WIDTH = 3072
FFN_CONV = 3
EPS = 1e-6
QKV_HEADS = Q_HEADS + 2 * KV_HEADS
QKV_WIDTH = QKV_HEADS * HEAD_DIM
GATE_WIDTH = 2 * D_MODEL
IN_WIDTH = RNN_WIDTH + QKV_WIDTH + GATE_WIDTH

SUBLANES = 8
LANES = 128
VMEM_LIMIT_BYTES = 56 * 1024 * 1024

MASK_VALUE = -1e30
LOG2E = math.log2(math.e)
QK_SCALE_LOG2 = HEAD_DIM ** -0.5 * LOG2E

RG_TS = 64
RG_NB = 8
RG_PITCH = RG_TS + 8
RG_TC = 16
MIX_TN = 256
FF_TM = 512
FF_SR = 256
FF_TC = 256
FF_DN = 256


def _rms_norm(x, g):
    ms = jnp.mean(x * x, axis=-1, keepdims=True)
    return x * lax.rsqrt(ms + EPS) * g


def _sigmoid(x):
    return 0.5 + 0.5 * jnp.tanh(0.5 * x)


def _t5_bucket(dist):
    max_exact = REL_BUCKETS // 2
    d = np.maximum(dist, 1).astype(np.float32)
    large = max_exact + np.log(d / max_exact) / math.log(REL_MAX_DIST / max_exact) * (REL_BUCKETS - max_exact)
    large = np.minimum(large.astype(np.int32), REL_BUCKETS - 1)
    return np.where(dist < max_exact, dist, large).astype(np.int32)


def _bucket_maps():
    qi = np.arange(ATTN_BLOCK)[:, None]
    kj = np.arange(2 * ATTN_BLOCK)[None, :]
    delta = ATTN_BLOCK + qi - kj
    maps = []
    for window, dilation in DILATED_CONFIGS:
        n_back = window // dilation
        valid = (delta >= 0) & (delta <= n_back)
        bucket = _t5_bucket(np.maximum(delta, 0) * dilation)
        maps.append(np.where(valid, bucket, -1).astype(np.int32))
    return np.stack(maps, axis=0)


def _bias_kernel(rb_ref, bucket_ref, out_ref):
    head = pl.program_id(0)
    bk = bucket_ref[0]
    acc = jnp.full(bk.shape, MASK_VALUE, F32)
    for b in range(REL_BUCKETS):
        acc = jnp.where(bk == b, rb_ref[b, head] * LOG2E, acc)
    out_ref[0] = acc


def _bias_tables(rel_bias):
    buckets = jnp.asarray(_bucket_maps())
    return pl.pallas_call(
        _bias_kernel,
        grid=(Q_HEADS,),
        in_specs=[
            pl.BlockSpec(memory_space=pltpu.SMEM),
            pl.BlockSpec((1, ATTN_BLOCK, 2 * ATTN_BLOCK), lambda h: (h // KV_HEADS, 0, 0)),
        ],
        out_specs=pl.BlockSpec((1, ATTN_BLOCK, 2 * ATTN_BLOCK), lambda h: (h, 0, 0)),
        out_shape=jax.ShapeDtypeStruct((Q_HEADS, ATTN_BLOCK, 2 * ATTN_BLOCK), F32),
        name="bias_table",
    )(rel_bias, buckets)


def _rglru_chunk(n, t0, xr_scr, y_scr, carry, cw_ref, cb_ref, wax_ref, bax_ref, lam_ref):
    pitch, tc = RG_PITCH, RG_TC
    cols = slice(n * RNN_BLOCK, (n + 1) * RNN_BLOCK)
    bcast = lambda row: jnp.broadcast_to(row, (RG_NB, RNN_BLOCK))
    w = [bcast(cw_ref[k:k + 1, cols]) for k in range(RNN_CONV)]
    cb = bcast(cb_ref[:, cols])
    xs = [carry[n, k] for k in range(RNN_CONV - 1)]
    xs += [xr_scr[n, pl.ds(t0 + j, RG_NB, stride=pitch), :] for j in range(tc)]
    xc = jnp.concatenate(
        [cb + w[3] * xs[j + 3] + w[2] * xs[j + 2] + w[1] * xs[j + 1] + w[0] * xs[j] for j in range(tc)],
        axis=0)
    g = jnp.dot(xc.astype(BF16), wax_ref[n], preferred_element_type=F32)
    r = _sigmoid(g[:, 0:RNN_BLOCK] + bax_ref[0:1, cols])
    i = _sigmoid(g[:, RNN_BLOCK:2 * RNN_BLOCK] + bax_ref[1:2, cols])
    neg_lam = -lam_ref[:, cols]
    softplus = jnp.maximum(neg_lam, 0.0) + jnp.log1p(jnp.exp(-jnp.abs(neg_lam)))
    log_a = (-LRU_C * softplus) * r
    a = jnp.exp(log_a)
    th = jnp.tanh(log_a)
    u = jnp.sqrt(-2.0 * th / (1.0 - th)) * (i * xc)
    h = carry[n, RNN_CONV - 1]
    for j in range(tc):
        h = a[j * RG_NB:(j + 1) * RG_NB] * h + u[j * RG_NB:(j + 1) * RG_NB]
        y_scr[n, pl.ds(t0 + j, RG_NB, stride=pitch), :] = h
    for k in range(RNN_CONV - 1):
        carry[n, k] = xs[tc + k]
    carry[n, RNN_CONV - 1] = h


def _mix_in_kernel(x_ref, g_ref, w_ref, cw_ref, cb_ref, wax_ref, bax_ref, lam_ref,
                   y_ref, qkv_ref, gate_ref, xr_scr, y_scr, carry):
    s = pl.program_id(1)
    ts, pitch = RG_TS, RG_PITCH
    rows = RG_NB * ts

    @pl.when(s == 0)
    def _():
        carry[...] = jnp.zeros_like(carry)

    hn = _rms_norm(x_ref[...].reshape(rows, D_MODEL), g_ref[...]).astype(BF16)
    c0, c1 = RNN_WIDTH, RNN_WIDTH + QKV_WIDTH

    def project(col):
        out = jnp.dot(hn, w_ref[:, col:col + MIX_TN], preferred_element_type=F32)
        if col < c0:
            for n in range(col // RNN_BLOCK, (col + MIX_TN) // RNN_BLOCK):
                lanes = slice(n * RNN_BLOCK - col, (n + 1) * RNN_BLOCK - col)
                for b in range(RG_NB):
                    xr_scr[n, b * pitch:b * pitch + ts, :] = out[b * ts:(b + 1) * ts, lanes]
        elif col < c1:
            for hd in range((col - c0) // HEAD_DIM, (col - c0 + MIX_TN) // HEAD_DIM):
                lanes = slice(c0 + hd * HEAD_DIM - col, c0 + (hd + 1) * HEAD_DIM - col)
                qkv_ref[:, hd, :, :] = out[:, lanes].reshape(RG_NB, ts, HEAD_DIM)
        else:
            gate_ref[:, :, col - c1:col - c1 + MIX_TN] = out.astype(BF16).reshape(RG_NB, ts, MIX_TN)

    def recur(n, ci):
        _rglru_chunk(n, ci * RG_TC, xr_scr, y_scr, carry, cw_ref, cb_ref, wax_ref, bax_ref, lam_ref)
        if ci == ts // RG_TC - 1:
            for b in range(RG_NB):
                y_ref[b, :, n * RNN_BLOCK:(n + 1) * RNN_BLOCK] = (
                    y_scr[n, b * pitch:b * pitch + ts, :].astype(BF16))

    slabs_per_block = MIX_TN // RNN_BLOCK
    units = [(n, ci) for n0 in range(0, RNN_BLOCKS, slabs_per_block) for ci in range(ts // RG_TC)
             for n in range(n0, n0 + slabs_per_block)]
    cols = list(range(0, IN_WIDTH, MIX_TN))
    pairs = [cols[j:j + 2] for j in range(0, len(cols), 2)]
    for col in pairs[0]:
        project(col)
    done = 1
    for k, (n, ci) in enumerate(units):
        recur(n, ci)
        want = 1 + ((k + 1) * (len(pairs) - 1) + len(units) - 1) // len(units)
        while done < want:
            for col in pairs[done]:
                project(col)
            done += 1


def _mix_in(x3, g, w_bf, conv_w, conv_b, wax, bax, lam):
    B, S, _ = x3.shape
    const2 = lambda b, s: (0, 0)
    tile = lambda width: pl.BlockSpec((RG_NB, RG_TS, width), lambda b, s: (b, s, 0))
    scan_scratch = pltpu.VMEM((RNN_BLOCKS, RG_NB * RG_PITCH, RNN_BLOCK), F32)
    return pl.pallas_call(
        _mix_in_kernel,
        grid=(B // RG_NB, S // RG_TS),
        in_specs=[
            tile(D_MODEL),
            pl.BlockSpec((1, D_MODEL), const2),
            pl.BlockSpec((D_MODEL, IN_WIDTH), const2, pipeline_mode=pl.Buffered(1)),
            pl.BlockSpec((RNN_CONV, RNN_WIDTH), const2),
            pl.BlockSpec((1, RNN_WIDTH), const2),
            pl.BlockSpec((RNN_BLOCKS, RNN_BLOCK, 2 * RNN_BLOCK), lambda b, s: (0, 0, 0)),
            pl.BlockSpec((2, RNN_WIDTH), const2),
            pl.BlockSpec((1, RNN_WIDTH), const2),
        ],
        out_specs=[
            tile(RNN_WIDTH),
            pl.BlockSpec((RG_NB, QKV_HEADS, RG_TS, HEAD_DIM), lambda b, s: (b, 0, s, 0)),
            tile(GATE_WIDTH),
        ],
        out_shape=[
            jax.ShapeDtypeStruct((B, S, RNN_WIDTH), BF16),
            jax.ShapeDtypeStruct((B, QKV_HEADS, S, HEAD_DIM), F32),
            jax.ShapeDtypeStruct((B, S, GATE_WIDTH), BF16),
        ],
        scratch_shapes=[
            scan_scratch, scan_scratch,
            pltpu.VMEM((RNN_BLOCKS, RNN_CONV, RG_NB, RNN_BLOCK), F32),
        ],
        compiler_params=pltpu.CompilerParams(
            dimension_semantics=("parallel", "arbitrary"), vmem_limit_bytes=VMEM_LIMIT_BYTES),
        name="mix_in",
    )(x3, g, w_bf, conv_w, conv_b, wax, bax, lam)


def _dilated_kernel(q1_ref, q2_ref, q3_ref, k_ref, v_ref, b1_ref, b2_ref, b3_ref, o_ref,
                    qp, kp, vp, tq, tk, tv, s_scr, p_scr, o_scr, m_scr, d_scr):
    S = k_ref.shape[0]
    blk = ATTN_BLOCK
    n_blocks = S // blk
    quarter = S // 4
    nt = (((1,), (1,)), ((), ()))

    def put(g, r0, n, q, k, v):
        qp[g, r0:r0 + n, :] = (q * QK_SCALE_LOG2).astype(BF16)
        kp[g, r0:r0 + n, :] = k.astype(BF16)
        vp[g, r0:r0 + n, 0:HEAD_DIM] = v.astype(BF16)

    for g in range(N_GROUPS):
        vp[g, :, HEAD_DIM:2 * HEAD_DIM] = jnp.ones((S, HEAD_DIM), BF16)
    for c in range(4):
        r = slice(c * quarter, (c + 1) * quarter)
        put(0, c * quarter, quarter, q1_ref[r, :], k_ref[r, :], v_ref[r, :])
    for c in range(4):
        sr = pl.ds(c, quarter, stride=4)
        r = slice(c * quarter, (c + 1) * quarter)
        k4, v4 = k_ref[sr, :], v_ref[sr, :]
        put(1, c * quarter, quarter, q2_ref[sr, :], k4, v4)
        tk[r, :] = k4
        tv[r, :] = v4
        tq[r, :] = q3_ref[sr, :]
    for c4 in range(4):
        for j in range(4):
            sr = pl.ds(c4 * quarter + j, blk, stride=4)
            put(2, (4 * j + c4) * blk, blk, tq[sr, :], tk[sr, :], tv[sr, :])

    def run_group(g, b_ref, dil):
        nb = S // dil // blk

        def nat_rows(p):
            c, n = divmod(p, nb)
            if nb == 1:
                return slice(c * ATTN_PITCH, c * ATTN_PITCH + blk)
            start = n * (blk * dil) + c
            return pl.ds(start, blk) if dil == 1 else pl.ds(start, blk, stride=dil)

        def key_rows(p):
            has_prev = p % nb != 0
            return (slice((p - 1) * blk, (p + 1) * blk) if has_prev else slice(p * blk, (p + 1) * blk),
                    slice(0, 2 * blk) if has_prev else slice(blk, 2 * blk))

        for p in range(n_blocks):
            kr, cols = key_rows(p)
            s = lax.dot_general(qp[g, p * blk:(p + 1) * blk, :], kp[g, kr, :], nt, preferred_element_type=F32)
            s_scr[p, :, cols] = s + b_ref[:, cols]
        for p in range(n_blocks):
            _, cols = key_rows(p)
            m = jnp.max(s_scr[p, :, cols], axis=-1, keepdims=True)
            m_scr[g, nat_rows(p), :] = jnp.broadcast_to(m, (blk, HEAD_DIM))
            p_scr[p, :, cols] = jnp.exp2(s_scr[p, :, cols] - m).astype(BF16)
        for p in range(n_blocks):
            kr, cols = key_rows(p)
            od = jnp.dot(p_scr[p, :, cols], vp[g, kr, :], preferred_element_type=F32)
            o_scr[g, nat_rows(p), :] = od[:, 0:HEAD_DIM]
            d_scr[g, nat_rows(p), :] = od[:, HEAD_DIM:2 * HEAD_DIM]

    run_group(0, b1_ref, DILATED_CONFIGS[0][1])
    run_group(1, b2_ref, DILATED_CONFIGS[1][1])
    run_group(2, b3_ref, DILATED_CONFIGS[2][1])

    chunk = 256

    last_dil = DILATED_CONFIGS[2][1]

    def widest(scr, i):
        per = chunk // last_dil
        return jnp.concatenate(
            [scr[2, pl.ds(i * per + j, last_dil, stride=ATTN_PITCH), :] for j in range(per)], axis=0)

    def merge(i, carry):
        r = pl.ds(pl.multiple_of(i * chunk, chunk), chunk)
        m0, m1, m2 = m_scr[0, r, :], m_scr[1, r, :], widest(m_scr, i)
        mx = jnp.maximum(jnp.maximum(m0, m1), m2)
        w0, w1, w2 = jnp.exp2(m0 - mx), jnp.exp2(m1 - mx), jnp.exp2(m2 - mx)
        num = w0 * o_scr[0, r, :] + w1 * o_scr[1, r, :] + w2 * widest(o_scr, i)
        z = w0 * d_scr[0, r, :] + w1 * d_scr[1, r, :] + w2 * widest(d_scr, i)
        o_ref[r, :] = (num / z).astype(o_ref.dtype)
        return carry

    lax.fori_loop(0, S // chunk, merge, 0)


def _dilated_attention(qkv3, bias_tab):
    B, _, S, _ = qkv3.shape
    head_spec = lambda col0: pl.BlockSpec((None, None, S, HEAD_DIM), lambda b, h: (b, col0 + h, 0, 0))
    bias_spec = lambda g: pl.BlockSpec(
        (None, ATTN_BLOCK, 2 * ATTN_BLOCK), lambda b, h: (g * KV_HEADS + h, 0, 0))
    return pl.pallas_call(
        _dilated_kernel,
        grid=(B, KV_HEADS),
        in_specs=[
            head_spec(0), head_spec(KV_HEADS), head_spec(2 * KV_HEADS),
            head_spec(Q_HEADS), head_spec(Q_HEADS + KV_HEADS),
            bias_spec(0), bias_spec(1), bias_spec(2),
        ],
        out_specs=pl.BlockSpec((None, S, HEAD_DIM), lambda b, h: (b, 0, h)),
        out_shape=jax.ShapeDtypeStruct((B, S, KV_HEADS * HEAD_DIM), BF16),
        scratch_shapes=[
            pltpu.VMEM((N_GROUPS, S, HEAD_DIM), BF16),
            pltpu.VMEM((N_GROUPS, S, HEAD_DIM), BF16),
            pltpu.VMEM((N_GROUPS, S, 2 * HEAD_DIM), BF16),
            pltpu.VMEM((S, HEAD_DIM), F32),
            pltpu.VMEM((S, HEAD_DIM), F32),
            pltpu.VMEM((S, HEAD_DIM), F32),
            pltpu.VMEM((S // ATTN_BLOCK, ATTN_BLOCK, 2 * ATTN_BLOCK), F32),
            pltpu.VMEM((S // ATTN_BLOCK, ATTN_BLOCK, 2 * ATTN_BLOCK), BF16),
            pltpu.VMEM((N_GROUPS, S // ATTN_BLOCK * ATTN_PITCH, HEAD_DIM), F32),
            pltpu.VMEM((N_GROUPS, S // ATTN_BLOCK * ATTN_PITCH, HEAD_DIM), F32),
            pltpu.VMEM((N_GROUPS, S // ATTN_BLOCK * ATTN_PITCH, HEAD_DIM), F32),
        ],
        compiler_params=pltpu.CompilerParams(
            dimension_semantics=("parallel", "parallel"), vmem_limit_bytes=VMEM_LIMIT_BYTES),
        name="dilated_attn",
    )(qkv3, qkv3, qkv3, qkv3, qkv3, bias_tab, bias_tab, bias_tab)


def _gelu_tanh(x):
    return 0.5 * x * (1.0 + jnp.tanh(math.sqrt(2.0 / math.pi) * (x + 0.044715 * (x * x * x))))


def _mix_out_kernel(y_ref, o_ref, mg_ref, x_ref, wbr_ref, wba_ref, wout_ref, gmix_ref,
                    gpre_ref, wg_ref, wu_ref, cw_ref, cb_ref, wd_ref, gpost_ref, out_ref,
                    mrg_scr, mix_scr, h_scr, hn_scr, gate_scr, up_scr, act_scr, ff_scr, *, tiles_per_seq):
    i = pl.program_id(0)
    sr, n_sub = FF_SR, FF_TM // FF_SR
    n_c = FFN_WIDTH // FF_TC

    @pl.when(i % tiles_per_seq == 0)
    def _():
        gate_scr[0, 0:SUBLANES, :] = jnp.zeros((SUBLANES, FFN_WIDTH), F32)

    def rows(r):
        return slice(r * sr, (r + 1) * sr)

    def stage_a(r, c):
        cols = slice(c * FF_DN, (c + 1) * FF_DN)
        br = jnp.dot(y_ref[rows(r), :], wbr_ref[:, cols], preferred_element_type=F32)
        ba = jnp.dot(o_ref[rows(r), :], wba_ref[:, cols], preferred_element_type=F32)
        g_rnn = _sigmoid(mg_ref[rows(r), c * FF_DN:(c + 1) * FF_DN].astype(F32))
        g_att = _sigmoid(mg_ref[rows(r), D_MODEL + c * FF_DN:D_MODEL + (c + 1) * FF_DN].astype(F32))
        mrg_scr[r % 2, :, cols] = (g_rnn * br + g_att * ba).astype(BF16)

    def stage_c(r, c):
        cols = slice(c * FF_DN, (c + 1) * FF_DN)
        mix_scr[r % 2, :, cols] = jnp.dot(mrg_scr[r % 2], wout_ref[:, cols], preferred_element_type=F32)

    def stage_n(r):
        h = x_ref[rows(r), :] + _rms_norm(mix_scr[r % 2], gmix_ref[...])
        h_scr[r] = h
        hn_scr[r % 2] = _rms_norm(h, gpre_ref[...]).astype(BF16)

    def stage_g(r, c, w_ref):
        cols = slice(c * FF_TC, (c + 1) * FF_TC)
        res = jnp.dot(hn_scr[r % 2], w_ref[:, cols], preferred_element_type=F32)
        if w_ref is wg_ref:
            gate_scr[r % 2, SUBLANES:SUBLANES + sr, cols] = res
            gate_scr[(r + 1) % 2, 0:SUBLANES, cols] = res[sr - SUBLANES:sr, :]
        else:
            up_scr[r % 2, :, cols] = res

    def stage_v(r, c):
        cols = slice(c * FF_TC, (c + 1) * FF_TC)
        conv = cb_ref[:, cols] + cw_ref[FFN_CONV - 1:FFN_CONV, cols] * gate_scr[r % 2, SUBLANES:SUBLANES + sr, cols]
        for k in range(FFN_CONV - 1):
            back = FFN_CONV - 1 - k
            conv = conv + cw_ref[k:k + 1, cols] * gate_scr[r % 2, SUBLANES - back:SUBLANES - back + sr, cols]
        act_scr[r % 2, :, cols] = (_gelu_tanh(conv) * up_scr[r % 2, :, cols]).astype(BF16)

    def stage_d(r, c):
        cols = slice(c * FF_DN, (c + 1) * FF_DN)
        ff_scr[r % 2, :, cols] = jnp.dot(act_scr[r % 2], wd_ref[:, cols], preferred_element_type=F32)

    def stage_e(r):
        out_ref[rows(r), :] = h_scr[r] + _rms_norm(ff_scr[r % 2], gpost_ref[...])

    n_d = D_MODEL // FF_DN
    live = lambda r: 0 <= r < n_sub
    for t in range(n_sub + 6):
        for c in range(n_d):
            if live(t):
                stage_a(t, c)
            if live(t - 1):
                stage_c(t - 1, c)
        if live(t - 2):
            stage_n(t - 2)
        for c in range(n_c):
            if live(t - 4):
                stage_v(t - 4, c)
            if live(t - 3):
                stage_g(t - 3, c, wg_ref)
                stage_g(t - 3, c, wu_ref)
            if live(t - 5):
                for d in range(c * n_d // n_c, (c + 1) * n_d // n_c):
                    stage_d(t - 5, d)
        if live(t - 6):
            stage_e(t - 6)


def _mix_out(y2, o2, mgates, x2, wbr, wba, wout, g_mix_post, g_pre, wg, wu, conv_w, conv_b, wd, g_post, seq_len):
    T = x2.shape[0]
    const = lambda i: (0, 0)
    row = lambda width: pl.BlockSpec((FF_TM, width), lambda i: (i, 0))
    resident = lambda shape: pl.BlockSpec(shape, const, pipeline_mode=pl.Buffered(1))
    return pl.pallas_call(
        functools.partial(_mix_out_kernel, tiles_per_seq=seq_len // FF_TM),
        grid=(T // FF_TM,),
        in_specs=[
            row(RNN_WIDTH), row(KV_HEADS * HEAD_DIM), row(GATE_WIDTH), row(D_MODEL),
            resident((RNN_WIDTH, D_MODEL)),
            resident((KV_HEADS * HEAD_DIM, D_MODEL)),
            resident((D_MODEL, D_MODEL)),
            pl.BlockSpec((1, D_MODEL), const),
            pl.BlockSpec((1, D_MODEL), const),
            resident((D_MODEL, FFN_WIDTH)),
            resident((D_MODEL, FFN_WIDTH)),
            pl.BlockSpec((FFN_CONV, FFN_WIDTH), const),
            pl.BlockSpec((1, FFN_WIDTH), const),
            resident((FFN_WIDTH, D_MODEL)),
            pl.BlockSpec((1, D_MODEL), const),
        ],
        out_specs=row(D_MODEL),
        out_shape=jax.ShapeDtypeStruct((T, D_MODEL), F32),
        scratch_shapes=[
            pltpu.VMEM((2, FF_SR, D_MODEL), BF16),
            pltpu.VMEM((2, FF_SR, D_MODEL), F32),
            pltpu.VMEM((FF_TM // FF_SR, FF_SR, D_MODEL), F32),
            pltpu.VMEM((2, FF_SR, D_MODEL), BF16),
            pltpu.VMEM((2, SUBLANES + FF_SR, FFN_WIDTH), F32),
            pltpu.VMEM((2, FF_SR, FFN_WIDTH), F32),
            pltpu.VMEM((2, FF_SR, FFN_WIDTH), BF16),
            pltpu.VMEM((2, FF_SR, D_MODEL), F32),
        ],
        compiler_params=pltpu.CompilerParams(
            dimension_semantics=("arbitrary",), vmem_limit_bytes=VMEM_LIMIT_BYTES),
        name="mix_out",
    )(y2, o2, mgates, x2, wbr, wba, wout, g_mix_post, g_pre, wg, wu, conv_w, conv_b, wd, g_post)


def kernel(x, rel_bias, norm_mix_pre, norm_mix_post, w_in, conv_rnn_w, conv_rnn_b, w_rg_a, b_rg_a,
           w_rg_x, b_rg_x, lru_lambda, w_branch_rnn, w_branch_att, w_out, norm_ffn_pre, norm_ffn_post,
           w_ffn_gate, w_ffn_up, conv_ffn_w, conv_ffn_b, w_ffn_down):
    B, S, D = x.shape
    assert D == D_MODEL and S % FF_TM == 0 and S % RG_TS == 0 and B % RG_NB == 0
    assert (FF_TM // FF_SR) % 2 == 0, "the conv halo hand-off between grid steps relies on an even sub-tile count"
    assert S == DILATED_CONFIGS[-1][0], "attention block structure assumes window == sequence for the widest group"
    depth = w_in.shape[0]
    T = B * S
    bias_tab = _bias_tables(rel_bias)
    h = x.reshape(T, D)
    for l in range(depth):
        wax = jnp.concatenate([w_rg_a[l], w_rg_x[l]], axis=-1).astype(BF16)
        bax = jnp.stack([b_rg_a[l], b_rg_x[l]], axis=0)
        y_rnn, qkv, gates = _mix_in(h.reshape(B, S, D), norm_mix_pre[l][None], w_in[l].astype(BF16),
                                    conv_rnn_w[l], conv_rnn_b[l][None], wax, bax, lru_lambda[l][None])
        o_att = _dilated_attention(qkv, bias_tab)
        h = _mix_out(y_rnn.reshape(T, RNN_WIDTH), o_att.reshape(T, KV_HEADS * HEAD_DIM),
                     gates.reshape(T, GATE_WIDTH), h,
                     w_branch_rnn[l].astype(BF16), w_branch_att[l].astype(BF16), w_out[l].astype(BF16),
                     norm_mix_post[l][None], norm_ffn_pre[l][None],
                     w_ffn_gate[l].astype(BF16), w_ffn_up[l].astype(BF16), conv_ffn_w[l], conv_ffn_b[l][None],
                     w_ffn_down[l].astype(BF16), norm_ffn_post[l][None], S)
    return h.reshape(B, S, D)
```

```python
import functools
import math

import numpy as np
import jax
import jax.numpy as jnp
from jax import lax
from jax.experimental import pallas as pl
from jax.experimental.pallas import tpu as pltpu

F32 = jnp.float32
BF16 = jnp.bfloat16

D_MODEL = 1024
RNN_WIDTH = 1280
RNN_BLOCKS = 10
RNN_BLOCK = 128
RNN_CONV = 4
LRU_C = 8.0
HEAD_DIM = 128
KV_HEADS = 4
DILATED_CONFIGS = ((128, 1), (512, 4), (2048, 16))
N_GROUPS = 3
Q_HEADS = 12
ATTN_BLOCK = 128
ATTN_PITCH = ATTN_BLOCK + 8
REL_BUCKETS = 32
REL_MAX_DIST = 2048
FFN_WIDTH = 3072
FFN_CONV = 3
EPS = 1e-6
QKV_HEADS = Q_HEADS + 2 * KV_HEADS
QKV_WIDTH = QKV_HEADS * HEAD_DIM
GATE_WIDTH = 2 * D_MODEL
IN_WIDTH = RNN_WIDTH + QKV_WIDTH + GATE_WIDTH

SUBLANES = 8
VMEM_LIMIT_BYTES = 56 * 1024 * 1024

MASK_VALUE = -1e30
LOG2E = math.log2(math.e)
QK_SCALE_LOG2 = HEAD_DIM ** -0.5 * LOG2E

RG_TS = 64
RG_NB = 8
RG_PITCH = RG_TS + 8
RG_TC = 16
MIX_TN = 256
FF_TM = 512
FF_SR = 256
FF_TC = 256
FF_DN = 256


def _rms_norm(x, g):
    ms = jnp.mean(x * x, axis=-1, keepdims=True)
    return x * lax.rsqrt(ms + EPS) * g


def _sigmoid(x):
    return 0.5 + 0.5 * jnp.tanh(0.5 * x)


def _t5_bucket(dist):
    max_exact = REL_BUCKETS // 2
    d = np.maximum(dist, 1).astype(np.float32)
    large = max_exact + np.log(d / max_exact) / math.log(REL_MAX_DIST / max_exact) * (REL_BUCKETS - max_exact)
    large = np.minimum(large.astype(np.int32), REL_BUCKETS - 1)
    return np.where(dist < max_exact, dist, large).astype(np.int32)


def _bucket_maps():
    qi = np.arange(ATTN_BLOCK)[:, None]
    kj = np.arange(2 * ATTN_BLOCK)[None, :]
    delta = ATTN_BLOCK + qi - kj
    maps = []
    for window, dilation in DILATED_CONFIGS:
        n_back = window // dilation
        valid = (delta >= 0) & (delta <= n_back)
        bucket = _t5_bucket(np.maximum(delta, 0) * dilation)
        maps.append(np.where(valid, bucket, -1).astype(np.int32))
    return np.stack(maps, axis=0)


def _bias_kernel(rb_ref, bucket_ref, out_ref):
    head = pl.program_id(0)
    bk = bucket_ref[0]
    acc = jnp.full(bk.shape, MASK_VALUE, F32)
    for b in range(REL_BUCKETS):
        acc = jnp.where(bk == b, rb_ref[b, head] * LOG2E, acc)
    out_ref[0] = acc


def _bias_tables(rel_bias):
    buckets = jnp.asarray(_bucket_maps())
    return pl.pallas_call(
        _bias_kernel,
        grid=(Q_HEADS,),
        in_specs=[
            pl.BlockSpec(memory_space=pltpu.SMEM),
            pl.BlockSpec((1, ATTN_BLOCK, 2 * ATTN_BLOCK), lambda h: (h // KV_HEADS, 0, 0)),
        ],
        out_specs=pl.BlockSpec((1, ATTN_BLOCK, 2 * ATTN_BLOCK), lambda h: (h, 0, 0)),
        out_shape=jax.ShapeDtypeStruct((Q_HEADS, ATTN_BLOCK, 2 * ATTN_BLOCK), F32),
        name="bias_table",
    )(rel_bias, buckets)


def _rglru_chunk(n, t0, xr_scr, y_scr, carry, cw_ref, cb_ref, wax_ref, bax_ref, lam_ref):
    pitch, tc = RG_PITCH, RG_TC
    cols = slice(n * RNN_BLOCK, (n + 1) * RNN_BLOCK)
    bcast = lambda row: jnp.broadcast_to(row, (RG_NB, RNN_BLOCK))
    w = [bcast(cw_ref[k:k + 1, cols]) for k in range(RNN_CONV)]
    cb = bcast(cb_ref[:, cols])
    xs = [carry[n, k] for k in range(RNN_CONV - 1)]
    xs += [xr_scr[n, pl.ds(t0 + j, RG_NB, stride=pitch), :] for j in range(tc)]
    xc = jnp.concatenate(
        [cb + w[3] * xs[j + 3] + w[2] * xs[j + 2] + w[1] * xs[j + 1] + w[0] * xs[j] for j in range(tc)],
        axis=0)
    g = jnp.dot(xc.astype(BF16), wax_ref[n], preferred_element_type=F32)
    r = _sigmoid(g[:, 0:RNN_BLOCK] + bax_ref[0:1, cols])
    i = _sigmoid(g[:, RNN_BLOCK:2 * RNN_BLOCK] + bax_ref[1:2, cols])
    neg_lam = -lam_ref[:, cols]
    softplus = jnp.maximum(neg_lam, 0.0) + jnp.log1p(jnp.exp(-jnp.abs(neg_lam)))
    log_a = (-LRU_C * softplus) * r
    a = jnp.exp(log_a)
    th = jnp.tanh(log_a)
    u = jnp.sqrt(-2.0 * th / (1.0 - th)) * (i * xc)
    h = carry[n, RNN_CONV - 1]
    for j in range(tc):
        h = a[j * RG_NB:(j + 1) * RG_NB] * h + u[j * RG_NB:(j + 1) * RG_NB]
        y_scr[n, pl.ds(t0 + j, RG_NB, stride=pitch), :] = h
    for k in range(RNN_CONV - 1):
        carry[n, k] = xs[tc + k]
    carry[n, RNN_CONV - 1] = h


def _mix_in_kernel(x_ref, g_ref, w_ref, cw_ref, cb_ref, wax_ref, bax_ref, lam_ref,
                   y_ref, qkv_ref, gate_ref, xr_scr, y_scr, carry):
    s = pl.program_id(1)
    ts, pitch = RG_TS, RG_PITCH
    rows = RG_NB * ts

    @pl.when(s == 0)
    def _():
        carry[...] = jnp.zeros_like(carry)

    hn = _rms_norm(x_ref[...].reshape(rows, D_MODEL), g_ref[...]).astype(BF16)
    c0, c1 = RNN_WIDTH, RNN_WIDTH + QKV_WIDTH

    def project(pair):
        width = MIX_TN * len(pair)
        both = jnp.dot(hn, w_ref[:, pair[0]:pair[0] + width], preferred_element_type=F32)
        for col in pair:
            route(col, both[:, col - pair[0]:col - pair[0] + MIX_TN])

    def route(col, out):
        if col < c0:
            for n in range(col // RNN_BLOCK, (col + MIX_TN) // RNN_BLOCK):
                lanes = slice(n * RNN_BLOCK - col, (n + 1) * RNN_BLOCK - col)
                for b in range(RG_NB):
                    xr_scr[n, b * pitch:b * pitch + ts, :] = out[b * ts:(b + 1) * ts, lanes]
        elif col < c1:
            for hd in range((col - c0) // HEAD_DIM, (col - c0 + MIX_TN) // HEAD_DIM):
                lanes = slice(c0 + hd * HEAD_DIM - col, c0 + (hd + 1) * HEAD_DIM - col)
                qkv_ref[:, hd, :, :] = out[:, lanes].reshape(RG_NB, ts, HEAD_DIM)
        else:
            gate_ref[:, :, col - c1:col - c1 + MIX_TN] = out.astype(BF16).reshape(RG_NB, ts, MIX_TN)

    def recur(n, ci):
        _rglru_chunk(n, ci * RG_TC, xr_scr, y_scr, carry, cw_ref, cb_ref, wax_ref, bax_ref, lam_ref)
        if ci == ts // RG_TC - 1:
            for b in range(RG_NB):
                y_ref[b, :, n * RNN_BLOCK:(n + 1) * RNN_BLOCK] = (
                    y_scr[n, b * pitch:b * pitch + ts, :].astype(BF16))

    slabs_per_block = MIX_TN // RNN_BLOCK
    units = [(n, ci) for n0 in range(0, RNN_BLOCKS, slabs_per_block) for ci in range(ts // RG_TC)
             for n in range(n0, n0 + slabs_per_block)]
    cols = list(range(0, IN_WIDTH, MIX_TN))
    pairs = [cols[j:j + 2] for j in range(0, len(cols), 2)]
    project(pairs[0])
    done = 1
    for k, (n, ci) in enumerate(units):
        recur(n, ci)
        want = 1 + ((k + 1) * (len(pairs) - 1) + len(units) - 1) // len(units)
        while done < want:
            project(pairs[done])
            done += 1


def _mix_in(x3, g, w_bf, conv_w, conv_b, wax, bax, lam):
    B, S, _ = x3.shape
    const2 = lambda b, s: (0, 0)
    tile = lambda width: pl.BlockSpec((RG_NB, RG_TS, width), lambda b, s: (b, s, 0))
    scan_scratch = pltpu.VMEM((RNN_BLOCKS, RG_NB * RG_PITCH, RNN_BLOCK), F32)
    return pl.pallas_call(
        _mix_in_kernel,
        grid=(B // RG_NB, S // RG_TS),
        in_specs=[
            tile(D_MODEL),
            pl.BlockSpec((1, D_MODEL), const2),
            pl.BlockSpec((D_MODEL, IN_WIDTH), const2, pipeline_mode=pl.Buffered(1)),
            pl.BlockSpec((RNN_CONV, RNN_WIDTH), const2),
            pl.BlockSpec((1, RNN_WIDTH), const2),
            pl.BlockSpec((RNN_BLOCKS, RNN_BLOCK, 2 * RNN_BLOCK), lambda b, s: (0, 0, 0)),
            pl.BlockSpec((2, RNN_WIDTH), const2),
            pl.BlockSpec((1, RNN_WIDTH), const2),
        ],
        out_specs=[
            tile(RNN_WIDTH),
            pl.BlockSpec((RG_NB, QKV_HEADS, RG_TS, HEAD_DIM), lambda b, s: (b, 0, s, 0)),
            tile(GATE_WIDTH),
        ],
        out_shape=[
            jax.ShapeDtypeStruct((B, S, RNN_WIDTH), BF16),
            jax.ShapeDtypeStruct((B, QKV_HEADS, S, HEAD_DIM), F32),
            jax.ShapeDtypeStruct((B, S, GATE_WIDTH), BF16),
        ],
        scratch_shapes=[
            scan_scratch, scan_scratch,
            pltpu.VMEM((RNN_BLOCKS, RNN_CONV, RG_NB, RNN_BLOCK), F32),
        ],
        compiler_params=pltpu.CompilerParams(
            dimension_semantics=("parallel", "arbitrary"), vmem_limit_bytes=VMEM_LIMIT_BYTES),
        name="mix_in",
    )(x3, g, w_bf, conv_w, conv_b, wax, bax, lam)


def _dilated_kernel(q1_ref, q2_ref, q3_ref, k_ref, v_ref, b1_ref, b2_ref, b3_ref, o_ref,
                    qp, kp, vp, tq, tk, tv, s_scr, p_scr, o_scr, m_scr, d_scr):
    S = k_ref.shape[0]
    blk = ATTN_BLOCK
    n_blocks = S // blk
    quarter = S // 4
    nt = (((1,), (1,)), ((), ()))

    def put(g, r0, n, q, k, v):
        qp[g, r0:r0 + n, :] = (q * QK_SCALE_LOG2).astype(BF16)
        kp[g, r0:r0 + n, :] = k.astype(BF16)
        vp[g, r0:r0 + n, 0:HEAD_DIM] = v.astype(BF16)

    for g in range(N_GROUPS):
        vp[g, :, HEAD_DIM:2 * HEAD_DIM] = jnp.ones((S, HEAD_DIM), BF16)
    for c in range(4):
        r = slice(c * quarter, (c + 1) * quarter)
        put(0, c * quarter, quarter, q1_ref[r, :], k_ref[r, :], v_ref[r, :])
    for c in range(4):
        sr = pl.ds(c, quarter, stride=4)
        r = slice(c * quarter, (c + 1) * quarter)
        k4, v4 = k_ref[sr, :], v_ref[sr, :]
        put(1, c * quarter, quarter, q2_ref[sr, :], k4, v4)
        tk[r, :] = k4
        tv[r, :] = v4
        tq[r, :] = q3_ref[sr, :]
    for c4 in range(4):
        for j in range(4):
            sr = pl.ds(c4 * quarter + j, blk, stride=4)
            put(2, (4 * j + c4) * blk, blk, tq[sr, :], tk[sr, :], tv[sr, :])

    def run_group(g, b_ref, dil):
        nb = S // dil // blk

        def nat_rows(p):
            c, n = divmod(p, nb)
            if nb == 1:
                return slice(c * ATTN_PITCH, c * ATTN_PITCH + blk)
            start = n * (blk * dil) + c
            return pl.ds(start, blk) if dil == 1 else pl.ds(start, blk, stride=dil)

        def key_rows(p):
            has_prev = p % nb != 0
            return (slice((p - 1) * blk, (p + 1) * blk) if has_prev else slice(p * blk, (p + 1) * blk),
                    slice(0, 2 * blk) if has_prev else slice(blk, 2 * blk))

        for p in range(n_blocks):
            kr, cols = key_rows(p)
            s = lax.dot_general(qp[g, p * blk:(p + 1) * blk, :], kp[g, kr, :], nt, preferred_element_type=F32)
            s_scr[p, :, cols] = s + b_ref[:, cols]
        for p in range(n_blocks):
            _, cols = key_rows(p)
            m = jnp.max(s_scr[p, :, cols], axis=-1, keepdims=True)
            m_scr[g, nat_rows(p), :] = jnp.broadcast_to(m, (blk, HEAD_DIM))
            p_scr[p, :, cols] = jnp.exp2(s_scr[p, :, cols] - m).astype(BF16)
        for p in range(n_blocks):
            kr, cols = key_rows(p)
            od = jnp.dot(p_scr[p, :, cols], vp[g, kr, :], preferred_element_type=F32)
            o_scr[g, nat_rows(p), :] = od[:, 0:HEAD_DIM]
            d_scr[g, nat_rows(p), :] = od[:, HEAD_DIM:2 * HEAD_DIM]

    run_group(0, b1_ref, DILATED_CONFIGS[0][1])
    run_group(1, b2_ref, DILATED_CONFIGS[1][1])
    run_group(2, b3_ref, DILATED_CONFIGS[2][1])

    chunk = 256

    last_dil = DILATED_CONFIGS[2][1]

    def widest(scr, i):
        per = chunk // last_dil
        return jnp.concatenate(
            [scr[2, pl.ds(i * per + j, last_dil, stride=ATTN_PITCH), :] for j in range(per)], axis=0)

    def merge(i, carry):
        r = pl.ds(pl.multiple_of(i * chunk, chunk), chunk)
        m0, m1, m2 = m_scr[0, r, :], m_scr[1, r, :], widest(m_scr, i)
        mx = jnp.maximum(jnp.maximum(m0, m1), m2)
        w0, w1, w2 = jnp.exp2(m0 - mx), jnp.exp2(m1 - mx), jnp.exp2(m2 - mx)
        num = w0 * o_scr[0, r, :] + w1 * o_scr[1, r, :] + w2 * widest(o_scr, i)
        z = w0 * d_scr[0, r, :] + w1 * d_scr[1, r, :] + w2 * widest(d_scr, i)
        o_ref[r, :] = (num / z).astype(o_ref.dtype)
        return carry

    lax.fori_loop(0, S // chunk, merge, 0)


def _dilated_attention(qkv3, bias_tab):
    B, _, S, _ = qkv3.shape
    head_spec = lambda col0: pl.BlockSpec((None, None, S, HEAD_DIM), lambda b, h: (b, col0 + h, 0, 0))
    bias_spec = lambda g: pl.BlockSpec(
        (None, ATTN_BLOCK, 2 * ATTN_BLOCK), lambda b, h: (g * KV_HEADS + h, 0, 0))
    return pl.pallas_call(
        _dilated_kernel,
        grid=(B, KV_HEADS),
        in_specs=[
            head_spec(0), head_spec(KV_HEADS), head_spec(2 * KV_HEADS),
            head_spec(Q_HEADS), head_spec(Q_HEADS + KV_HEADS),
            bias_spec(0), bias_spec(1), bias_spec(2),
        ],
        out_specs=pl.BlockSpec((None, S, HEAD_DIM), lambda b, h: (b, 0, h)),
        out_shape=jax.ShapeDtypeStruct((B, S, KV_HEADS * HEAD_DIM), BF16),
        scratch_shapes=[
            pltpu.VMEM((N_GROUPS, S, HEAD_DIM), BF16),
            pltpu.VMEM((N_GROUPS, S, HEAD_DIM), BF16),
            pltpu.VMEM((N_GROUPS, S, 2 * HEAD_DIM), BF16),
            pltpu.VMEM((S, HEAD_DIM), F32),
            pltpu.VMEM((S, HEAD_DIM), F32),
            pltpu.VMEM((S, HEAD_DIM), F32),
            pltpu.VMEM((S // ATTN_BLOCK, ATTN_BLOCK, 2 * ATTN_BLOCK), F32),
            pltpu.VMEM((S // ATTN_BLOCK, ATTN_BLOCK, 2 * ATTN_BLOCK), BF16),
            pltpu.VMEM((N_GROUPS, S // ATTN_BLOCK * ATTN_PITCH, HEAD_DIM), F32),
            pltpu.VMEM((N_GROUPS, S // ATTN_BLOCK * ATTN_PITCH, HEAD_DIM), F32),
            pltpu.VMEM((N_GROUPS, S // ATTN_BLOCK * ATTN_PITCH, HEAD_DIM), F32),
        ],
        compiler_params=pltpu.CompilerParams(
            dimension_semantics=("parallel", "parallel"), vmem_limit_bytes=VMEM_LIMIT_BYTES),
        name="dilated_attn",
    )(qkv3, qkv3, qkv3, qkv3, qkv3, bias_tab, bias_tab, bias_tab)


def _gelu_tanh(x):
    return 0.5 * x * (1.0 + jnp.tanh(math.sqrt(2.0 / math.pi) * (x + 0.044715 * (x * x * x))))


def _mix_out_kernel(y_ref, o_ref, mg_ref, x_ref, wbr_ref, wba_ref, wout_ref, gmix_ref,
                    gpre_ref, wgu_ref, cw_ref, cb_ref, wd_ref, gpost_ref, out_ref,
                    mrg_scr, mix_scr, h_scr, hn_scr, gate_scr, up_scr, act_scr, ff_scr, *, tiles_per_seq):
    i = pl.program_id(0)
    sr, n_sub = FF_SR, FF_TM // FF_SR
    n_c = FFN_WIDTH // FF_TC

    @pl.when(i % tiles_per_seq == 0)
    def _():
        gate_scr[0, 0:SUBLANES, :] = jnp.zeros((SUBLANES, FFN_WIDTH), F32)

    def rows(r):
        return slice(r * sr, (r + 1) * sr)

    def stage_a(r, c):
        cols = slice(c * FF_DN, (c + 1) * FF_DN)
        br = jnp.dot(y_ref[rows(r), :], wbr_ref[:, cols], preferred_element_type=F32)
        ba = jnp.dot(o_ref[rows(r), :], wba_ref[:, cols], preferred_element_type=F32)
        g_rnn = _sigmoid(mg_ref[rows(r), c * FF_DN:(c + 1) * FF_DN].astype(F32))
        g_att = _sigmoid(mg_ref[rows(r), D_MODEL + c * FF_DN:D_MODEL + (c + 1) * FF_DN].astype(F32))
        mrg_scr[r % 2, :, cols] = (g_rnn * br + g_att * ba).astype(BF16)

    def stage_c(r, c):
        cols = slice(c * FF_DN, (c + 1) * FF_DN)
        mix_scr[r % 2, :, cols] = jnp.dot(mrg_scr[r % 2], wout_ref[:, cols], preferred_element_type=F32)

    def stage_n(r):
        h = x_ref[rows(r), :] + _rms_norm(mix_scr[r % 2], gmix_ref[...])
        h_scr[r] = h
        hn_scr[r % 2] = _rms_norm(h, gpre_ref[...]).astype(BF16)

    def stage_g(r, c):
        cols = slice(c * FF_TC, (c + 1) * FF_TC)
        res = jnp.dot(hn_scr[r % 2], wgu_ref[:, 2 * c * FF_TC:2 * (c + 1) * FF_TC], preferred_element_type=F32)
        gate = res[:, 0:FF_TC]
        gate_scr[r % 2, SUBLANES:SUBLANES + sr, cols] = gate
        gate_scr[(r + 1) % 2, 0:SUBLANES, cols] = gate[sr - SUBLANES:sr, :]
        up_scr[r % 2, :, cols] = res[:, FF_TC:2 * FF_TC]

    def stage_v(r, c):
        cols = slice(c * FF_TC, (c + 1) * FF_TC)
        conv = cb_ref[:, cols] + cw_ref[FFN_CONV - 1:FFN_CONV, cols] * gate_scr[r % 2, SUBLANES:SUBLANES + sr, cols]
        for k in range(FFN_CONV - 1):
            back = FFN_CONV - 1 - k
            conv = conv + cw_ref[k:k + 1, cols] * gate_scr[r % 2, SUBLANES - back:SUBLANES - back + sr, cols]
        act_scr[r % 2, :, cols] = (_gelu_tanh(conv) * up_scr[r % 2, :, cols]).astype(BF16)

    def stage_d(r, c):
        cols = slice(c * FF_DN, (c + 1) * FF_DN)
        ff_scr[r % 2, :, cols] = jnp.dot(act_scr[r % 2], wd_ref[:, cols], preferred_element_type=F32)

    def stage_e(r):
        out_ref[rows(r), :] = h_scr[r] + _rms_norm(ff_scr[r % 2], gpost_ref[...])

    n_d = D_MODEL // FF_DN
    live = lambda r: 0 <= r < n_sub
    for t in range(n_sub + 6):
        for c in range(n_d):
            if live(t):
                stage_a(t, c)
            if live(t - 1):
                stage_c(t - 1, c)
        if live(t - 2):
            stage_n(t - 2)
        for c in range(n_c):
            if live(t - 4):
                stage_v(t - 4, c)
            if live(t - 3):
                stage_g(t - 3, c)
            if live(t - 5):
                for d in range(c * n_d // n_c, (c + 1) * n_d // n_c):
                    stage_d(t - 5, d)
        if live(t - 6):
            stage_e(t - 6)


def _mix_out(y2, o2, mgates, x2, wbr, wba, wout, g_mix_post, g_pre, wgu, conv_w, conv_b, wd, g_post, seq_len):
    T = x2.shape[0]
    const = lambda i: (0, 0)
    row = lambda width: pl.BlockSpec((FF_TM, width), lambda i: (i, 0))
    resident = lambda shape: pl.BlockSpec(shape, const, pipeline_mode=pl.Buffered(1))
    return pl.pallas_call(
        functools.partial(_mix_out_kernel, tiles_per_seq=seq_len // FF_TM),
        grid=(T // FF_TM,),
        in_specs=[
            row(RNN_WIDTH), row(KV_HEADS * HEAD_DIM), row(GATE_WIDTH), row(D_MODEL),
            resident((RNN_WIDTH, D_MODEL)),
            resident((KV_HEADS * HEAD_DIM, D_MODEL)),
            resident((D_MODEL, D_MODEL)),
            pl.BlockSpec((1, D_MODEL), const),
            pl.BlockSpec((1, D_MODEL), const),
            resident((D_MODEL, 2 * FFN_WIDTH)),
            pl.BlockSpec((FFN_CONV, FFN_WIDTH), const),
            pl.BlockSpec((1, FFN_WIDTH), const),
            resident((FFN_WIDTH, D_MODEL)),
            pl.BlockSpec((1, D_MODEL), const),
        ],
        out_specs=row(D_MODEL),
        out_shape=jax.ShapeDtypeStruct((T, D_MODEL), F32),
        scratch_shapes=[
            pltpu.VMEM((2, FF_SR, D_MODEL), BF16),
            pltpu.VMEM((2, FF_SR, D_MODEL), F32),
            pltpu.VMEM((FF_TM // FF_SR, FF_SR, D_MODEL), F32),
            pltpu.VMEM((2, FF_SR, D_MODEL), BF16),
            pltpu.VMEM((2, SUBLANES + FF_SR, FFN_WIDTH), F32),
            pltpu.VMEM((2, FF_SR, FFN_WIDTH), F32),
            pltpu.VMEM((2, FF_SR, FFN_WIDTH), BF16),
            pltpu.VMEM((2, FF_SR, D_MODEL), F32),
        ],
        compiler_params=pltpu.CompilerParams(
            dimension_semantics=("arbitrary",), vmem_limit_bytes=VMEM_LIMIT_BYTES),
        name="mix_out",
    )(y2, o2, mgates, x2, wbr, wba, wout, g_mix_post, g_pre, wgu, conv_w, conv_b, wd, g_post)


def kernel(x, rel_bias, norm_mix_pre, norm_mix_post, w_in, conv_rnn_w, conv_rnn_b, w_rg_a, b_rg_a,
           w_rg_x, b_rg_x, lru_lambda, w_branch_rnn, w_branch_att, w_out, norm_ffn_pre, norm_ffn_post,
           w_ffn_gate, w_ffn_up, conv_ffn_w, conv_ffn_b, w_ffn_down):
    B, S, D = x.shape
    assert D == D_MODEL and S % FF_TM == 0 and S % RG_TS == 0 and B % RG_NB == 0
    assert (FF_TM // FF_SR) % 2 == 0, "the conv halo hand-off between grid steps relies on an even sub-tile count"
    assert S == DILATED_CONFIGS[-1][0], "attention block structure assumes window == sequence for the widest group"
    depth = w_in.shape[0]
    T = B * S
    bias_tab = _bias_tables(rel_bias)
    h = x.reshape(T, D)
    for l in range(depth):
        wax = jnp.concatenate([w_rg_a[l], w_rg_x[l]], axis=-1).astype(BF16)
        bax = jnp.stack([b_rg_a[l], b_rg_x[l]], axis=0)
        y_rnn, qkv, gates = _mix_in(h.reshape(B, S, D), norm_mix_pre[l][None], w_in[l].astype(BF16),
                                    conv_rnn_w[l], conv_rnn_b[l][None], wax, bax, lru_lambda[l][None])
        o_att = _dilated_attention(qkv, bias_tab)
        n_c = FFN_WIDTH // FF_TC
        wgu = jnp.stack([w_ffn_gate[l].reshape(D, n_c, FF_TC), w_ffn_up[l].reshape(D, n_c, FF_TC)],
                        axis=2).reshape(D, 2 * FFN_WIDTH).astype(BF16)
        h = _mix_out(y_rnn.reshape(T, RNN_WIDTH), o_att.reshape(T, KV_HEADS * HEAD_DIM),
                     gates.reshape(T, GATE_WIDTH), h,
                     w_branch_rnn[l].astype(BF16), w_branch_att[l].astype(BF16), w_out[l].astype(BF16),
                     norm_mix_post[l][None], norm_ffn_pre[l][None],
                     wgu, conv_ffn_w[l], conv_ffn_b[l][None],
                     w_ffn_down[l].astype(BF16), norm_ffn_post[l][None], S)
    return h.reshape(B, S, D)
```

```python
import functools
import math

import numpy as np
import jax
import jax.numpy as jnp
from jax import lax
from jax.experimental import pallas as pl
from jax.experimental.pallas import tpu as pltpu

F32 = jnp.float32
BF16 = jnp.bfloat16

D_MODEL = 1024
RNN_WIDTH = 1280
RNN_BLOCKS = 10
RNN_BLOCK = 128
RNN_CONV = 4
LRU_C = 8.0
HEAD_DIM = 128
KV_HEADS = 4
DILATED_CONFIGS = ((128, 1), (512, 4), (2048, 16))
N_GROUPS = 3
Q_HEADS = 12
ATTN_BLOCK = 128
ATTN_PITCH = ATTN_BLOCK + 8
REL_BUCKETS = 32
REL_MAX_DIST = 2048
FFN_WIDTH = 3072
FFN_CONV = 3
EPS = 1e-6
QKV_HEADS = Q_HEADS + 2 * KV_HEADS
QKV_WIDTH = QKV_HEADS * HEAD_DIM
GATE_WIDTH = 2 * D_MODEL
IN_WIDTH = RNN_WIDTH + QKV_WIDTH + GATE_WIDTH

SUBLANES = 8
VMEM_LIMIT_BYTES = 56 * 1024 * 1024

MASK_VALUE = -1e30
LOG2E = math.log2(math.e)
QK_SCALE_LOG2 = HEAD_DIM ** -0.5 * LOG2E

RG_TS = 64
RG_NB = 8
RG_PITCH = RG_TS + 8
RG_TC = 16
MIX_TN = 256
FF_TM = 512
FF_SR = 256
FF_TC = 256
FF_DN = 256


def _rms_norm(x, g):
    ms = jnp.mean(x * x, axis=-1, keepdims=True)
    return x * lax.rsqrt(ms + EPS) * g


def _sigmoid(x):
    return 0.5 + 0.5 * jnp.tanh(0.5 * x)


def _t5_bucket(dist):
    max_exact = REL_BUCKETS // 2
    d = np.maximum(dist, 1).astype(np.float32)
    large = max_exact + np.log(d / max_exact) / math.log(REL_MAX_DIST / max_exact) * (REL_BUCKETS - max_exact)
    large = np.minimum(large.astype(np.int32), REL_BUCKETS - 1)
    return np.where(dist < max_exact, dist, large).astype(np.int32)


def _bucket_maps():
    qi = np.arange(ATTN_BLOCK)[:, None]
    kj = np.arange(2 * ATTN_BLOCK)[None, :]
    delta = ATTN_BLOCK + qi - kj
    maps = []
    for window, dilation in DILATED_CONFIGS:
        n_back = window // dilation
        valid = (delta >= 0) & (delta <= n_back)
        bucket = _t5_bucket(np.maximum(delta, 0) * dilation)
        maps.append(np.where(valid, bucket, -1).astype(np.int32))
    return np.stack(maps, axis=0)


def _bias_kernel(rb_ref, bucket_ref, out_ref):
    head = pl.program_id(0)
    bk = bucket_ref[0]
    acc = jnp.full(bk.shape, MASK_VALUE, F32)
    for b in range(REL_BUCKETS):
        acc = jnp.where(bk == b, rb_ref[b, head] * LOG2E, acc)
    out_ref[0] = acc


def _bias_tables(rel_bias):
    buckets = jnp.asarray(_bucket_maps())
    return pl.pallas_call(
        _bias_kernel,
        grid=(Q_HEADS,),
        in_specs=[
            pl.BlockSpec(memory_space=pltpu.SMEM),
            pl.BlockSpec((1, ATTN_BLOCK, 2 * ATTN_BLOCK), lambda h: (h // KV_HEADS, 0, 0)),
        ],
        out_specs=pl.BlockSpec((1, ATTN_BLOCK, 2 * ATTN_BLOCK), lambda h: (h, 0, 0)),
        out_shape=jax.ShapeDtypeStruct((Q_HEADS, ATTN_BLOCK, 2 * ATTN_BLOCK), F32),
        name="bias_table",
    )(rel_bias, buckets)


def _rglru_chunk(n, t0, xr_scr, y_scr, carry, cw_ref, cb_ref, wax_ref, bax_ref, lam_ref):
    pitch, tc = RG_PITCH, RG_TC
    cols = slice(n * RNN_BLOCK, (n + 1) * RNN_BLOCK)
    bcast = lambda row: jnp.broadcast_to(row, (RG_NB, RNN_BLOCK))
    w = [bcast(cw_ref[k:k + 1, cols]) for k in range(RNN_CONV)]
    cb = bcast(cb_ref[:, cols])
    xs = [carry[n, k] for k in range(RNN_CONV - 1)]
    xs += [xr_scr[n, pl.ds(t0 + j, RG_NB, stride=pitch), :] for j in range(tc)]
    xc = jnp.concatenate(
        [cb + w[3] * xs[j + 3] + w[2] * xs[j + 2] + w[1] * xs[j + 1] + w[0] * xs[j] for j in range(tc)],
        axis=0)
    g = jnp.dot(xc.astype(BF16), wax_ref[n], preferred_element_type=F32)
    r = _sigmoid(g[:, 0:RNN_BLOCK] + bax_ref[0:1, cols])
    i = _sigmoid(g[:, RNN_BLOCK:2 * RNN_BLOCK] + bax_ref[1:2, cols])
    neg_lam = -lam_ref[:, cols]
    softplus = jnp.maximum(neg_lam, 0.0) + jnp.log1p(jnp.exp(-jnp.abs(neg_lam)))
    log_a = (-LRU_C * softplus) * r
    a = jnp.exp(log_a)
    th = jnp.tanh(log_a)
    u = jnp.sqrt(-2.0 * th / (1.0 - th)) * (i * xc)
    h = carry[n, RNN_CONV - 1]
    for j in range(tc):
        h = a[j * RG_NB:(j + 1) * RG_NB] * h + u[j * RG_NB:(j + 1) * RG_NB]
        y_scr[n, pl.ds(t0 + j, RG_NB, stride=pitch), :] = h
    for k in range(RNN_CONV - 1):
        carry[n, k] = xs[tc + k]
    carry[n, RNN_CONV - 1] = h


def _mix_in_kernel(x_ref, g_ref, w_ref, cw_ref, cb_ref, wax_ref, bax_ref, lam_ref,
                   y_ref, qkv_ref, gate_ref, xr_scr, y_scr, carry):
    s = pl.program_id(1)
    ts, pitch = RG_TS, RG_PITCH
    rows = RG_NB * ts

    @pl.when(s == 0)
    def _():
        carry[...] = jnp.zeros_like(carry)

    hn = _rms_norm(x_ref[...].reshape(rows, D_MODEL), g_ref[...]).astype(BF16)
    c0, c1 = RNN_WIDTH, RNN_WIDTH + QKV_WIDTH

    def project(pair):
        width = MIX_TN * len(pair)
        both = jnp.dot(hn, w_ref[:, pair[0]:pair[0] + width], preferred_element_type=F32)
        for col in pair:
            route(col, both[:, col - pair[0]:col - pair[0] + MIX_TN])

    def route(col, out):
        if col < c0:
            for n in range(col // RNN_BLOCK, (col + MIX_TN) // RNN_BLOCK):
                lanes = slice(n * RNN_BLOCK - col, (n + 1) * RNN_BLOCK - col)
                for b in range(RG_NB):
                    xr_scr[n, b * pitch:b * pitch + ts, :] = out[b * ts:(b + 1) * ts, lanes]
        elif col < c1:
            for hd in range((col - c0) // HEAD_DIM, (col - c0 + MIX_TN) // HEAD_DIM):
                lanes = slice(c0 + hd * HEAD_DIM - col, c0 + (hd + 1) * HEAD_DIM - col)
                qkv_ref[:, hd, :, :] = out[:, lanes].reshape(RG_NB, ts, HEAD_DIM)
        else:
            gate_ref[:, :, col - c1:col - c1 + MIX_TN] = out.astype(BF16).reshape(RG_NB, ts, MIX_TN)

    def recur(n, ci):
        _rglru_chunk(n, ci * RG_TC, xr_scr, y_scr, carry, cw_ref, cb_ref, wax_ref, bax_ref, lam_ref)
        if ci == ts // RG_TC - 1:
            for b in range(RG_NB):
                y_ref[b, :, n * RNN_BLOCK:(n + 1) * RNN_BLOCK] = (
                    y_scr[n, b * pitch:b * pitch + ts, :].astype(BF16))

    slabs_per_block = MIX_TN // RNN_BLOCK
    units = [(n, ci) for n0 in range(0, RNN_BLOCKS, slabs_per_block) for ci in range(ts // RG_TC)
             for n in range(n0, n0 + slabs_per_block)]
    cols = list(range(0, IN_WIDTH, MIX_TN))
    pairs = [cols[j:j + 2] for j in range(0, len(cols), 2)]
    project(pairs[0])
    done = 1
    for k, (n, ci) in enumerate(units):
        recur(n, ci)
        want = 1 + ((k + 1) * (len(pairs) - 1) + len(units) - 1) // len(units)
        while done < want:
            project(pairs[done])
            done += 1


def _mix_in(x3, g, w_bf, conv_w, conv_b, wax, bax, lam):
    B, S, _ = x3.shape
    const2 = lambda b, s: (0, 0)
    tile = lambda width: pl.BlockSpec((RG_NB, RG_TS, width), lambda b, s: (b, s, 0))
    scan_scratch = pltpu.VMEM((RNN_BLOCKS, RG_NB * RG_PITCH, RNN_BLOCK), F32)
    return pl.pallas_call(
        _mix_in_kernel,
        grid=(B // RG_NB, S // RG_TS),
        in_specs=[
            tile(D_MODEL),
            pl.BlockSpec((1, D_MODEL), const2),
            pl.BlockSpec((D_MODEL, IN_WIDTH), const2, pipeline_mode=pl.Buffered(1)),
            pl.BlockSpec((RNN_CONV, RNN_WIDTH), const2),
            pl.BlockSpec((1, RNN_WIDTH), const2),
            pl.BlockSpec((RNN_BLOCKS, RNN_BLOCK, 2 * RNN_BLOCK), lambda b, s: (0, 0, 0)),
            pl.BlockSpec((2, RNN_WIDTH), const2),
            pl.BlockSpec((1, RNN_WIDTH), const2),
        ],
        out_specs=[
            tile(RNN_WIDTH),
            pl.BlockSpec((RG_NB, QKV_HEADS, RG_TS, HEAD_DIM), lambda b, s: (b, 0, s, 0)),
            tile(GATE_WIDTH),
        ],
        out_shape=[
            jax.ShapeDtypeStruct((B, S, RNN_WIDTH), BF16),
            jax.ShapeDtypeStruct((B, QKV_HEADS, S, HEAD_DIM), F32),
            jax.ShapeDtypeStruct((B, S, GATE_WIDTH), BF16),
        ],
        scratch_shapes=[
            scan_scratch, scan_scratch,
            pltpu.VMEM((RNN_BLOCKS, RNN_CONV, RG_NB, RNN_BLOCK), F32),
        ],
        compiler_params=pltpu.CompilerParams(
            dimension_semantics=("parallel", "arbitrary"), vmem_limit_bytes=VMEM_LIMIT_BYTES),
        name="mix_in",
    )(x3, g, w_bf, conv_w, conv_b, wax, bax, lam)


def _dilated_kernel(q1_ref, q2_ref, q3_ref, k_ref, v_ref, b1_ref, b2_ref, b3_ref, o_ref,
                    qp, kp, vp, tq, tk, tv, s_scr, p_scr, o_scr, m_scr, d_scr):
    S = k_ref.shape[0]
    blk = ATTN_BLOCK
    n_blocks = S // blk
    quarter = S // 4
    nt = (((1,), (1,)), ((), ()))

    def put(g, r0, n, q, k, v):
        qp[g, r0:r0 + n, :] = (q * QK_SCALE_LOG2).astype(BF16)
        kp[g, r0:r0 + n, :] = k.astype(BF16)
        vp[g, r0:r0 + n, 0:HEAD_DIM] = v.astype(BF16)

    for g in range(N_GROUPS):
        vp[g, :, HEAD_DIM:2 * HEAD_DIM] = jnp.ones((S, HEAD_DIM), BF16)
    for c in range(4):
        r = slice(c * quarter, (c + 1) * quarter)
        put(0, c * quarter, quarter, q1_ref[r, :], k_ref[r, :], v_ref[r, :])
    for c in range(4):
        sr = pl.ds(c, quarter, stride=4)
        r = slice(c * quarter, (c + 1) * quarter)
        k4, v4 = k_ref[sr, :], v_ref[sr, :]
        put(1, c * quarter, quarter, q2_ref[sr, :], k4, v4)
        tk[r, :] = k4
        tv[r, :] = v4
        tq[r, :] = q3_ref[sr, :]
    for c4 in range(4):
        for j in range(4):
            sr = pl.ds(c4 * quarter + j, blk, stride=4)
            put(2, (4 * j + c4) * blk, blk, tq[sr, :], tk[sr, :], tv[sr, :])

    def run_group(g, b_ref, dil):
        nb = S // dil // blk

        def nat_rows(p):
            c, n = divmod(p, nb)
            if nb == 1:
                return slice(c * ATTN_PITCH, c * ATTN_PITCH + blk)
            start = n * (blk * dil) + c
            return pl.ds(start, blk) if dil == 1 else pl.ds(start, blk, stride=dil)

        def key_rows(p):
            has_prev = p % nb != 0
            return (slice((p - 1) * blk, (p + 1) * blk) if has_prev else slice(p * blk, (p + 1) * blk),
                    slice(0, 2 * blk) if has_prev else slice(blk, 2 * blk))

        for p in range(n_blocks):
            kr, cols = key_rows(p)
            s = lax.dot_general(qp[g, p * blk:(p + 1) * blk, :], kp[g, kr, :], nt, preferred_element_type=F32)
            s_scr[p, :, cols] = s + b_ref[:, cols]
        for p in range(n_blocks):
            _, cols = key_rows(p)
            m = jnp.max(s_scr[p, :, cols], axis=-1, keepdims=True)
            m_scr[g, nat_rows(p), :] = jnp.broadcast_to(m, (blk, HEAD_DIM))
            p_scr[p, :, cols] = jnp.exp2(s_scr[p, :, cols] - m).astype(BF16)
        for p in range(n_blocks):
            kr, cols = key_rows(p)
            od = jnp.dot(p_scr[p, :, cols], vp[g, kr, :], preferred_element_type=F32)
            o_scr[g, nat_rows(p), :] = od[:, 0:HEAD_DIM]
            d_scr[g, nat_rows(p), :] = od[:, HEAD_DIM:2 * HEAD_DIM]

    run_group(0, b1_ref, DILATED_CONFIGS[0][1])
    run_group(1, b2_ref, DILATED_CONFIGS[1][1])
    run_group(2, b3_ref, DILATED_CONFIGS[2][1])

    chunk = 256

    last_dil = DILATED_CONFIGS[2][1]

    def widest(scr, i):
        per = chunk // last_dil
        return jnp.concatenate(
            [scr[2, pl.ds(i * per + j, last_dil, stride=ATTN_PITCH), :] for j in range(per)], axis=0)

    def merge(i, carry):
        r = pl.ds(pl.multiple_of(i * chunk, chunk), chunk)
        m0, m1, m2 = m_scr[0, r, :], m_scr[1, r, :], widest(m_scr, i)
        mx = jnp.maximum(jnp.maximum(m0, m1), m2)
        w0, w1, w2 = jnp.exp2(m0 - mx), jnp.exp2(m1 - mx), jnp.exp2(m2 - mx)
        num = w0 * o_scr[0, r, :] + w1 * o_scr[1, r, :] + w2 * widest(o_scr, i)
        z = w0 * d_scr[0, r, :] + w1 * d_scr[1, r, :] + w2 * widest(d_scr, i)
        o_ref[r, :] = (num / z).astype(o_ref.dtype)
        return carry

    lax.fori_loop(0, S // chunk, merge, 0)


def _dilated_attention(qkv3, bias_tab):
    B, _, S, _ = qkv3.shape
    head_spec = lambda col0: pl.BlockSpec((None, None, S, HEAD_DIM), lambda b, h: (b, col0 + h, 0, 0))
    bias_spec = lambda g: pl.BlockSpec(
        (None, ATTN_BLOCK, 2 * ATTN_BLOCK), lambda b, h: (g * KV_HEADS + h, 0, 0))
    return pl.pallas_call(
        _dilated_kernel,
        grid=(B, KV_HEADS),
        in_specs=[
            head_spec(0), head_spec(KV_HEADS), head_spec(2 * KV_HEADS),
            head_spec(Q_HEADS), head_spec(Q_HEADS + KV_HEADS),
            bias_spec(0), bias_spec(1), bias_spec(2),
        ],
        out_specs=pl.BlockSpec((None, S, HEAD_DIM), lambda b, h: (b, 0, h)),
        out_shape=jax.ShapeDtypeStruct((B, S, KV_HEADS * HEAD_DIM), BF16),
        scratch_shapes=[
            pltpu.VMEM((N_GROUPS, S, HEAD_DIM), BF16),
            pltpu.VMEM((N_GROUPS, S, HEAD_DIM), BF16),
            pltpu.VMEM((N_GROUPS, S, 2 * HEAD_DIM), BF16),
            pltpu.VMEM((S, HEAD_DIM), F32),
            pltpu.VMEM((S, HEAD_DIM), F32),
            pltpu.VMEM((S, HEAD_DIM), F32),
            pltpu.VMEM((S // ATTN_BLOCK, ATTN_BLOCK, 2 * ATTN_BLOCK), F32),
            pltpu.VMEM((S // ATTN_BLOCK, ATTN_BLOCK, 2 * ATTN_BLOCK), BF16),
            pltpu.VMEM((N_GROUPS, S // ATTN_BLOCK * ATTN_PITCH, HEAD_DIM), F32),
            pltpu.VMEM((N_GROUPS, S // ATTN_BLOCK * ATTN_PITCH, HEAD_DIM), F32),
            pltpu.VMEM((N_GROUPS, S // ATTN_BLOCK * ATTN_PITCH, HEAD_DIM), F32),
        ],
        compiler_params=pltpu.CompilerParams(
            dimension_semantics=("parallel", "parallel"), vmem_limit_bytes=VMEM_LIMIT_BYTES),
        name="dilated_attn",
    )(qkv3, qkv3, qkv3, qkv3, qkv3, bias_tab, bias_tab, bias_tab)


def _gelu_tanh(x):
    return 0.5 * x * (1.0 + jnp.tanh(math.sqrt(2.0 / math.pi) * (x + 0.044715 * (x * x * x))))


def _mix_out_kernel(y_ref, o_ref, mg_ref, x_ref, wbr_ref, wba_ref, wout_ref, gmix_ref,
                    gpre_ref, wgu_ref, cw_ref, cb_ref, wd_ref, gpost_ref, out_ref,
                    mrg_scr, mix_scr, h_scr, hn_scr, gate_scr, up_scr, act_scr, ff_scr, *, tiles_per_seq):
    i = pl.program_id(0)
    sr, n_sub = FF_SR, FF_TM // FF_SR
    n_c = FFN_WIDTH // FF_TC

    @pl.when(i % tiles_per_seq == 0)
    def _():
        gate_scr[0, 0:SUBLANES, :] = jnp.zeros((SUBLANES, FFN_WIDTH), F32)

    def rows(r):
        return slice(r * sr, (r + 1) * sr)

    def stage_a(r, c):
        cols = slice(c * FF_DN, (c + 1) * FF_DN)
        br = jnp.dot(y_ref[rows(r), :], wbr_ref[:, cols], preferred_element_type=F32)
        ba = jnp.dot(o_ref[rows(r), :], wba_ref[:, cols], preferred_element_type=F32)
        g_rnn = _sigmoid(mg_ref[rows(r), c * FF_DN:(c + 1) * FF_DN].astype(F32))
        g_att = _sigmoid(mg_ref[rows(r), D_MODEL + c * FF_DN:D_MODEL + (c + 1) * FF_DN].astype(F32))
        mrg_scr[r % 2, :, cols] = (g_rnn * br + g_att * ba).astype(BF16)

    def stage_c(r, c):
        cols = slice(c * FF_DN, (c + 1) * FF_DN)
        mix_scr[r % 2, :, cols] = jnp.dot(mrg_scr[r % 2], wout_ref[:, cols], preferred_element_type=F32)

    def stage_n(r):
        h = x_ref[rows(r), :] + _rms_norm(mix_scr[r % 2], gmix_ref[...])
        h_scr[r] = h
        hn_scr[r % 2] = _rms_norm(h, gpre_ref[...]).astype(BF16)

    def stage_g(r, c):
        cols = slice(c * FF_TC, (c + 1) * FF_TC)
        res = jnp.dot(hn_scr[r % 2], wgu_ref[:, 2 * c * FF_TC:2 * (c + 1) * FF_TC], preferred_element_type=F32)
        gate = res[:, 0:FF_TC]
        gate_scr[r % 2, SUBLANES:SUBLANES + sr, cols] = gate
        gate_scr[(r + 1) % 2, 0:SUBLANES, cols] = gate[sr - SUBLANES:sr, :]
        up_scr[r % 2, :, cols] = res[:, FF_TC:2 * FF_TC]

    def stage_v(r, c):
        cols = slice(c * FF_TC, (c + 1) * FF_TC)
        conv = cb_ref[:, cols] + cw_ref[FFN_CONV - 1:FFN_CONV, cols] * gate_scr[r % 2, SUBLANES:SUBLANES + sr, cols]
        for k in range(FFN_CONV - 1):
            back = FFN_CONV - 1 - k
            conv = conv + cw_ref[k:k + 1, cols] * gate_scr[r % 2, SUBLANES - back:SUBLANES - back + sr, cols]
        act_scr[r % 2, :, cols] = (_gelu_tanh(conv) * up_scr[r % 2, :, cols]).astype(BF16)

    def stage_d(r, c):
        cols = slice(c * FF_DN, (c + 1) * FF_DN)
        ff_scr[r % 2, :, cols] = jnp.dot(act_scr[r % 2], wd_ref[:, cols], preferred_element_type=F32)

    def stage_e(r):
        out_ref[rows(r), :] = h_scr[r] + _rms_norm(ff_scr[r % 2], gpost_ref[...])

    n_d = D_MODEL // FF_DN
    live = lambda r: 0 <= r < n_sub
    for t in range(n_sub + 6):
        for c in range(n_d):
            if live(t):
                stage_a(t, c)
            if live(t - 1):
                stage_c(t - 1, c)
        if live(t - 2):
            stage_n(t - 2)
        for c in range(n_c):
            if live(t - 4):
                stage_v(t - 4, c)
            if live(t - 3):
                stage_g(t - 3, c)
            if live(t - 5):
                for d in range(c * n_d // n_c, (c + 1) * n_d // n_c):
                    stage_d(t - 5, d)
        if live(t - 6):
            stage_e(t - 6)


def _mix_out(y2, o2, mgates, x2, wbr, wba, wout, g_mix_post, g_pre, wgu, conv_w, conv_b, wd, g_post, seq_len):
    T = x2.shape[0]
    const = lambda i: (0, 0)
    row = lambda width: pl.BlockSpec((FF_TM, width), lambda i: (i, 0))
    resident = lambda shape: pl.BlockSpec(shape, const, pipeline_mode=pl.Buffered(1))
    return pl.pallas_call(
        functools.partial(_mix_out_kernel, tiles_per_seq=seq_len // FF_TM),
        grid=(T // FF_TM,),
        in_specs=[
            row(RNN_WIDTH), row(KV_HEADS * HEAD_DIM), row(GATE_WIDTH), row(D_MODEL),
            resident((RNN_WIDTH, D_MODEL)),
            resident((KV_HEADS * HEAD_DIM, D_MODEL)),
            resident((D_MODEL, D_MODEL)),
            pl.BlockSpec((1, D_MODEL), const),
            pl.BlockSpec((1, D_MODEL), const),
            resident((D_MODEL, 2 * FFN_WIDTH)),
            pl.BlockSpec((FFN_CONV, FFN_WIDTH), const),
            pl.BlockSpec((1, FFN_WIDTH), const),
            resident((FFN_WIDTH, D_MODEL)),
            pl.BlockSpec((1, D_MODEL), const),
        ],
        out_specs=row(D_MODEL),
        out_shape=jax.ShapeDtypeStruct((T, D_MODEL), F32),
        scratch_shapes=[
            pltpu.VMEM((2, FF_SR, D_MODEL), BF16),
            pltpu.VMEM((2, FF_SR, D_MODEL), F32),
            pltpu.VMEM((FF_TM // FF_SR, FF_SR, D_MODEL), F32),
            pltpu.VMEM((2, FF_SR, D_MODEL), BF16),
            pltpu.VMEM((2, SUBLANES + FF_SR, FFN_WIDTH), F32),
            pltpu.VMEM((2, FF_SR, FFN_WIDTH), F32),
            pltpu.VMEM((2, FF_SR, FFN_WIDTH), BF16),
            pltpu.VMEM((2, FF_SR, D_MODEL), F32),
        ],
        compiler_params=pltpu.CompilerParams(
            dimension_semantics=("arbitrary",), vmem_limit_bytes=VMEM_LIMIT_BYTES),
        name="mix_out",
    )(y2, o2, mgates, x2, wbr, wba, wout, g_mix_post, g_pre, wgu, conv_w, conv_b, wd, g_post)


def kernel(x, rel_bias, norm_mix_pre, norm_mix_post, w_in, conv_rnn_w, conv_rnn_b, w_rg_a, b_rg_a,
           w_rg_x, b_rg_x, lru_lambda, w_branch_rnn, w_branch_att, w_out, norm_ffn_pre, norm_ffn_post,
           w_ffn_gate, w_ffn_up, conv_ffn_w, conv_ffn_b, w_ffn_down):
    B, S, D = x.shape
    assert D == D_MODEL and S % FF_TM == 0 and S % RG_TS == 0 and B % RG_NB == 0
    assert (FF_TM // FF_SR) % 2 == 0, "the conv halo hand-off between grid steps relies on an even sub-tile count"
    assert S == DILATED_CONFIGS[-1][0], "attention block structure assumes window == sequence for the widest group"
    depth = w_in.shape[0]
    T = B * S
    bias_tab = _bias_tables(rel_bias)
    h = x.reshape(T, D)
    for l in range(depth):
        wax = jnp.concatenate([w_rg_a[l], w_rg_x[l]], axis=-1).astype(BF16)
        bax = jnp.stack([b_rg_a[l], b_rg_x[l]], axis=0)
        y_rnn, qkv, gates = _mix_in(h.reshape(B, S, D), norm_mix_pre[l][None], w_in[l].astype(BF16),
                                    conv_rnn_w[l], conv_rnn_b[l][None], wax, bax, lru_lambda[l][None])
        o_att = _dilated_attention(qkv, bias_tab)
        blocks = [w[:, c * FF_TC:(c + 1) * FF_TC] for c in range(FFN_WIDTH // FF_TC)
                  for w in (w_ffn_gate[l], w_ffn_up[l])]
        wgu = jnp.concatenate(blocks, axis=1).astype(BF16)
        h = _mix_out(y_rnn.reshape(T, RNN_WIDTH), o_att.reshape(T, KV_HEADS * HEAD_DIM),
                     gates.reshape(T, GATE_WIDTH), h,
                     w_branch_rnn[l].astype(BF16), w_branch_att[l].astype(BF16), w_out[l].astype(BF16),
                     norm_mix_post[l][None], norm_ffn_pre[l][None],
                     wgu, conv_ffn_w[l], conv_ffn_b[l][None],
                     w_ffn_down[l].astype(BF16), norm_ffn_post[l][None], S)
    return h.reshape(B, S, D)
```

```python
import functools
import math

import numpy as np
import jax
import jax.numpy as jnp
from jax import lax
from jax.experimental import pallas as pl
from jax.experimental.pallas import tpu as pltpu

F32 = jnp.float32
BF16 = jnp.bfloat16

D_MODEL = 1024
RNN_WIDTH = 1280
RNN_BLOCKS = 10
RNN_BLOCK = 128
RNN_CONV = 4
LRU_C = 8.0
HEAD_DIM = 128
KV_HEADS = 4
DILATED_CONFIGS = ((128, 1), (512, 4), (2048, 16))
N_GROUPS = 3
Q_HEADS = 12
ATTN_BLOCK = 128
ATTN_PITCH = ATTN_BLOCK + 8
REL_BUCKETS = 32
REL_MAX_DIST = 2048
FFN_WIDTH = 3072
FFN_CONV = 3
EPS = 1e-6
QKV_HEADS = Q_HEADS + 2 * KV_HEADS
QKV_WIDTH = QKV_HEADS * HEAD_DIM
GATE_WIDTH = 2 * D_MODEL
IN_WIDTH = RNN_WIDTH + QKV_WIDTH + GATE_WIDTH

SUBLANES = 8
VMEM_LIMIT_BYTES = 56 * 1024 * 1024

MASK_VALUE = -1e30
LOG2E = math.log2(math.e)
QK_SCALE_LOG2 = HEAD_DIM ** -0.5 * LOG2E

RG_TS = 64
RG_NB = 8
RG_PITCH = RG_TS + 8
RG_TC = 16
MIX_TN = 256
FF_TM = 512
FF_SR = 256
FF_TC = 256
FF_DN = 256


def _rms_norm(x, g):
    ms = jnp.mean(x * x, axis=-1, keepdims=True)
    return x * lax.rsqrt(ms + EPS) * g


def _sigmoid(x):
    return 0.5 + 0.5 * jnp.tanh(0.5 * x)


def _t5_bucket(dist):
    max_exact = REL_BUCKETS // 2
    d = np.maximum(dist, 1).astype(np.float32)
    large = max_exact + np.log(d / max_exact) / math.log(REL_MAX_DIST / max_exact) * (REL_BUCKETS - max_exact)
    large = np.minimum(large.astype(np.int32), REL_BUCKETS - 1)
    return np.where(dist < max_exact, dist, large).astype(np.int32)


def _bucket_maps():
    qi = np.arange(ATTN_BLOCK)[:, None]
    kj = np.arange(2 * ATTN_BLOCK)[None, :]
    delta = ATTN_BLOCK + qi - kj
    maps = []
    for window, dilation in DILATED_CONFIGS:
        n_back = window // dilation
        valid = (delta >= 0) & (delta <= n_back)
        bucket = _t5_bucket(np.maximum(delta, 0) * dilation)
        maps.append(np.where(valid, bucket, -1).astype(np.int32))
    return np.stack(maps, axis=0)


def _bias_kernel(rb_ref, bucket_ref, out_ref):
    head = pl.program_id(0)
    bk = bucket_ref[0]
    acc = jnp.full(bk.shape, MASK_VALUE, F32)
    for b in range(REL_BUCKETS):
        acc = jnp.where(bk == b, rb_ref[b, head] * LOG2E, acc)
    out_ref[0] = acc


def _bias_tables(rel_bias):
    buckets = jnp.asarray(_bucket_maps())
    return pl.pallas_call(
        _bias_kernel,
        grid=(Q_HEADS,),
        in_specs=[
            pl.BlockSpec(memory_space=pltpu.SMEM),
            pl.BlockSpec((1, ATTN_BLOCK, 2 * ATTN_BLOCK), lambda h: (h // KV_HEADS, 0, 0)),
        ],
        out_specs=pl.BlockSpec((1, ATTN_BLOCK, 2 * ATTN_BLOCK), lambda h: (h, 0, 0)),
        out_shape=jax.ShapeDtypeStruct((Q_HEADS, ATTN_BLOCK, 2 * ATTN_BLOCK), F32),
        name="bias_table",
    )(rel_bias, buckets)


def _rglru_chunk(n, t0, xr_scr, y_scr, carry, cw_ref, cb_ref, wax_ref, bax_ref, lam_ref):
    pitch, tc = RG_PITCH, RG_TC
    cols = slice(n * RNN_BLOCK, (n + 1) * RNN_BLOCK)
    bcast = lambda row: jnp.broadcast_to(row, (RG_NB, RNN_BLOCK))
    w = [bcast(cw_ref[k:k + 1, cols]) for k in range(RNN_CONV)]
    cb = bcast(cb_ref[:, cols])
    xs = [carry[n, k] for k in range(RNN_CONV - 1)]
    xs += [xr_scr[n, pl.ds(t0 + j, RG_NB, stride=pitch), :] for j in range(tc)]
    xc = jnp.concatenate(
        [cb + w[3] * xs[j + 3] + w[2] * xs[j + 2] + w[1] * xs[j + 1] + w[0] * xs[j] for j in range(tc)],
        axis=0)
    g = jnp.dot(xc.astype(BF16), wax_ref[n], preferred_element_type=F32)
    r = _sigmoid(g[:, 0:RNN_BLOCK] + bax_ref[0:1, cols])
    i = _sigmoid(g[:, RNN_BLOCK:2 * RNN_BLOCK] + bax_ref[1:2, cols])
    neg_lam = -lam_ref[:, cols]
    softplus = jnp.maximum(neg_lam, 0.0) + jnp.log1p(jnp.exp(-jnp.abs(neg_lam)))
    log_a = (-LRU_C * softplus) * r
    a = jnp.exp(log_a)
    th = jnp.tanh(log_a)
    u = jnp.sqrt(-2.0 * th / (1.0 - th)) * (i * xc)
    h = carry[n, RNN_CONV - 1]
    for j in range(tc):
        h = a[j * RG_NB:(j + 1) * RG_NB] * h + u[j * RG_NB:(j + 1) * RG_NB]
        y_scr[n, pl.ds(t0 + j, RG_NB, stride=pitch), :] = h
    for k in range(RNN_CONV - 1):
        carry[n, k] = xs[tc + k]
    carry[n, RNN_CONV - 1] = h


def _mix_in_kernel(x_ref, g_ref, w_ref, cw_ref, cb_ref, wax_ref, bax_ref, lam_ref,
                   y_ref, qkv_ref, gate_ref, xr_scr, y_scr, carry):
    s = pl.program_id(1)
    ts, pitch = RG_TS, RG_PITCH
    rows = RG_NB * ts

    @pl.when(s == 0)
    def _():
        carry[...] = jnp.zeros_like(carry)

    hn = _rms_norm(x_ref[...].reshape(rows, D_MODEL), g_ref[...]).astype(BF16)
    c0, c1 = RNN_WIDTH, RNN_WIDTH + QKV_WIDTH

    def project(pair):
        width = MIX_TN * len(pair)
        both = jnp.dot(hn, w_ref[:, pair[0]:pair[0] + width], preferred_element_type=F32)
        for col in pair:
            route(col, both[:, col - pair[0]:col - pair[0] + MIX_TN])

    def route(col, out):
        if col < c0:
            for n in range(col // RNN_BLOCK, (col + MIX_TN) // RNN_BLOCK):
                lanes = slice(n * RNN_BLOCK - col, (n + 1) * RNN_BLOCK - col)
                for b in range(RG_NB):
                    xr_scr[n, b * pitch:b * pitch + ts, :] = out[b * ts:(b + 1) * ts, lanes]
        elif col < c1:
            for hd in range((col - c0) // HEAD_DIM, (col - c0 + MIX_TN) // HEAD_DIM):
                lanes = slice(c0 + hd * HEAD_DIM - col, c0 + (hd + 1) * HEAD_DIM - col)
                qkv_ref[:, hd, :, :] = out[:, lanes].reshape(RG_NB, ts, HEAD_DIM)
        else:
            gate_ref[:, :, col - c1:col - c1 + MIX_TN] = out.astype(BF16).reshape(RG_NB, ts, MIX_TN)

    def recur(n, ci):
        _rglru_chunk(n, ci * RG_TC, xr_scr, y_scr, carry, cw_ref, cb_ref, wax_ref, bax_ref, lam_ref)
        if ci == ts // RG_TC - 1:
            for b in range(RG_NB):
                y_ref[b, :, n * RNN_BLOCK:(n + 1) * RNN_BLOCK] = (
                    y_scr[n, b * pitch:b * pitch + ts, :].astype(BF16))

    slabs_per_block = MIX_TN // RNN_BLOCK
    units = [(n, ci) for n0 in range(0, RNN_BLOCKS, slabs_per_block) for ci in range(ts // RG_TC)
             for n in range(n0, n0 + slabs_per_block)]
    cols = list(range(0, IN_WIDTH, MIX_TN))
    pairs = [cols[j:j + 2] for j in range(0, len(cols), 2)]
    project(pairs[0])
    done = 1
    for k, (n, ci) in enumerate(units):
        recur(n, ci)
        want = 1 + ((k + 1) * (len(pairs) - 1) + len(units) - 1) // len(units)
        while done < want:
            project(pairs[done])
            done += 1


def _mix_in(x3, g, w_bf, conv_w, conv_b, wax, bax, lam):
    B, S, _ = x3.shape
    const2 = lambda b, s: (0, 0)
    tile = lambda width: pl.BlockSpec((RG_NB, RG_TS, width), lambda b, s: (b, s, 0))
    scan_scratch = pltpu.VMEM((RNN_BLOCKS, RG_NB * RG_PITCH, RNN_BLOCK), F32)
    return pl.pallas_call(
        _mix_in_kernel,
        grid=(B // RG_NB, S // RG_TS),
        in_specs=[
            tile(D_MODEL),
            pl.BlockSpec((1, D_MODEL), const2),
            pl.BlockSpec((D_MODEL, IN_WIDTH), const2, pipeline_mode=pl.Buffered(1)),
            pl.BlockSpec((RNN_CONV, RNN_WIDTH), const2),
            pl.BlockSpec((1, RNN_WIDTH), const2),
            pl.BlockSpec((RNN_BLOCKS, RNN_BLOCK, 2 * RNN_BLOCK), lambda b, s: (0, 0, 0)),
            pl.BlockSpec((2, RNN_WIDTH), const2),
            pl.BlockSpec((1, RNN_WIDTH), const2),
        ],
        out_specs=[
            tile(RNN_WIDTH),
            pl.BlockSpec((RG_NB, QKV_HEADS, RG_TS, HEAD_DIM), lambda b, s: (b, 0, s, 0)),
            tile(GATE_WIDTH),
        ],
        out_shape=[
            jax.ShapeDtypeStruct((B, S, RNN_WIDTH), BF16),
            jax.ShapeDtypeStruct((B, QKV_HEADS, S, HEAD_DIM), F32),
            jax.ShapeDtypeStruct((B, S, GATE_WIDTH), BF16),
        ],
        scratch_shapes=[
            scan_scratch, scan_scratch,
            pltpu.VMEM((RNN_BLOCKS, RNN_CONV, RG_NB, RNN_BLOCK), F32),
        ],
        compiler_params=pltpu.CompilerParams(
            dimension_semantics=("parallel", "arbitrary"), vmem_limit_bytes=VMEM_LIMIT_BYTES),
        name="mix_in",
    )(x3, g, w_bf, conv_w, conv_b, wax, bax, lam)


def _dilated_kernel(q1_ref, q2_ref, q3_ref, k_ref, v_ref, b1_ref, b2_ref, b3_ref, o_ref,
                    qp, kp, vp, tq, tk, tv, s_scr, p_scr, o_scr, m_scr, d_scr):
    S = k_ref.shape[0]
    blk = ATTN_BLOCK
    n_blocks = S // blk
    quarter = S // 4
    nt = (((1,), (1,)), ((), ()))

    def put(g, r0, n, q, k, v):
        qp[g, r0:r0 + n, :] = (q * QK_SCALE_LOG2).astype(BF16)
        kp[g, r0:r0 + n, :] = k.astype(BF16)
        vp[g, r0:r0 + n, 0:HEAD_DIM] = v.astype(BF16)

    for g in range(N_GROUPS):
        vp[g, :, HEAD_DIM:2 * HEAD_DIM] = jnp.ones((S, HEAD_DIM), BF16)
    for c in range(4):
        r = slice(c * quarter, (c + 1) * quarter)
        put(0, c * quarter, quarter, q1_ref[r, :], k_ref[r, :], v_ref[r, :])
    for c in range(4):
        sr = pl.ds(c, quarter, stride=4)
        r = slice(c * quarter, (c + 1) * quarter)
        k4, v4 = k_ref[sr, :], v_ref[sr, :]
        put(1, c * quarter, quarter, q2_ref[sr, :], k4, v4)
        tk[r, :] = k4
        tv[r, :] = v4
        tq[r, :] = q3_ref[sr, :]
    for c4 in range(4):
        for j in range(4):
            sr = pl.ds(c4 * quarter + j, blk, stride=4)
            put(2, (4 * j + c4) * blk, blk, tq[sr, :], tk[sr, :], tv[sr, :])

    def run_group(g, b_ref, dil):
        nb = S // dil // blk

        def nat_rows(p):
            c, n = divmod(p, nb)
            if nb == 1:
                return slice(c * ATTN_PITCH, c * ATTN_PITCH + blk)
            start = n * (blk * dil) + c
            return pl.ds(start, blk) if dil == 1 else pl.ds(start, blk, stride=dil)

        def key_rows(p):
            has_prev = p % nb != 0
            return (slice((p - 1) * blk, (p + 1) * blk) if has_prev else slice(p * blk, (p + 1) * blk),
                    slice(0, 2 * blk) if has_prev else slice(blk, 2 * blk))

        for p in range(n_blocks):
            kr, cols = key_rows(p)
            s = lax.dot_general(qp[g, p * blk:(p + 1) * blk, :], kp[g, kr, :], nt, preferred_element_type=F32)
            s_scr[p, :, cols] = s + b_ref[:, cols]
        for p in range(n_blocks):
            _, cols = key_rows(p)
            m = jnp.max(s_scr[p, :, cols], axis=-1, keepdims=True)
            m_scr[g, nat_rows(p), :] = jnp.broadcast_to(m, (blk, HEAD_DIM))
            p_scr[p, :, cols] = jnp.exp2(s_scr[p, :, cols] - m).astype(BF16)
        for p in range(n_blocks):
            kr, cols = key_rows(p)
            od = jnp.dot(p_scr[p, :, cols], vp[g, kr, :], preferred_element_type=F32)
            o_scr[g, nat_rows(p), :] = od[:, 0:HEAD_DIM]
            d_scr[g, nat_rows(p), :] = od[:, HEAD_DIM:2 * HEAD_DIM]

    run_group(0, b1_ref, DILATED_CONFIGS[0][1])
    run_group(1, b2_ref, DILATED_CONFIGS[1][1])
    run_group(2, b3_ref, DILATED_CONFIGS[2][1])

    chunk = 256

    last_dil = DILATED_CONFIGS[2][1]

    def widest(scr, i):
        per = chunk // last_dil
        return jnp.concatenate(
            [scr[2, pl.ds(i * per + j, last_dil, stride=ATTN_PITCH), :] for j in range(per)], axis=0)

    def merge(i, carry):
        r = pl.ds(pl.multiple_of(i * chunk, chunk), chunk)
        m0, m1, m2 = m_scr[0, r, :], m_scr[1, r, :], widest(m_scr, i)
        mx = jnp.maximum(jnp.maximum(m0, m1), m2)
        w0, w1, w2 = jnp.exp2(m0 - mx), jnp.exp2(m1 - mx), jnp.exp2(m2 - mx)
        num = w0 * o_scr[0, r, :] + w1 * o_scr[1, r, :] + w2 * widest(o_scr, i)
        z = w0 * d_scr[0, r, :] + w1 * d_scr[1, r, :] + w2 * widest(d_scr, i)
        o_ref[r, :] = (num / z).astype(o_ref.dtype)
        return carry

    lax.fori_loop(0, S // chunk, merge, 0)


def _dilated_attention(qkv3, bias_tab):
    B, _, S, _ = qkv3.shape
    head_spec = lambda col0: pl.BlockSpec((None, None, S, HEAD_DIM), lambda b, h: (b, col0 + h, 0, 0))
    bias_spec = lambda g: pl.BlockSpec(
        (None, ATTN_BLOCK, 2 * ATTN_BLOCK), lambda b, h: (g * KV_HEADS + h, 0, 0))
    return pl.pallas_call(
        _dilated_kernel,
        grid=(B, KV_HEADS),
        in_specs=[
            head_spec(0), head_spec(KV_HEADS), head_spec(2 * KV_HEADS),
            head_spec(Q_HEADS), head_spec(Q_HEADS + KV_HEADS),
            bias_spec(0), bias_spec(1), bias_spec(2),
        ],
        out_specs=pl.BlockSpec((None, S, HEAD_DIM), lambda b, h: (b, 0, h)),
        out_shape=jax.ShapeDtypeStruct((B, S, KV_HEADS * HEAD_DIM), BF16),
        scratch_shapes=[
            pltpu.VMEM((N_GROUPS, S, HEAD_DIM), BF16),
            pltpu.VMEM((N_GROUPS, S, HEAD_DIM), BF16),
            pltpu.VMEM((N_GROUPS, S, 2 * HEAD_DIM), BF16),
            pltpu.VMEM((S, HEAD_DIM), F32),
            pltpu.VMEM((S, HEAD_DIM), F32),
            pltpu.VMEM((S, HEAD_DIM), F32),
            pltpu.VMEM((S // ATTN_BLOCK, ATTN_BLOCK, 2 * ATTN_BLOCK), F32),
            pltpu.VMEM((S // ATTN_BLOCK, ATTN_BLOCK, 2 * ATTN_BLOCK), BF16),
            pltpu.VMEM((N_GROUPS, S // ATTN_BLOCK * ATTN_PITCH, HEAD_DIM), F32),
            pltpu.VMEM((N_GROUPS, S // ATTN_BLOCK * ATTN_PITCH, HEAD_DIM), F32),
            pltpu.VMEM((N_GROUPS, S // ATTN_BLOCK * ATTN_PITCH, HEAD_DIM), F32),
        ],
        compiler_params=pltpu.CompilerParams(
            dimension_semantics=("parallel", "parallel"), vmem_limit_bytes=VMEM_LIMIT_BYTES),
        name="dilated_attn",
    )(qkv3, qkv3, qkv3, qkv3, qkv3, bias_tab, bias_tab, bias_tab)


def _gelu_tanh(x):
    return 0.5 * x * (1.0 + jnp.tanh(math.sqrt(2.0 / math.pi) * (x + 0.044715 * (x * x * x))))


def _mix_out_kernel(y_ref, o_ref, mg_ref, x_ref, wbr_ref, wba_ref, wout_ref, gmix_ref,
                    gpre_ref, wgu_ref, cw_ref, cb_ref, wd_ref, gpost_ref, out_ref,
                    mrg_scr, mix_scr, h_scr, hn_scr, gate_scr, up_scr, act_scr, ff_scr, *, tiles_per_seq):
    i = pl.program_id(0)
    sr, n_sub = FF_SR, FF_TM // FF_SR
    n_c = FFN_WIDTH // FF_TC

    @pl.when(i % tiles_per_seq == 0)
    def _():
        gate_scr[0, 0:SUBLANES, :] = jnp.zeros((SUBLANES, FFN_WIDTH), F32)

    def rows(r):
        return slice(r * sr, (r + 1) * sr)

    def stage_a(r, c):
        cols = slice(c * FF_DN, (c + 1) * FF_DN)
        br = jnp.dot(y_ref[rows(r), :], wbr_ref[:, cols], preferred_element_type=F32)
        ba = jnp.dot(o_ref[rows(r), :], wba_ref[:, cols], preferred_element_type=F32)
        g_rnn = _sigmoid(mg_ref[rows(r), c * FF_DN:(c + 1) * FF_DN].astype(F32))
        g_att = _sigmoid(mg_ref[rows(r), D_MODEL + c * FF_DN:D_MODEL + (c + 1) * FF_DN].astype(F32))
        mrg_scr[r % 2, :, cols] = (g_rnn * br + g_att * ba).astype(BF16)

    def stage_c(r, c):
        cols = slice(c * FF_DN, (c + 1) * FF_DN)
        mix_scr[r % 2, :, cols] = jnp.dot(mrg_scr[r % 2], wout_ref[:, cols], preferred_element_type=F32)

    def stage_n(r):
        h = x_ref[rows(r), :] + _rms_norm(mix_scr[r % 2], gmix_ref[...])
        h_scr[r] = h
        hn_scr[r % 2] = _rms_norm(h, gpre_ref[...]).astype(BF16)

    def stage_g(r, c):
        cols = slice(c * FF_TC, (c + 1) * FF_TC)
        res = jnp.dot(hn_scr[r % 2], wgu_ref[:, 2 * c * FF_TC:2 * (c + 1) * FF_TC], preferred_element_type=F32)
        gate = res[:, 0:FF_TC]
        gate_scr[r % 2, SUBLANES:SUBLANES + sr, cols] = gate
        gate_scr[(r + 1) % 2, 0:SUBLANES, cols] = gate[sr - SUBLANES:sr, :]
        up_scr[r % 2, :, cols] = res[:, FF_TC:2 * FF_TC]

    def stage_v(r, c):
        cols = slice(c * FF_TC, (c + 1) * FF_TC)
        conv = cb_ref[:, cols] + cw_ref[FFN_CONV - 1:FFN_CONV, cols] * gate_scr[r % 2, SUBLANES:SUBLANES + sr, cols]
        for k in range(FFN_CONV - 1):
            back = FFN_CONV - 1 - k
            conv = conv + cw_ref[k:k + 1, cols] * gate_scr[r % 2, SUBLANES - back:SUBLANES - back + sr, cols]
        act_scr[r % 2, :, cols] = (_gelu_tanh(conv) * up_scr[r % 2, :, cols]).astype(BF16)

    def stage_d(r, c):
        cols = slice(c * FF_DN, (c + 1) * FF_DN)
        ff_scr[r % 2, :, cols] = jnp.dot(act_scr[r % 2], wd_ref[:, cols], preferred_element_type=F32)

    def stage_e(r):
        out_ref[rows(r), :] = h_scr[r] + _rms_norm(ff_scr[r % 2], gpost_ref[...])

    n_d = D_MODEL // FF_DN
    live = lambda r: 0 <= r < n_sub
    for t in range(n_sub + 6):
        for c in range(n_d):
            if live(t):
                stage_a(t, c)
            if live(t - 1):
                stage_c(t - 1, c)
        if live(t - 2):
            stage_n(t - 2)
        for c in range(n_c):
            if live(t - 4):
                stage_v(t - 4, c)
            if live(t - 3):
                stage_g(t - 3, c)
            if live(t - 5):
                for d in range(c * n_d // n_c, (c + 1) * n_d // n_c):
                    stage_d(t - 5, d)
        if live(t - 6):
            stage_e(t - 6)


def _pack_gate_up_kernel(wg_ref, wu_ref, out_ref):
    out_ref[:, 0:FF_TC] = wg_ref[...].astype(BF16)
    out_ref[:, FF_TC:2 * FF_TC] = wu_ref[...].astype(BF16)


def _pack_gate_up(wg, wu):
    block = pl.BlockSpec((D_MODEL, FF_TC), lambda c: (0, c))
    return pl.pallas_call(
        _pack_gate_up_kernel,
        grid=(FFN_WIDTH // FF_TC,),
        in_specs=[block, block],
        out_specs=pl.BlockSpec((D_MODEL, 2 * FF_TC), lambda c: (0, c)),
        out_shape=jax.ShapeDtypeStruct((D_MODEL, 2 * FFN_WIDTH), BF16),
        name="pack_gate_up",
    )(wg, wu)


def _mix_out(y2, o2, mgates, x2, wbr, wba, wout, g_mix_post, g_pre, wgu, conv_w, conv_b, wd, g_post, seq_len):
    T = x2.shape[0]
    const = lambda i: (0, 0)
    row = lambda width: pl.BlockSpec((FF_TM, width), lambda i: (i, 0))
    resident = lambda shape: pl.BlockSpec(shape, const, pipeline_mode=pl.Buffered(1))
    return pl.pallas_call(
        functools.partial(_mix_out_kernel, tiles_per_seq=seq_len // FF_TM),
        grid=(T // FF_TM,),
        in_specs=[
            row(RNN_WIDTH), row(KV_HEADS * HEAD_DIM), row(GATE_WIDTH), row(D_MODEL),
            resident((RNN_WIDTH, D_MODEL)),
            resident((KV_HEADS * HEAD_DIM, D_MODEL)),
            resident((D_MODEL, D_MODEL)),
            pl.BlockSpec((1, D_MODEL), const),
            pl.BlockSpec((1, D_MODEL), const),
            resident((D_MODEL, 2 * FFN_WIDTH)),
            pl.BlockSpec((FFN_CONV, FFN_WIDTH), const),
            pl.BlockSpec((1, FFN_WIDTH), const),
            resident((FFN_WIDTH, D_MODEL)),
            pl.BlockSpec((1, D_MODEL), const),
        ],
        out_specs=row(D_MODEL),
        out_shape=jax.ShapeDtypeStruct((T, D_MODEL), F32),
        scratch_shapes=[
            pltpu.VMEM((2, FF_SR, D_MODEL), BF16),
            pltpu.VMEM((2, FF_SR, D_MODEL), F32),
            pltpu.VMEM((FF_TM // FF_SR, FF_SR, D_MODEL), F32),
            pltpu.VMEM((2, FF_SR, D_MODEL), BF16),
            pltpu.VMEM((2, SUBLANES + FF_SR, FFN_WIDTH), F32),
            pltpu.VMEM((2, FF_SR, FFN_WIDTH), F32),
            pltpu.VMEM((2, FF_SR, FFN_WIDTH), BF16),
            pltpu.VMEM((2, FF_SR, D_MODEL), F32),
        ],
        compiler_params=pltpu.CompilerParams(
            dimension_semantics=("arbitrary",), vmem_limit_bytes=VMEM_LIMIT_BYTES),
        name="mix_out",
    )(y2, o2, mgates, x2, wbr, wba, wout, g_mix_post, g_pre, wgu, conv_w, conv_b, wd, g_post)


def kernel(x, rel_bias, norm_mix_pre, norm_mix_post, w_in, conv_rnn_w, conv_rnn_b, w_rg_a, b_rg_a,
           w_rg_x, b_rg_x, lru_lambda, w_branch_rnn, w_branch_att, w_out, norm_ffn_pre, norm_ffn_post,
           w_ffn_gate, w_ffn_up, conv_ffn_w, conv_ffn_b, w_ffn_down):
    B, S, D = x.shape
    assert D == D_MODEL and S % FF_TM == 0 and S % RG_TS == 0 and B % RG_NB == 0
    assert (FF_TM // FF_SR) % 2 == 0, "the conv halo hand-off between grid steps relies on an even sub-tile count"
    assert S == DILATED_CONFIGS[-1][0], "attention block structure assumes window == sequence for the widest group"
    depth = w_in.shape[0]
    T = B * S
    bias_tab = _bias_tables(rel_bias)
    h = x.reshape(T, D)
    for l in range(depth):
        wax = jnp.concatenate([w_rg_a[l], w_rg_x[l]], axis=-1).astype(BF16)
        bax = jnp.stack([b_rg_a[l], b_rg_x[l]], axis=0)
        y_rnn, qkv, gates = _mix_in(h.reshape(B, S, D), norm_mix_pre[l][None], w_in[l].astype(BF16),
                                    conv_rnn_w[l], conv_rnn_b[l][None], wax, bax, lru_lambda[l][None])
        o_att = _dilated_attention(qkv, bias_tab)
        wgu = _pack_gate_up(w_ffn_gate[l], w_ffn_up[l])
        h = _mix_out(y_rnn.reshape(T, RNN_WIDTH), o_att.reshape(T, KV_HEADS * HEAD_DIM),
                     gates.reshape(T, GATE_WIDTH), h,
                     w_branch_rnn[l].astype(BF16), w_branch_att[l].astype(BF16), w_out[l].astype(BF16),
                     norm_mix_post[l][None], norm_ffn_pre[l][None],
                     wgu, conv_ffn_w[l], conv_ffn_b[l][None],
                     w_ffn_down[l].astype(BF16), norm_ffn_post[l][None], S)
    return h.reshape(B, S, D)
```

```python
import functools
import math

import numpy as np
import jax
import jax.numpy as jnp
from jax import lax
from jax.experimental import pallas as pl
from jax.experimental.pallas import tpu as pltpu

F32 = jnp.float32
BF16 = jnp.bfloat16

D_MODEL = 1024
RNN_WIDTH = 1280
RNN_BLOCKS = 10
RNN_BLOCK = 128
RNN_CONV = 4
LRU_C = 8.0
HEAD_DIM = 128
KV_HEADS = 4
DILATED_CONFIGS = ((128, 1), (512, 4), (2048, 16))
N_GROUPS = 3
Q_HEADS = 12
ATTN_BLOCK = 128
ATTN_PITCH = ATTN_BLOCK + 8
REL_BUCKETS = 32
REL_MAX_DIST = 2048
FFN_WIDTH = 3072
FFN_CONV = 3
EPS = 1e-6
QKV_HEADS = Q_HEADS + 2 * KV_HEADS
QKV_WIDTH = QKV_HEADS * HEAD_DIM
GATE_WIDTH = 2 * D_MODEL
IN_WIDTH = RNN_WIDTH + QKV_WIDTH + GATE_WIDTH

SUBLANES = 8
VMEM_LIMIT_BYTES = 56 * 1024 * 1024

MASK_VALUE = -1e30
LOG2E = math.log2(math.e)
QK_SCALE_LOG2 = HEAD_DIM ** -0.5 * LOG2E

RG_TS = 64
RG_NB = 8
RG_PITCH = RG_TS + 8
RG_TC = 16
MIX_TN = 256
FF_TM = 512
FF_SR = 256
FF_TC = 256
FF_DN = 256


def _rms_norm(x, g):
    ms = jnp.mean(x * x, axis=-1, keepdims=True)
    return x * lax.rsqrt(ms + EPS) * g


def _sigmoid(x):
    return 0.5 + 0.5 * jnp.tanh(0.5 * x)


def _t5_bucket(dist):
    max_exact = REL_BUCKETS // 2
    d = np.maximum(dist, 1).astype(np.float32)
    large = max_exact + np.log(d / max_exact) / math.log(REL_MAX_DIST / max_exact) * (REL_BUCKETS - max_exact)
    large = np.minimum(large.astype(np.int32), REL_BUCKETS - 1)
    return np.where(dist < max_exact, dist, large).astype(np.int32)


def _bucket_maps():
    qi = np.arange(ATTN_BLOCK)[:, None]
    kj = np.arange(2 * ATTN_BLOCK)[None, :]
    delta = ATTN_BLOCK + qi - kj
    maps = []
    for window, dilation in DILATED_CONFIGS:
        n_back = window // dilation
        valid = (delta >= 0) & (delta <= n_back)
        bucket = _t5_bucket(np.maximum(delta, 0) * dilation)
        maps.append(np.where(valid, bucket, -1).astype(np.int32))
    return np.stack(maps, axis=0)


def _bias_kernel(rb_ref, bucket_ref, out_ref):
    head = pl.program_id(0)
    bk = bucket_ref[0]
    acc = jnp.full(bk.shape, MASK_VALUE, F32)
    for b in range(REL_BUCKETS):
        acc = jnp.where(bk == b, rb_ref[b, head] * LOG2E, acc)
    out_ref[0] = acc


def _bias_tables(rel_bias):
    buckets = jnp.asarray(_bucket_maps())
    return pl.pallas_call(
        _bias_kernel,
        grid=(Q_HEADS,),
        in_specs=[
            pl.BlockSpec(memory_space=pltpu.SMEM),
            pl.BlockSpec((1, ATTN_BLOCK, 2 * ATTN_BLOCK), lambda h: (h // KV_HEADS, 0, 0)),
        ],
        out_specs=pl.BlockSpec((1, ATTN_BLOCK, 2 * ATTN_BLOCK), lambda h: (h, 0, 0)),
        out_shape=jax.ShapeDtypeStruct((Q_HEADS, ATTN_BLOCK, 2 * ATTN_BLOCK), F32),
        name="bias_table",
    )(rel_bias, buckets)


def _rglru_chunk(n, t0, xr_scr, y_scr, carry, cw_ref, cb_ref, wax_ref, bax_ref, lam_ref):
    pitch, tc = RG_PITCH, RG_TC
    cols = slice(n * RNN_BLOCK, (n + 1) * RNN_BLOCK)
    bcast = lambda row: jnp.broadcast_to(row, (RG_NB, RNN_BLOCK))
    w = [bcast(cw_ref[k:k + 1, cols]) for k in range(RNN_CONV)]
    cb = bcast(cb_ref[:, cols])
    xs = [carry[n, k] for k in range(RNN_CONV - 1)]
    xs += [xr_scr[n, pl.ds(t0 + j, RG_NB, stride=pitch), :] for j in range(tc)]
    xc = jnp.concatenate(
        [cb + w[3] * xs[j + 3] + w[2] * xs[j + 2] + w[1] * xs[j + 1] + w[0] * xs[j] for j in range(tc)],
        axis=0)
    g = jnp.dot(xc.astype(BF16), wax_ref[n], preferred_element_type=F32)
    r = _sigmoid(g[:, 0:RNN_BLOCK] + bax_ref[0:1, cols])
    i = _sigmoid(g[:, RNN_BLOCK:2 * RNN_BLOCK] + bax_ref[1:2, cols])
    neg_lam = -lam_ref[:, cols]
    softplus = jnp.maximum(neg_lam, 0.0) + jnp.log1p(jnp.exp(-jnp.abs(neg_lam)))
    log_a = (-LRU_C * softplus) * r
    a = jnp.exp(log_a)
    th = jnp.tanh(log_a)
    u = jnp.sqrt(-2.0 * th / (1.0 - th)) * (i * xc)
    h = carry[n, RNN_CONV - 1]
    for j in range(tc):
        h = a[j * RG_NB:(j + 1) * RG_NB] * h + u[j * RG_NB:(j + 1) * RG_NB]
        y_scr[n, pl.ds(t0 + j, RG_NB, stride=pitch), :] = h
    for k in range(RNN_CONV - 1):
        carry[n, k] = xs[tc + k]
    carry[n, RNN_CONV - 1] = h


def _mix_in_kernel(x_ref, g_ref, w_ref, cw_ref, cb_ref, wax_ref, bax_ref, lam_ref,
                   y_ref, qkv_ref, gate_ref, xr_scr, y_scr, carry):
    s = pl.program_id(1)
    ts, pitch = RG_TS, RG_PITCH
    rows = RG_NB * ts

    @pl.when(s == 0)
    def _():
        carry[...] = jnp.zeros_like(carry)

    hn = _rms_norm(x_ref[...].reshape(rows, D_MODEL), g_ref[...]).astype(BF16)
    c0, c1 = RNN_WIDTH, RNN_WIDTH + QKV_WIDTH

    def project(pair):
        width = MIX_TN * len(pair)
        both = jnp.dot(hn, w_ref[:, pair[0]:pair[0] + width], preferred_element_type=F32)
        for col in pair:
            route(col, both[:, col - pair[0]:col - pair[0] + MIX_TN])

    def route(col, out):
        if col < c0:
            for n in range(col // RNN_BLOCK, (col + MIX_TN) // RNN_BLOCK):
                lanes = slice(n * RNN_BLOCK - col, (n + 1) * RNN_BLOCK - col)
                for b in range(RG_NB):
                    xr_scr[n, b * pitch:b * pitch + ts, :] = out[b * ts:(b + 1) * ts, lanes]
        elif col < c1:
            for hd in range((col - c0) // HEAD_DIM, (col - c0 + MIX_TN) // HEAD_DIM):
                lanes = slice(c0 + hd * HEAD_DIM - col, c0 + (hd + 1) * HEAD_DIM - col)
                qkv_ref[:, hd, :, :] = out[:, lanes].reshape(RG_NB, ts, HEAD_DIM)
        else:
            gate_ref[:, :, col - c1:col - c1 + MIX_TN] = out.astype(BF16).reshape(RG_NB, ts, MIX_TN)

    def recur(n, ci):
        _rglru_chunk(n, ci * RG_TC, xr_scr, y_scr, carry, cw_ref, cb_ref, wax_ref, bax_ref, lam_ref)
        if ci == ts // RG_TC - 1:
            for b in range(RG_NB):
                y_ref[b, :, n * RNN_BLOCK:(n + 1) * RNN_BLOCK] = (
                    y_scr[n, b * pitch:b * pitch + ts, :].astype(BF16))

    slabs_per_block = MIX_TN // RNN_BLOCK
    units = [(n, ci) for n0 in range(0, RNN_BLOCKS, slabs_per_block) for ci in range(ts // RG_TC)
             for n in range(n0, n0 + slabs_per_block)]
    cols = list(range(0, IN_WIDTH, MIX_TN))
    pairs = [cols[j:j + 2] for j in range(0, len(cols), 2)]
    project(pairs[0])
    done = 1
    for k, (n, ci) in enumerate(units):
        recur(n, ci)
        want = 1 + ((k + 1) * (len(pairs) - 1) + len(units) - 1) // len(units)
        while done < want:
            project(pairs[done])
            done += 1


def _mix_in(x3, g, w_bf, conv_w, conv_b, wax, bax, lam):
    B, S, _ = x3.shape
    const2 = lambda b, s: (0, 0)
    tile = lambda width: pl.BlockSpec((RG_NB, RG_TS, width), lambda b, s: (b, s, 0))
    scan_scratch = pltpu.VMEM((RNN_BLOCKS, RG_NB * RG_PITCH, RNN_BLOCK), F32)
    return pl.pallas_call(
        _mix_in_kernel,
        grid=(B // RG_NB, S // RG_TS),
        in_specs=[
            tile(D_MODEL),
            pl.BlockSpec((1, D_MODEL), const2),
            pl.BlockSpec((D_MODEL, IN_WIDTH), const2, pipeline_mode=pl.Buffered(1)),
            pl.BlockSpec((RNN_CONV, RNN_WIDTH), const2),
            pl.BlockSpec((1, RNN_WIDTH), const2),
            pl.BlockSpec((RNN_BLOCKS, RNN_BLOCK, 2 * RNN_BLOCK), lambda b, s: (0, 0, 0)),
            pl.BlockSpec((2, RNN_WIDTH), const2),
            pl.BlockSpec((1, RNN_WIDTH), const2),
        ],
        out_specs=[
            tile(RNN_WIDTH),
            pl.BlockSpec((RG_NB, QKV_HEADS, RG_TS, HEAD_DIM), lambda b, s: (b, 0, s, 0)),
            tile(GATE_WIDTH),
        ],
        out_shape=[
            jax.ShapeDtypeStruct((B, S, RNN_WIDTH), BF16),
            jax.ShapeDtypeStruct((B, QKV_HEADS, S, HEAD_DIM), F32),
            jax.ShapeDtypeStruct((B, S, GATE_WIDTH), BF16),
        ],
        scratch_shapes=[
            scan_scratch, scan_scratch,
            pltpu.VMEM((RNN_BLOCKS, RNN_CONV, RG_NB, RNN_BLOCK), F32),
        ],
        compiler_params=pltpu.CompilerParams(
            dimension_semantics=("parallel", "arbitrary"), vmem_limit_bytes=VMEM_LIMIT_BYTES),
        name="mix_in",
    )(x3, g, w_bf, conv_w, conv_b, wax, bax, lam)


def _dilated_kernel(q1_ref, q2_ref, q3_ref, k_ref, v_ref, b1_ref, b2_ref, b3_ref, o_ref,
                    qp, kp, vp, tq, tk, tv, s_scr, p_scr, o_scr, m_scr, d_scr):
    S = k_ref.shape[0]
    blk = ATTN_BLOCK
    n_blocks = S // blk
    quarter = S // 4

    def put(g, r0, n, q, k, v):
        qp[g, r0:r0 + n, :] = (q * QK_SCALE_LOG2).astype(BF16)
        kp[g, :, r0:r0 + n] = k.T.astype(BF16)
        vp[g, r0:r0 + n, 0:HEAD_DIM] = v.astype(BF16)

    for g in range(N_GROUPS):
        vp[g, :, HEAD_DIM:2 * HEAD_DIM] = jnp.ones((S, HEAD_DIM), BF16)
    for c in range(4):
        r = slice(c * quarter, (c + 1) * quarter)
        put(0, c * quarter, quarter, q1_ref[r, :], k_ref[r, :], v_ref[r, :])
    for c in range(4):
        sr = pl.ds(c, quarter, stride=4)
        r = slice(c * quarter, (c + 1) * quarter)
        k4, v4 = k_ref[sr, :], v_ref[sr, :]
        put(1, c * quarter, quarter, q2_ref[sr, :], k4, v4)
        tk[r, :] = k4
        tv[r, :] = v4
        tq[r, :] = q3_ref[sr, :]
    for c4 in range(4):
        for j in range(4):
            sr = pl.ds(c4 * quarter + j, blk, stride=4)
            put(2, (4 * j + c4) * blk, blk, tq[sr, :], tk[sr, :], tv[sr, :])

    def run_group(g, b_ref, dil):
        nb = S // dil // blk

        def nat_rows(p):
            c, n = divmod(p, nb)
            if nb == 1:
                return slice(c * ATTN_PITCH, c * ATTN_PITCH + blk)
            start = n * (blk * dil) + c
            return pl.ds(start, blk) if dil == 1 else pl.ds(start, blk, stride=dil)

        def key_rows(p):
            has_prev = p % nb != 0
            return (slice((p - 1) * blk, (p + 1) * blk) if has_prev else slice(p * blk, (p + 1) * blk),
                    slice(0, 2 * blk) if has_prev else slice(blk, 2 * blk))

        for p in range(n_blocks):
            kr, cols = key_rows(p)
            s = jnp.dot(qp[g, p * blk:(p + 1) * blk, :], kp[g, :, kr], preferred_element_type=F32)
            s_scr[p, :, cols] = s + b_ref[:, cols]
        for p in range(n_blocks):
            _, cols = key_rows(p)
            m = jnp.max(s_scr[p, :, cols], axis=-1, keepdims=True)
            m_scr[g, nat_rows(p), :] = jnp.broadcast_to(m, (blk, HEAD_DIM))
            p_scr[p, :, cols] = jnp.exp2(s_scr[p, :, cols] - m).astype(BF16)
        for p in range(n_blocks):
            kr, cols = key_rows(p)
            od = jnp.dot(p_scr[p, :, cols], vp[g, kr, :], preferred_element_type=F32)
            o_scr[g, nat_rows(p), :] = od[:, 0:HEAD_DIM]
            d_scr[g, nat_rows(p), :] = od[:, HEAD_DIM:2 * HEAD_DIM]

    run_group(0, b1_ref, DILATED_CONFIGS[0][1])
    run_group(1, b2_ref, DILATED_CONFIGS[1][1])
    run_group(2, b3_ref, DILATED_CONFIGS[2][1])

    chunk = 256

    last_dil = DILATED_CONFIGS[2][1]

    def widest(scr, i):
        per = chunk // last_dil
        return jnp.concatenate(
            [scr[2, pl.ds(i * per + j, last_dil, stride=ATTN_PITCH), :] for j in range(per)], axis=0)

    def merge(i, carry):
        r = pl.ds(pl.multiple_of(i * chunk, chunk), chunk)
        m0, m1, m2 = m_scr[0, r, :], m_scr[1, r, :], widest(m_scr, i)
        mx = jnp.maximum(jnp.maximum(m0, m1), m2)
        w0, w1, w2 = jnp.exp2(m0 - mx), jnp.exp2(m1 - mx), jnp.exp2(m2 - mx)
        num = w0 * o_scr[0, r, :] + w1 * o_scr[1, r, :] + w2 * widest(o_scr, i)
        z = w0 * d_scr[0, r, :] + w1 * d_scr[1, r, :] + w2 * widest(d_scr, i)
        o_ref[r, :] = (num / z).astype(o_ref.dtype)
        return carry

    lax.fori_loop(0, S // chunk, merge, 0)


def _dilated_attention(qkv3, bias_tab):
    B, _, S, _ = qkv3.shape
    head_spec = lambda col0: pl.BlockSpec((None, None, S, HEAD_DIM), lambda b, h: (b, col0 + h, 0, 0))
    bias_spec = lambda g: pl.BlockSpec(
        (None, ATTN_BLOCK, 2 * ATTN_BLOCK), lambda b, h: (g * KV_HEADS + h, 0, 0))
    return pl.pallas_call(
        _dilated_kernel,
        grid=(B, KV_HEADS),
        in_specs=[
            head_spec(0), head_spec(KV_HEADS), head_spec(2 * KV_HEADS),
            head_spec(Q_HEADS), head_spec(Q_HEADS + KV_HEADS),
            bias_spec(0), bias_spec(1), bias_spec(2),
        ],
        out_specs=pl.BlockSpec((None, S, HEAD_DIM), lambda b, h: (b, 0, h)),
        out_shape=jax.ShapeDtypeStruct((B, S, KV_HEADS * HEAD_DIM), BF16),
        scratch_shapes=[
            pltpu.VMEM((N_GROUPS, S, HEAD_DIM), BF16),
            pltpu.VMEM((N_GROUPS, HEAD_DIM, S), BF16),
            pltpu.VMEM((N_GROUPS, S, 2 * HEAD_DIM), BF16),
            pltpu.VMEM((S, HEAD_DIM), F32),
            pltpu.VMEM((S, HEAD_DIM), F32),
            pltpu.VMEM((S, HEAD_DIM), F32),
            pltpu.VMEM((S // ATTN_BLOCK, ATTN_BLOCK, 2 * ATTN_BLOCK), F32),
            pltpu.VMEM((S // ATTN_BLOCK, ATTN_BLOCK, 2 * ATTN_BLOCK), BF16),
            pltpu.VMEM((N_GROUPS, S // ATTN_BLOCK * ATTN_PITCH, HEAD_DIM), F32),
            pltpu.VMEM((N_GROUPS, S // ATTN_BLOCK * ATTN_PITCH, HEAD_DIM), F32),
            pltpu.VMEM((N_GROUPS, S // ATTN_BLOCK * ATTN_PITCH, HEAD_DIM), F32),
        ],
        compiler_params=pltpu.CompilerParams(
            dimension_semantics=("parallel", "parallel"), vmem_limit_bytes=VMEM_LIMIT_BYTES),
        name="dilated_attn",
    )(qkv3, qkv3, qkv3, qkv3, qkv3, bias_tab, bias_tab, bias_tab)


def _gelu_tanh(x):
    return 0.5 * x * (1.0 + jnp.tanh(math.sqrt(2.0 / math.pi) * (x + 0.044715 * (x * x * x))))


def _mix_out_kernel(y_ref, o_ref, mg_ref, x_ref, wbr_ref, wba_ref, wout_ref, gmix_ref,
                    gpre_ref, wgu_ref, cw_ref, cb_ref, wd_ref, gpost_ref, out_ref,
                    mrg_scr, mix_scr, h_scr, hn_scr, gate_scr, up_scr, act_scr, ff_scr, *, tiles_per_seq):
    i = pl.program_id(0)
    sr, n_sub = FF_SR, FF_TM // FF_SR
    n_c = FFN_WIDTH // FF_TC

    @pl.when(i % tiles_per_seq == 0)
    def _():
        gate_scr[0, 0:SUBLANES, :] = jnp.zeros((SUBLANES, FFN_WIDTH), F32)

    def rows(r):
        return slice(r * sr, (r + 1) * sr)

    def stage_a(r, c):
        cols = slice(c * FF_DN, (c + 1) * FF_DN)
        br = jnp.dot(y_ref[rows(r), :], wbr_ref[:, cols], preferred_element_type=F32)
        ba = jnp.dot(o_ref[rows(r), :], wba_ref[:, cols], preferred_element_type=F32)
        g_rnn = _sigmoid(mg_ref[rows(r), c * FF_DN:(c + 1) * FF_DN].astype(F32))
        g_att = _sigmoid(mg_ref[rows(r), D_MODEL + c * FF_DN:D_MODEL + (c + 1) * FF_DN].astype(F32))
        mrg_scr[r % 2, :, cols] = (g_rnn * br + g_att * ba).astype(BF16)

    def stage_c(r, c):
        cols = slice(c * FF_DN, (c + 1) * FF_DN)
        mix_scr[r % 2, :, cols] = jnp.dot(mrg_scr[r % 2], wout_ref[:, cols], preferred_element_type=F32)

    def stage_n(r):
        h = x_ref[rows(r), :] + _rms_norm(mix_scr[r % 2], gmix_ref[...])
        h_scr[r] = h
        hn_scr[r % 2] = _rms_norm(h, gpre_ref[...]).astype(BF16)

    def stage_g(r, c):
        cols = slice(c * FF_TC, (c + 1) * FF_TC)
        res = jnp.dot(hn_scr[r % 2], wgu_ref[:, 2 * c * FF_TC:2 * (c + 1) * FF_TC], preferred_element_type=F32)
        gate = res[:, 0:FF_TC]
        gate_scr[r % 2, SUBLANES:SUBLANES + sr, cols] = gate
        gate_scr[(r + 1) % 2, 0:SUBLANES, cols] = gate[sr - SUBLANES:sr, :]
        up_scr[r % 2, :, cols] = res[:, FF_TC:2 * FF_TC]

    def stage_v(r, c):
        cols = slice(c * FF_TC, (c + 1) * FF_TC)
        conv = cb_ref[:, cols] + cw_ref[FFN_CONV - 1:FFN_CONV, cols] * gate_scr[r % 2, SUBLANES:SUBLANES + sr, cols]
        for k in range(FFN_CONV - 1):
            back = FFN_CONV - 1 - k
            conv = conv + cw_ref[k:k + 1, cols] * gate_scr[r % 2, SUBLANES - back:SUBLANES - back + sr, cols]
        act_scr[r % 2, :, cols] = (_gelu_tanh(conv) * up_scr[r % 2, :, cols]).astype(BF16)

    def stage_d(r, c):
        cols = slice(c * FF_DN, (c + 1) * FF_DN)
        ff_scr[r % 2, :, cols] = jnp.dot(act_scr[r % 2], wd_ref[:, cols], preferred_element_type=F32)

    def stage_e(r):
        out_ref[rows(r), :] = h_scr[r] + _rms_norm(ff_scr[r % 2], gpost_ref[...])

    n_d = D_MODEL // FF_DN
    live = lambda r: 0 <= r < n_sub
    for t in range(n_sub + 6):
        for c in range(n_d):
            if live(t):
                stage_a(t, c)
            if live(t - 1):
                stage_c(t - 1, c)
        if live(t - 2):
            stage_n(t - 2)
        for c in range(n_c):
            if live(t - 4):
                stage_v(t - 4, c)
            if live(t - 3):
                stage_g(t - 3, c)
            if live(t - 5):
                for d in range(c * n_d // n_c, (c + 1) * n_d // n_c):
                    stage_d(t - 5, d)
        if live(t - 6):
            stage_e(t - 6)


def _pack_gate_up_kernel(wg_ref, wu_ref, out_ref):
    out_ref[:, 0:FF_TC] = wg_ref[...].astype(BF16)
    out_ref[:, FF_TC:2 * FF_TC] = wu_ref[...].astype(BF16)


def _pack_gate_up(wg, wu):
    block = pl.BlockSpec((D_MODEL, FF_TC), lambda c: (0, c))
    return pl.pallas_call(
        _pack_gate_up_kernel,
        grid=(FFN_WIDTH // FF_TC,),
        in_specs=[block, block],
        out_specs=pl.BlockSpec((D_MODEL, 2 * FF_TC), lambda c: (0, c)),
        out_shape=jax.ShapeDtypeStruct((D_MODEL, 2 * FFN_WIDTH), BF16),
        name="pack_gate_up",
    )(wg, wu)


def _mix_out(y2, o2, mgates, x2, wbr, wba, wout, g_mix_post, g_pre, wgu, conv_w, conv_b, wd, g_post, seq_len):
    T = x2.shape[0]
    const = lambda i: (0, 0)
    row = lambda width: pl.BlockSpec((FF_TM, width), lambda i: (i, 0))
    resident = lambda shape: pl.BlockSpec(shape, const, pipeline_mode=pl.Buffered(1))
    return pl.pallas_call(
        functools.partial(_mix_out_kernel, tiles_per_seq=seq_len // FF_TM),
        grid=(T // FF_TM,),
        in_specs=[
            row(RNN_WIDTH), row(KV_HEADS * HEAD_DIM), row(GATE_WIDTH), row(D_MODEL),
            resident((RNN_WIDTH, D_MODEL)),
            resident((KV_HEADS * HEAD_DIM, D_MODEL)),
            resident((D_MODEL, D_MODEL)),
            pl.BlockSpec((1, D_MODEL), const),
            pl.BlockSpec((1, D_MODEL), const),
            resident((D_MODEL, 2 * FFN_WIDTH)),
            pl.BlockSpec((FFN_CONV, FFN_WIDTH), const),
            pl.BlockSpec((1, FFN_WIDTH), const),
            resident((FFN_WIDTH, D_MODEL)),
            pl.BlockSpec((1, D_MODEL), const),
        ],
        out_specs=row(D_MODEL),
        out_shape=jax.ShapeDtypeStruct((T, D_MODEL), F32),
        scratch_shapes=[
            pltpu.VMEM((2, FF_SR, D_MODEL), BF16),
            pltpu.VMEM((2, FF_SR, D_MODEL), F32),
            pltpu.VMEM((FF_TM // FF_SR, FF_SR, D_MODEL), F32),
            pltpu.VMEM((2, FF_SR, D_MODEL), BF16),
            pltpu.VMEM((2, SUBLANES + FF_SR, FFN_WIDTH), F32),
            pltpu.VMEM((2, FF_SR, FFN_WIDTH), F32),
            pltpu.VMEM((2, FF_SR, FFN_WIDTH), BF16),
            pltpu.VMEM((2, FF_SR, D_MODEL), F32),
        ],
        compiler_params=pltpu.CompilerParams(
            dimension_semantics=("arbitrary",), vmem_limit_bytes=VMEM_LIMIT_BYTES),
        name="mix_out",
    )(y2, o2, mgates, x2, wbr, wba, wout, g_mix_post, g_pre, wgu, conv_w, conv_b, wd, g_post)


def kernel(x, rel_bias, norm_mix_pre, norm_mix_post, w_in, conv_rnn_w, conv_rnn_b, w_rg_a, b_rg_a,
           w_rg_x, b_rg_x, lru_lambda, w_branch_rnn, w_branch_att, w_out, norm_ffn_pre, norm_ffn_post,
           w_ffn_gate, w_ffn_up, conv_ffn_w, conv_ffn_b, w_ffn_down):
    B, S, D = x.shape
    assert D == D_MODEL and S % FF_TM == 0 and S % RG_TS == 0 and B % RG_NB == 0
    assert (FF_TM // FF_SR) % 2 == 0, "the conv halo hand-off between grid steps relies on an even sub-tile count"
    assert S == DILATED_CONFIGS[-1][0], "attention block structure assumes window == sequence for the widest group"
    depth = w_in.shape[0]
    T = B * S
    bias_tab = _bias_tables(rel_bias)
    h = x.reshape(T, D)
    for l in range(depth):
        wax = jnp.concatenate([w_rg_a[l], w_rg_x[l]], axis=-1).astype(BF16)
        bax = jnp.stack([b_rg_a[l], b_rg_x[l]], axis=0)
        y_rnn, qkv, gates = _mix_in(h.reshape(B, S, D), norm_mix_pre[l][None], w_in[l].astype(BF16),
                                    conv_rnn_w[l], conv_rnn_b[l][None], wax, bax, lru_lambda[l][None])
        o_att = _dilated_attention(qkv, bias_tab)
        wgu = _pack_gate_up(w_ffn_gate[l], w_ffn_up[l])
        h = _mix_out(y_rnn.reshape(T, RNN_WIDTH), o_att.reshape(T, KV_HEADS * HEAD_DIM),
                     gates.reshape(T, GATE_WIDTH), h,
                     w_branch_rnn[l].astype(BF16), w_branch_att[l].astype(BF16), w_out[l].astype(BF16),
                     norm_mix_post[l][None], norm_ffn_pre[l][None],
                     wgu, conv_ffn_w[l], conv_ffn_b[l][None],
                     w_ffn_down[l].astype(BF16), norm_ffn_post[l][None], S)
    return h.reshape(B, S, D)
```

```python
import functools
import math

import numpy as np
import jax
import jax.numpy as jnp
from jax import lax
from jax.experimental import pallas as pl
from jax.experimental.pallas import tpu as pltpu

F32 = jnp.float32
BF16 = jnp.bfloat16

D_MODEL = 1024
RNN_WIDTH = 1280
RNN_BLOCKS = 10
RNN_BLOCK = 128
RNN_CONV = 4
LRU_C = 8.0
HEAD_DIM = 128
KV_HEADS = 4
DILATED_CONFIGS = ((128, 1), (512, 4), (2048, 16))
N_GROUPS = 3
Q_HEADS = 12
ATTN_BLOCK = 128
ATTN_PITCH = ATTN_BLOCK + 8
REL_BUCKETS = 32
REL_MAX_DIST = 2048
FFN_WIDTH = 3072
FFN_CONV = 3
EPS = 1e-6
QKV_HEADS = Q_HEADS + 2 * KV_HEADS
QKV_WIDTH = QKV_HEADS * HEAD_DIM
GATE_WIDTH = 2 * D_MODEL
IN_WIDTH = RNN_WIDTH + QKV_WIDTH + GATE_WIDTH

SUBLANES = 8
VMEM_LIMIT_BYTES = 56 * 1024 * 1024

MASK_VALUE = -1e30
LOG2E = math.log2(math.e)
QK_SCALE_LOG2 = HEAD_DIM ** -0.5 * LOG2E

RG_TS = 64
RG_NB = 8
RG_PITCH = RG_TS + 8
RG_TC = 16
MIX_TN = 256
FF_TM = 512
FF_SR = 256
FF_TC = 256
FF_DN = 256


def _rms_norm(x, g):
    ms = jnp.mean(x * x, axis=-1, keepdims=True)
    return x * lax.rsqrt(ms + EPS) * g


def _sigmoid(x):
    return 0.5 + 0.5 * jnp.tanh(0.5 * x)


def _t5_bucket(dist):
    max_exact = REL_BUCKETS // 2
    d = np.maximum(dist, 1).astype(np.float32)
    large = max_exact + np.log(d / max_exact) / math.log(REL_MAX_DIST / max_exact) * (REL_BUCKETS - max_exact)
    large = np.minimum(large.astype(np.int32), REL_BUCKETS - 1)
    return np.where(dist < max_exact, dist, large).astype(np.int32)


def _bucket_maps():
    qi = np.arange(ATTN_BLOCK)[:, None]
    kj = np.arange(2 * ATTN_BLOCK)[None, :]
    delta = ATTN_BLOCK + qi - kj
    maps = []
    for window, dilation in DILATED_CONFIGS:
        n_back = window // dilation
        valid = (delta >= 0) & (delta <= n_back)
        bucket = _t5_bucket(np.maximum(delta, 0) * dilation)
        maps.append(np.where(valid, bucket, -1).astype(np.int32))
    return np.stack(maps, axis=0)


def _bias_kernel(rb_ref, bucket_ref, out_ref):
    head = pl.program_id(0)
    bk = bucket_ref[0]
    acc = jnp.full(bk.shape, MASK_VALUE, F32)
    for b in range(REL_BUCKETS):
        acc = jnp.where(bk == b, rb_ref[b, head] * LOG2E, acc)
    out_ref[0] = acc


def _bias_tables(rel_bias):
    buckets = jnp.asarray(_bucket_maps())
    return pl.pallas_call(
        _bias_kernel,
        grid=(Q_HEADS,),
        in_specs=[
            pl.BlockSpec(memory_space=pltpu.SMEM),
            pl.BlockSpec((1, ATTN_BLOCK, 2 * ATTN_BLOCK), lambda h: (h // KV_HEADS, 0, 0)),
        ],
        out_specs=pl.BlockSpec((1, ATTN_BLOCK, 2 * ATTN_BLOCK), lambda h: (h, 0, 0)),
        out_shape=jax.ShapeDtypeStruct((Q_HEADS, ATTN_BLOCK, 2 * ATTN_BLOCK), F32),
        name="bias_table",
    )(rel_bias, buckets)


def _rglru_chunk(n, t0, xr_scr, y_scr, carry, cw_ref, cb_ref, wax_ref, bax_ref, lam_ref):
    pitch, tc = RG_PITCH, RG_TC
    cols = slice(n * RNN_BLOCK, (n + 1) * RNN_BLOCK)
    bcast = lambda row: jnp.broadcast_to(row, (RG_NB, RNN_BLOCK))
    w = [bcast(cw_ref[k:k + 1, cols]) for k in range(RNN_CONV)]
    cb = bcast(cb_ref[:, cols])
    xs = [carry[n, k] for k in range(RNN_CONV - 1)]
    xs += [xr_scr[n, pl.ds(t0 + j, RG_NB, stride=pitch), :] for j in range(tc)]
    xc = jnp.concatenate(
        [cb + w[3] * xs[j + 3] + w[2] * xs[j + 2] + w[1] * xs[j + 1] + w[0] * xs[j] for j in range(tc)],
        axis=0)
    g = jnp.dot(xc.astype(BF16), wax_ref[n], preferred_element_type=F32)
    r = _sigmoid(g[:, 0:RNN_BLOCK] + bax_ref[0:1, cols])
    i = _sigmoid(g[:, RNN_BLOCK:2 * RNN_BLOCK] + bax_ref[1:2, cols])
    neg_lam = -lam_ref[:, cols]
    softplus = jnp.maximum(neg_lam, 0.0) + jnp.log1p(jnp.exp(-jnp.abs(neg_lam)))
    log_a = (-LRU_C * softplus) * r
    a = jnp.exp(log_a)
    th = jnp.tanh(log_a)
    u = jnp.sqrt(-2.0 * th / (1.0 - th)) * (i * xc)
    h = carry[n, RNN_CONV - 1]
    for j in range(tc):
        h = a[j * RG_NB:(j + 1) * RG_NB] * h + u[j * RG_NB:(j + 1) * RG_NB]
        y_scr[n, pl.ds(t0 + j, RG_NB, stride=pitch), :] = h
    for k in range(RNN_CONV - 1):
        carry[n, k] = xs[tc + k]
    carry[n, RNN_CONV - 1] = h


def _mix_in_kernel(x_ref, g_ref, w_ref, cw_ref, cb_ref, wax_ref, bax_ref, lam_ref,
                   y_ref, qkv_ref, gate_ref, xr_scr, y_scr, carry):
    s = pl.program_id(1)
    ts, pitch = RG_TS, RG_PITCH
    rows = RG_NB * ts

    @pl.when(s == 0)
    def _():
        carry[...] = jnp.zeros_like(carry)

    hn = _rms_norm(x_ref[...].reshape(rows, D_MODEL), g_ref[...]).astype(BF16)
    c0, c1 = RNN_WIDTH, RNN_WIDTH + QKV_WIDTH

    def project(pair):
        width = MIX_TN * len(pair)
        both = jnp.dot(hn, w_ref[:, pair[0]:pair[0] + width], preferred_element_type=F32)
        for col in pair:
            route(col, both[:, col - pair[0]:col - pair[0] + MIX_TN])

    def route(col, out):
        if col < c0:
            for n in range(col // RNN_BLOCK, (col + MIX_TN) // RNN_BLOCK):
                lanes = slice(n * RNN_BLOCK - col, (n + 1) * RNN_BLOCK - col)
                for b in range(RG_NB):
                    xr_scr[n, b * pitch:b * pitch + ts, :] = out[b * ts:(b + 1) * ts, lanes]
        elif col < c1:
            for hd in range((col - c0) // HEAD_DIM, (col - c0 + MIX_TN) // HEAD_DIM):
                lanes = slice(c0 + hd * HEAD_DIM - col, c0 + (hd + 1) * HEAD_DIM - col)
                qkv_ref[:, hd, :, :] = out[:, lanes].reshape(RG_NB, ts, HEAD_DIM)
        else:
            gate_ref[:, :, col - c1:col - c1 + MIX_TN] = out.astype(BF16).reshape(RG_NB, ts, MIX_TN)

    def recur(n, ci):
        _rglru_chunk(n, ci * RG_TC, xr_scr, y_scr, carry, cw_ref, cb_ref, wax_ref, bax_ref, lam_ref)
        if ci == ts // RG_TC - 1:
            for b in range(RG_NB):
                y_ref[b, :, n * RNN_BLOCK:(n + 1) * RNN_BLOCK] = (
                    y_scr[n, b * pitch:b * pitch + ts, :].astype(BF16))

    slabs_per_block = MIX_TN // RNN_BLOCK
    units = [(n, ci) for n0 in range(0, RNN_BLOCKS, slabs_per_block) for ci in range(ts // RG_TC)
             for n in range(n0, n0 + slabs_per_block)]
    cols = list(range(0, IN_WIDTH, MIX_TN))
    pairs = [cols[j:j + 2] for j in range(0, len(cols), 2)]
    project(pairs[0])
    done = 1
    for k, (n, ci) in enumerate(units):
        recur(n, ci)
        want = 1 + ((k + 1) * (len(pairs) - 1) + len(units) - 1) // len(units)
        while done < want:
            project(pairs[done])
            done += 1


def _mix_in(x3, g, w_bf, conv_w, conv_b, wax, bax, lam):
    B, S, _ = x3.shape
    const2 = lambda b, s: (0, 0)
    tile = lambda width: pl.BlockSpec((RG_NB, RG_TS, width), lambda b, s: (b, s, 0))
    scan_scratch = pltpu.VMEM((RNN_BLOCKS, RG_NB * RG_PITCH, RNN_BLOCK), F32)
    return pl.pallas_call(
        _mix_in_kernel,
        grid=(B // RG_NB, S // RG_TS),
        in_specs=[
            tile(D_MODEL),
            pl.BlockSpec((1, D_MODEL), const2),
            pl.BlockSpec((D_MODEL, IN_WIDTH), const2, pipeline_mode=pl.Buffered(1)),
            pl.BlockSpec((RNN_CONV, RNN_WIDTH), const2),
            pl.BlockSpec((1, RNN_WIDTH), const2),
            pl.BlockSpec((RNN_BLOCKS, RNN_BLOCK, 2 * RNN_BLOCK), lambda b, s: (0, 0, 0)),
            pl.BlockSpec((2, RNN_WIDTH), const2),
            pl.BlockSpec((1, RNN_WIDTH), const2),
        ],
        out_specs=[
            tile(RNN_WIDTH),
            pl.BlockSpec((RG_NB, QKV_HEADS, RG_TS, HEAD_DIM), lambda b, s: (b, 0, s, 0)),
            tile(GATE_WIDTH),
        ],
        out_shape=[
            jax.ShapeDtypeStruct((B, S, RNN_WIDTH), BF16),
            jax.ShapeDtypeStruct((B, QKV_HEADS, S, HEAD_DIM), F32),
            jax.ShapeDtypeStruct((B, S, GATE_WIDTH), BF16),
        ],
        scratch_shapes=[
            scan_scratch, scan_scratch,
            pltpu.VMEM((RNN_BLOCKS, RNN_CONV, RG_NB, RNN_BLOCK), F32),
        ],
        compiler_params=pltpu.CompilerParams(
            dimension_semantics=("parallel", "arbitrary"), vmem_limit_bytes=VMEM_LIMIT_BYTES),
        name="mix_in",
    )(x3, g, w_bf, conv_w, conv_b, wax, bax, lam)


def _dilated_kernel(q1_ref, q2_ref, q3_ref, k_ref, v_ref, b1_ref, b2_ref, b3_ref, o_ref,
                    qp, kp, vp, tq, tk, tv, s_scr, p_scr, o_scr, m_scr, d_scr):
    S = k_ref.shape[0]
    blk = ATTN_BLOCK
    n_blocks = S // blk
    quarter = S // 4
    nt = (((1,), (1,)), ((), ()))

    def put(g, r0, n, q, k, v):
        qp[g, r0:r0 + n, :] = (q * QK_SCALE_LOG2).astype(BF16)
        kp[g, r0:r0 + n, :] = k.astype(BF16)
        vp[g, r0:r0 + n, 0:HEAD_DIM] = v.astype(BF16)

    @pl.when((pl.program_id(0) == 0) & (pl.program_id(1) == 0))
    def _():
        for g in range(N_GROUPS):
            vp[g, :, HEAD_DIM:2 * HEAD_DIM] = jnp.ones((S, HEAD_DIM), BF16)
    for c in range(4):
        r = slice(c * quarter, (c + 1) * quarter)
        put(0, c * quarter, quarter, q1_ref[r, :], k_ref[r, :], v_ref[r, :])
    for c in range(4):
        sr = pl.ds(c, quarter, stride=4)
        r = slice(c * quarter, (c + 1) * quarter)
        k4, v4 = k_ref[sr, :], v_ref[sr, :]
        put(1, c * quarter, quarter, q2_ref[sr, :], k4, v4)
        tk[r, :] = k4
        tv[r, :] = v4
        tq[r, :] = q3_ref[sr, :]
    for c4 in range(4):
        for j in range(4):
            sr = pl.ds(c4 * quarter + j, blk, stride=4)
            put(2, (4 * j + c4) * blk, blk, tq[sr, :], tk[sr, :], tv[sr, :])

    def run_group(g, b_ref, dil):
        nb = S // dil // blk

        def nat_rows(p):
            c, n = divmod(p, nb)
            if nb == 1:
                return slice(c * ATTN_PITCH, c * ATTN_PITCH + blk)
            start = n * (blk * dil) + c
            return pl.ds(start, blk) if dil == 1 else pl.ds(start, blk, stride=dil)

        def key_rows(p):
            has_prev = p % nb != 0
            return (slice((p - 1) * blk, (p + 1) * blk) if has_prev else slice(p * blk, (p + 1) * blk),
                    slice(0, 2 * blk) if has_prev else slice(blk, 2 * blk))

        for p in range(n_blocks):
            kr, cols = key_rows(p)
            s = lax.dot_general(qp[g, p * blk:(p + 1) * blk, :], kp[g, kr, :], nt, preferred_element_type=F32)
            s_scr[p, :, cols] = s + b_ref[:, cols]
        for p in range(n_blocks):
            _, cols = key_rows(p)
            m = jnp.max(s_scr[p, :, cols], axis=-1, keepdims=True)
            m_scr[g, nat_rows(p), :] = jnp.broadcast_to(m, (blk, HEAD_DIM))
            p_scr[p, :, cols] = jnp.exp2(s_scr[p, :, cols] - m).astype(BF16)
        for p in range(n_blocks):
            kr, cols = key_rows(p)
            od = jnp.dot(p_scr[p, :, cols], vp[g, kr, :], preferred_element_type=F32)
            o_scr[g, nat_rows(p), :] = od[:, 0:HEAD_DIM]
            d_scr[g, nat_rows(p), :] = od[:, HEAD_DIM:2 * HEAD_DIM]

    run_group(0, b1_ref, DILATED_CONFIGS[0][1])
    run_group(1, b2_ref, DILATED_CONFIGS[1][1])
    run_group(2, b3_ref, DILATED_CONFIGS[2][1])

    chunk = 256

    last_dil = DILATED_CONFIGS[2][1]

    def widest(scr, i):
        per = chunk // last_dil
        return jnp.concatenate(
            [scr[2, pl.ds(i * per + j, last_dil, stride=ATTN_PITCH), :] for j in range(per)], axis=0)

    def merge(i, carry):
        r = pl.ds(pl.multiple_of(i * chunk, chunk), chunk)
        m0, m1, m2 = m_scr[0, r, :], m_scr[1, r, :], widest(m_scr, i)
        mx = jnp.maximum(jnp.maximum(m0, m1), m2)
        w0, w1, w2 = jnp.exp2(m0 - mx), jnp.exp2(m1 - mx), jnp.exp2(m2 - mx)
        num = w0 * o_scr[0, r, :] + w1 * o_scr[1, r, :] + w2 * widest(o_scr, i)
        z = w0 * d_scr[0, r, :] + w1 * d_scr[1, r, :] + w2 * widest(d_scr, i)
        o_ref[r, :] = (num / z).astype(o_ref.dtype)
        return carry

    lax.fori_loop(0, S // chunk, merge, 0)


def _dilated_attention(qkv3, bias_tab):
    B, _, S, _ = qkv3.shape
    head_spec = lambda col0: pl.BlockSpec((None, None, S, HEAD_DIM), lambda b, h: (b, col0 + h, 0, 0))
    bias_spec = lambda g: pl.BlockSpec(
        (None, ATTN_BLOCK, 2 * ATTN_BLOCK), lambda b, h: (g * KV_HEADS + h, 0, 0))
    return pl.pallas_call(
        _dilated_kernel,
        grid=(B, KV_HEADS),
        in_specs=[
            head_spec(0), head_spec(KV_HEADS), head_spec(2 * KV_HEADS),
            head_spec(Q_HEADS), head_spec(Q_HEADS + KV_HEADS),
            bias_spec(0), bias_spec(1), bias_spec(2),
        ],
        out_specs=pl.BlockSpec((None, S, HEAD_DIM), lambda b, h: (b, 0, h)),
        out_shape=jax.ShapeDtypeStruct((B, S, KV_HEADS * HEAD_DIM), BF16),
        scratch_shapes=[
            pltpu.VMEM((N_GROUPS, S, HEAD_DIM), BF16),
            pltpu.VMEM((N_GROUPS, S, HEAD_DIM), BF16),
            pltpu.VMEM((N_GROUPS, S, 2 * HEAD_DIM), BF16),
            pltpu.VMEM((S, HEAD_DIM), F32),
            pltpu.VMEM((S, HEAD_DIM), F32),
            pltpu.VMEM((S, HEAD_DIM), F32),
            pltpu.VMEM((S // ATTN_BLOCK, ATTN_BLOCK, 2 * ATTN_BLOCK), F32),
            pltpu.VMEM((S // ATTN_BLOCK, ATTN_BLOCK, 2 * ATTN_BLOCK), BF16),
            pltpu.VMEM((N_GROUPS, S // ATTN_BLOCK * ATTN_PITCH, HEAD_DIM), F32),
            pltpu.VMEM((N_GROUPS, S // ATTN_BLOCK * ATTN_PITCH, HEAD_DIM), F32),
            pltpu.VMEM((N_GROUPS, S // ATTN_BLOCK * ATTN_PITCH, HEAD_DIM), F32),
        ],
        compiler_params=pltpu.CompilerParams(
            dimension_semantics=("arbitrary", "arbitrary"), vmem_limit_bytes=VMEM_LIMIT_BYTES),
        name="dilated_attn",
    )(qkv3, qkv3, qkv3, qkv3, qkv3, bias_tab, bias_tab, bias_tab)


def _gelu_tanh(x):
    return 0.5 * x * (1.0 + jnp.tanh(math.sqrt(2.0 / math.pi) * (x + 0.044715 * (x * x * x))))


def _mix_out_kernel(y_ref, o_ref, mg_ref, x_ref, wbr_ref, wba_ref, wout_ref, gmix_ref,
                    gpre_ref, wgu_ref, cw_ref, cb_ref, wd_ref, gpost_ref, out_ref,
                    mrg_scr, mix_scr, h_scr, hn_scr, gate_scr, up_scr, act_scr, ff_scr, *, tiles_per_seq):
    i = pl.program_id(0)
    sr, n_sub = FF_SR, FF_TM // FF_SR
    n_c = FFN_WIDTH // FF_TC

    @pl.when(i % tiles_per_seq == 0)
    def _():
        gate_scr[0, 0:SUBLANES, :] = jnp.zeros((SUBLANES, FFN_WIDTH), F32)

    def rows(r):
        return slice(r * sr, (r + 1) * sr)

    def stage_a(r, c):
        cols = slice(c * FF_DN, (c + 1) * FF_DN)
        br = jnp.dot(y_ref[rows(r), :], wbr_ref[:, cols], preferred_element_type=F32)
        ba = jnp.dot(o_ref[rows(r), :], wba_ref[:, cols], preferred_element_type=F32)
        g_rnn = _sigmoid(mg_ref[rows(r), c * FF_DN:(c + 1) * FF_DN].astype(F32))
        g_att = _sigmoid(mg_ref[rows(r), D_MODEL + c * FF_DN:D_MODEL + (c + 1) * FF_DN].astype(F32))
        mrg_scr[r % 2, :, cols] = (g_rnn * br + g_att * ba).astype(BF16)

    def stage_c(r, c):
        cols = slice(c * FF_DN, (c + 1) * FF_DN)
        mix_scr[r % 2, :, cols] = jnp.dot(mrg_scr[r % 2], wout_ref[:, cols], preferred_element_type=F32)

    def stage_n(r):
        h = x_ref[rows(r), :] + _rms_norm(mix_scr[r % 2], gmix_ref[...])
        h_scr[r] = h
        hn_scr[r % 2] = _rms_norm(h, gpre_ref[...]).astype(BF16)

    def stage_g(r, c):
        cols = slice(c * FF_TC, (c + 1) * FF_TC)
        res = jnp.dot(hn_scr[r % 2], wgu_ref[:, 2 * c * FF_TC:2 * (c + 1) * FF_TC], preferred_element_type=F32)
        gate = res[:, 0:FF_TC]
        gate_scr[r % 2, SUBLANES:SUBLANES + sr, cols] = gate
        gate_scr[(r + 1) % 2, 0:SUBLANES, cols] = gate[sr - SUBLANES:sr, :]
        up_scr[r % 2, :, cols] = res[:, FF_TC:2 * FF_TC]

    def stage_v(r, c):
        cols = slice(c * FF_TC, (c + 1) * FF_TC)
        conv = cb_ref[:, cols] + cw_ref[FFN_CONV - 1:FFN_CONV, cols] * gate_scr[r % 2, SUBLANES:SUBLANES + sr, cols]
        for k in range(FFN_CONV - 1):
            back = FFN_CONV - 1 - k
            conv = conv + cw_ref[k:k + 1, cols] * gate_scr[r % 2, SUBLANES - back:SUBLANES - back + sr, cols]
        act_scr[r % 2, :, cols] = (_gelu_tanh(conv) * up_scr[r % 2, :, cols]).astype(BF16)

    def stage_d(r, c):
        cols = slice(c * FF_DN, (c + 1) * FF_DN)
        ff_scr[r % 2, :, cols] = jnp.dot(act_scr[r % 2], wd_ref[:, cols], preferred_element_type=F32)

    def stage_e(r):
        out_ref[rows(r), :] = h_scr[r] + _rms_norm(ff_scr[r % 2], gpost_ref[...])

    n_d = D_MODEL // FF_DN
    live = lambda r: 0 <= r < n_sub
    for t in range(n_sub + 6):
        for c in range(n_d):
            if live(t):
                stage_a(t, c)
            if live(t - 1):
                stage_c(t - 1, c)
        if live(t - 2):
            stage_n(t - 2)
        for c in range(n_c):
            if live(t - 4):
                stage_v(t - 4, c)
            if live(t - 3):
                stage_g(t - 3, c)
            if live(t - 5):
                for d in range(c * n_d // n_c, (c + 1) * n_d // n_c):
                    stage_d(t - 5, d)
        if live(t - 6):
            stage_e(t - 6)


def _pack_gate_up_kernel(wg_ref, wu_ref, out_ref):
    out_ref[:, 0:FF_TC] = wg_ref[...].astype(BF16)
    out_ref[:, FF_TC:2 * FF_TC] = wu_ref[...].astype(BF16)


def _pack_gate_up(wg, wu):
    block = pl.BlockSpec((D_MODEL, FF_TC), lambda c: (0, c))
    return pl.pallas_call(
        _pack_gate_up_kernel,
        grid=(FFN_WIDTH // FF_TC,),
        in_specs=[block, block],
        out_specs=pl.BlockSpec((D_MODEL, 2 * FF_TC), lambda c: (0, c)),
        out_shape=jax.ShapeDtypeStruct((D_MODEL, 2 * FFN_WIDTH), BF16),
        name="pack_gate_up",
    )(wg, wu)


def _mix_out(y2, o2, mgates, x2, wbr, wba, wout, g_mix_post, g_pre, wgu, conv_w, conv_b, wd, g_post, seq_len):
    T = x2.shape[0]
    const = lambda i: (0, 0)
    row = lambda width: pl.BlockSpec((FF_TM, width), lambda i: (i, 0))
    resident = lambda shape: pl.BlockSpec(shape, const, pipeline_mode=pl.Buffered(1))
    return pl.pallas_call(
        functools.partial(_mix_out_kernel, tiles_per_seq=seq_len // FF_TM),
        grid=(T // FF_TM,),
        in_specs=[
            row(RNN_WIDTH), row(KV_HEADS * HEAD_DIM), row(GATE_WIDTH), row(D_MODEL),
            resident((RNN_WIDTH, D_MODEL)),
            resident((KV_HEADS * HEAD_DIM, D_MODEL)),
            resident((D_MODEL, D_MODEL)),
            pl.BlockSpec((1, D_MODEL), const),
            pl.BlockSpec((1, D_MODEL), const),
            resident((D_MODEL, 2 * FFN_WIDTH)),
            pl.BlockSpec((FFN_CONV, FFN_WIDTH), const),
            pl.BlockSpec((1, FFN_WIDTH), const),
            resident((FFN_WIDTH, D_MODEL)),
            pl.BlockSpec((1, D_MODEL), const),
        ],
        out_specs=row(D_MODEL),
        out_shape=jax.ShapeDtypeStruct((T, D_MODEL), F32),
        scratch_shapes=[
            pltpu.VMEM((2, FF_SR, D_MODEL), BF16),
            pltpu.VMEM((2, FF_SR, D_MODEL), F32),
            pltpu.VMEM((FF_TM // FF_SR, FF_SR, D_MODEL), F32),
            pltpu.VMEM((2, FF_SR, D_MODEL), BF16),
            pltpu.VMEM((2, SUBLANES + FF_SR, FFN_WIDTH), F32),
            pltpu.VMEM((2, FF_SR, FFN_WIDTH), F32),
            pltpu.VMEM((2, FF_SR, FFN_WIDTH), BF16),
            pltpu.VMEM((2, FF_SR, D_MODEL), F32),
        ],
        compiler_params=pltpu.CompilerParams(
            dimension_semantics=("arbitrary",), vmem_limit_bytes=VMEM_LIMIT_BYTES),
        name="mix_out",
    )(y2, o2, mgates, x2, wbr, wba, wout, g_mix_post, g_pre, wgu, conv_w, conv_b, wd, g_post)


def kernel(x, rel_bias, norm_mix_pre, norm_mix_post, w_in, conv_rnn_w, conv_rnn_b, w_rg_a, b_rg_a,
           w_rg_x, b_rg_x, lru_lambda, w_branch_rnn, w_branch_att, w_out, norm_ffn_pre, norm_ffn_post,
           w_ffn_gate, w_ffn_up, conv_ffn_w, conv_ffn_b, w_ffn_down):
    B, S, D = x.shape
    assert D == D_MODEL and S % FF_TM == 0 and S % RG_TS == 0 and B % RG_NB == 0
    assert (FF_TM // FF_SR) % 2 == 0, "the conv halo hand-off between grid steps relies on an even sub-tile count"
    assert S == DILATED_CONFIGS[-1][0], "attention block structure assumes window == sequence for the widest group"
    depth = w_in.shape[0]
    T = B * S
    bias_tab = _bias_tables(rel_bias)
    h = x.reshape(T, D)
    for l in range(depth):
        wax = jnp.concatenate([w_rg_a[l], w_rg_x[l]], axis=-1).astype(BF16)
        bax = jnp.stack([b_rg_a[l], b_rg_x[l]], axis=0)
        y_rnn, qkv, gates = _mix_in(h.reshape(B, S, D), norm_mix_pre[l][None], w_in[l].astype(BF16),
                                    conv_rnn_w[l], conv_rnn_b[l][None], wax, bax, lru_lambda[l][None])
        o_att = _dilated_attention(qkv, bias_tab)
        wgu = _pack_gate_up(w_ffn_gate[l], w_ffn_up[l])
        h = _mix_out(y_rnn.reshape(T, RNN_WIDTH), o_att.reshape(T, KV_HEADS * HEAD_DIM),
                     gates.reshape(T, GATE_WIDTH), h,
                     w_branch_rnn[l].astype(BF16), w_branch_att[l].astype(BF16), w_out[l].astype(BF16),
                     norm_mix_post[l][None], norm_ffn_pre[l][None],
                     wgu, conv_ffn_w[l], conv_ffn_b[l][None],
                     w_ffn_down[l].astype(BF16), norm_ffn_post[l][None], S)
    return h.reshape(B, S, D)
```

```python
import functools
import math

import numpy as np
import jax
import jax.numpy as jnp
from jax import lax
from jax.experimental import pallas as pl
from jax.experimental.pallas import tpu as pltpu

F32 = jnp.float32
BF16 = jnp.bfloat16

D_MODEL = 1024
RNN_WIDTH = 1280
RNN_BLOCKS = 10
RNN_BLOCK = 128
RNN_CONV = 4
LRU_C = 8.0
HEAD_DIM = 128
KV_HEADS = 4
DILATED_CONFIGS = ((128, 1), (512, 4), (2048, 16))
N_GROUPS = 3
Q_HEADS = 12
ATTN_BLOCK = 128
ATTN_PITCH = ATTN_BLOCK + 8
REL_BUCKETS = 32
REL_MAX_DIST = 2048
FFN_WIDTH = 3072
FFN_CONV = 3
EPS = 1e-6
QKV_HEADS = Q_HEADS + 2 * KV_HEADS
QKV_WIDTH = QKV_HEADS * HEAD_DIM
GATE_WIDTH = 2 * D_MODEL
IN_WIDTH = RNN_WIDTH + QKV_WIDTH + GATE_WIDTH

SUBLANES = 8
VMEM_LIMIT_BYTES = 56 * 1024 * 1024

MASK_VALUE = -1e30
LOG2E = math.log2(math.e)
QK_SCALE_LOG2 = HEAD_DIM ** -0.5 * LOG2E

RG_TS = 64
RG_NB = 8
RG_PITCH = RG_TS + 8
RG_TC = 16
MIX_TN = 256
FF_TM = 512
FF_SR = 256
FF_TC = 256
FF_DN = 256


def _rms_norm(x, g):
    ms = jnp.mean(x * x, axis=-1, keepdims=True)
    return x * lax.rsqrt(ms + EPS) * g


def _sigmoid(x):
    return 0.5 + 0.5 * jnp.tanh(0.5 * x)


def _t5_bucket(dist):
    max_exact = REL_BUCKETS // 2
    d = np.maximum(dist, 1).astype(np.float32)
    large = max_exact + np.log(d / max_exact) / math.log(REL_MAX_DIST / max_exact) * (REL_BUCKETS - max_exact)
    large = np.minimum(large.astype(np.int32), REL_BUCKETS - 1)
    return np.where(dist < max_exact, dist, large).astype(np.int32)


def _bucket_maps():
    qi = np.arange(ATTN_BLOCK)[:, None]
    kj = np.arange(2 * ATTN_BLOCK)[None, :]
    delta = ATTN_BLOCK + qi - kj
    maps = []
    for window, dilation in DILATED_CONFIGS:
        n_back = window // dilation
        valid = (delta >= 0) & (delta <= n_back)
        bucket = _t5_bucket(np.maximum(delta, 0) * dilation)
        maps.append(np.where(valid, bucket, -1).astype(np.int32))
    return np.stack(maps, axis=0)


def _bias_kernel(rb_ref, bucket_ref, out_ref):
    head = pl.program_id(0)
    bk = bucket_ref[0]
    acc = jnp.full(bk.shape, MASK_VALUE, F32)
    for b in range(REL_BUCKETS):
        acc = jnp.where(bk == b, rb_ref[b, head] * LOG2E, acc)
    out_ref[0] = acc


def _bias_tables(rel_bias):
    buckets = jnp.asarray(_bucket_maps())
    return pl.pallas_call(
        _bias_kernel,
        grid=(Q_HEADS,),
        in_specs=[
            pl.BlockSpec(memory_space=pltpu.SMEM),
            pl.BlockSpec((1, ATTN_BLOCK, 2 * ATTN_BLOCK), lambda h: (h // KV_HEADS, 0, 0)),
        ],
        out_specs=pl.BlockSpec((1, ATTN_BLOCK, 2 * ATTN_BLOCK), lambda h: (h, 0, 0)),
        out_shape=jax.ShapeDtypeStruct((Q_HEADS, ATTN_BLOCK, 2 * ATTN_BLOCK), F32),
        name="bias_table",
    )(rel_bias, buckets)


def _rglru_chunk(n, t0, xr_scr, y_scr, carry, cw_ref, cb_ref, wax_ref, bax_ref, lam_ref):
    pitch, tc = RG_PITCH, RG_TC
    cols = slice(n * RNN_BLOCK, (n + 1) * RNN_BLOCK)
    bcast = lambda row: jnp.broadcast_to(row, (RG_NB, RNN_BLOCK))
    w = [bcast(cw_ref[k:k + 1, cols]) for k in range(RNN_CONV)]
    cb = bcast(cb_ref[:, cols])
    xs = [carry[n, k] for k in range(RNN_CONV - 1)]
    xs += [xr_scr[n, pl.ds(t0 + j, RG_NB, stride=pitch), :] for j in range(tc)]
    xc = jnp.concatenate(
        [cb + w[3] * xs[j + 3] + w[2] * xs[j + 2] + w[1] * xs[j + 1] + w[0] * xs[j] for j in range(tc)],
        axis=0)
    g = jnp.dot(xc.astype(BF16), wax_ref[n], preferred_element_type=F32)
    r = _sigmoid(g[:, 0:RNN_BLOCK] + bax_ref[0:1, cols])
    i = _sigmoid(g[:, RNN_BLOCK:2 * RNN_BLOCK] + bax_ref[1:2, cols])
    neg_lam = -lam_ref[:, cols]
    softplus = jnp.maximum(neg_lam, 0.0) + jnp.log1p(jnp.exp(-jnp.abs(neg_lam)))
    log_a = (-LRU_C * softplus) * r
    a = jnp.exp(log_a)
    th = jnp.tanh(log_a)
    u = jnp.sqrt(-2.0 * th / (1.0 - th)) * (i * xc)
    h = carry[n, RNN_CONV - 1]
    for j in range(tc):
        h = a[j * RG_NB:(j + 1) * RG_NB] * h + u[j * RG_NB:(j + 1) * RG_NB]
        y_scr[n, pl.ds(t0 + j, RG_NB, stride=pitch), :] = h
    for k in range(RNN_CONV - 1):
        carry[n, k] = xs[tc + k]
    carry[n, RNN_CONV - 1] = h


def _mix_in_kernel(x_ref, g_ref, w_ref, cw_ref, cb_ref, wax_ref, bax_ref, lam_ref,
                   y_ref, qkv_ref, gate_ref, xr_scr, y_scr, carry):
    s = pl.program_id(1)
    ts, pitch = RG_TS, RG_PITCH
    rows = RG_NB * ts

    @pl.when(s == 0)
    def _():
        carry[...] = jnp.zeros_like(carry)

    hn = _rms_norm(x_ref[...].reshape(rows, D_MODEL), g_ref[...]).astype(BF16)
    c0, c1 = RNN_WIDTH, RNN_WIDTH + QKV_WIDTH

    def project(pair):
        width = MIX_TN * len(pair)
        both = jnp.dot(hn, w_ref[:, pair[0]:pair[0] + width], preferred_element_type=F32)
        for col in pair:
            route(col, both[:, col - pair[0]:col - pair[0] + MIX_TN])

    def route(col, out):
        if col < c0:
            for n in range(col // RNN_BLOCK, (col + MIX_TN) // RNN_BLOCK):
                lanes = slice(n * RNN_BLOCK - col, (n + 1) * RNN_BLOCK - col)
                for b in range(RG_NB):
                    xr_scr[n, b * pitch:b * pitch + ts, :] = out[b * ts:(b + 1) * ts, lanes]
        elif col < c1:
            for hd in range((col - c0) // HEAD_DIM, (col - c0 + MIX_TN) // HEAD_DIM):
                lanes = slice(c0 + hd * HEAD_DIM - col, c0 + (hd + 1) * HEAD_DIM - col)
                qkv_ref[:, hd, :, :] = out[:, lanes].reshape(RG_NB, ts, HEAD_DIM)
        else:
            gate_ref[:, :, col - c1:col - c1 + MIX_TN] = out.astype(BF16).reshape(RG_NB, ts, MIX_TN)

    def recur(n, ci):
        _rglru_chunk(n, ci * RG_TC, xr_scr, y_scr, carry, cw_ref, cb_ref, wax_ref, bax_ref, lam_ref)
        if ci == ts // RG_TC - 1:
            for b in range(RG_NB):
                y_ref[b, :, n * RNN_BLOCK:(n + 1) * RNN_BLOCK] = (
                    y_scr[n, b * pitch:b * pitch + ts, :].astype(BF16))

    slabs_per_block = MIX_TN // RNN_BLOCK
    units = [(n, ci) for n0 in range(0, RNN_BLOCKS, slabs_per_block) for ci in range(ts // RG_TC)
             for n in range(n0, n0 + slabs_per_block)]
    cols = list(range(0, IN_WIDTH, MIX_TN))
    pairs = [cols[j:j + 2] for j in range(0, len(cols), 2)]
    project(pairs[0])
    done = 1
    for k, (n, ci) in enumerate(units):
        recur(n, ci)
        want = 1 + ((k + 1) * (len(pairs) - 1) + len(units) - 1) // len(units)
        while done < want:
            project(pairs[done])
            done += 1


def _mix_in(x3, g, w_bf, conv_w, conv_b, wax, bax, lam):
    B, S, _ = x3.shape
    const2 = lambda b, s: (0, 0)
    tile = lambda width: pl.BlockSpec((RG_NB, RG_TS, width), lambda b, s: (b, s, 0))
    scan_scratch = pltpu.VMEM((RNN_BLOCKS, RG_NB * RG_PITCH, RNN_BLOCK), F32)
    return pl.pallas_call(
        _mix_in_kernel,
        grid=(B // RG_NB, S // RG_TS),
        in_specs=[
            tile(D_MODEL),
            pl.BlockSpec((1, D_MODEL), const2),
            pl.BlockSpec((D_MODEL, IN_WIDTH), const2, pipeline_mode=pl.Buffered(1)),
            pl.BlockSpec((RNN_CONV, RNN_WIDTH), const2),
            pl.BlockSpec((1, RNN_WIDTH), const2),
            pl.BlockSpec((RNN_BLOCKS, RNN_BLOCK, 2 * RNN_BLOCK), lambda b, s: (0, 0, 0)),
            pl.BlockSpec((2, RNN_WIDTH), const2),
            pl.BlockSpec((1, RNN_WIDTH), const2),
        ],
        out_specs=[
            tile(RNN_WIDTH),
            pl.BlockSpec((RG_NB, QKV_HEADS, RG_TS, HEAD_DIM), lambda b, s: (b, 0, s, 0)),
            tile(GATE_WIDTH),
        ],
        out_shape=[
            jax.ShapeDtypeStruct((B, S, RNN_WIDTH), BF16),
            jax.ShapeDtypeStruct((B, QKV_HEADS, S, HEAD_DIM), F32),
            jax.ShapeDtypeStruct((B, S, GATE_WIDTH), BF16),
        ],
        scratch_shapes=[
            scan_scratch, scan_scratch,
            pltpu.VMEM((RNN_BLOCKS, RNN_CONV, RG_NB, RNN_BLOCK), F32),
        ],
        compiler_params=pltpu.CompilerParams(
            dimension_semantics=("parallel", "arbitrary"), vmem_limit_bytes=VMEM_LIMIT_BYTES),
        name="mix_in",
    )(x3, g, w_bf, conv_w, conv_b, wax, bax, lam)


def _dilated_kernel(q1_ref, q2_ref, q3_ref, k_ref, v_ref, b1_ref, b2_ref, b3_ref, o_ref,
                    qp, kp, vp, tq, tk, tv, s_scr, p_scr, o_scr, m_scr, d_scr):
    S = k_ref.shape[0]
    blk = ATTN_BLOCK
    n_blocks = S // blk
    quarter = S // 4
    nt = (((1,), (1,)), ((), ()))

    def put(g, r0, n, q, k, v):
        qp[g, r0:r0 + n, :] = (q * QK_SCALE_LOG2).astype(BF16)
        kp[g, r0:r0 + n, :] = k.astype(BF16)
        vp[g, r0:r0 + n, 0:HEAD_DIM] = v.astype(BF16)

    @pl.when((pl.program_id(0) == 0) & (pl.program_id(1) == 0))
    def _():
        for g in range(N_GROUPS):
            vp[g, :, HEAD_DIM:2 * HEAD_DIM] = jnp.ones((S, HEAD_DIM), BF16)
    for c in range(4):
        r = slice(c * quarter, (c + 1) * quarter)
        put(0, c * quarter, quarter, q1_ref[r, :], k_ref[r, :], v_ref[r, :])
    for c in range(4):
        sr = pl.ds(c, quarter, stride=4)
        r = slice(c * quarter, (c + 1) * quarter)
        put(1, c * quarter, quarter, q2_ref[sr, :], k_ref[sr, :], v_ref[sr, :])
    last = DILATED_CONFIGS[2][1]
    for c in range(last):
        sr = pl.ds(c, blk, stride=last)
        put(2, c * blk, blk, q3_ref[sr, :], k_ref[sr, :], v_ref[sr, :])

    def run_group(g, b_ref, dil):
        nb = S // dil // blk

        def nat_rows(p):
            c, n = divmod(p, nb)
            if nb == 1:
                return slice(c * ATTN_PITCH, c * ATTN_PITCH + blk)
            start = n * (blk * dil) + c
            return pl.ds(start, blk) if dil == 1 else pl.ds(start, blk, stride=dil)

        def key_rows(p):
            has_prev = p % nb != 0
            return (slice((p - 1) * blk, (p + 1) * blk) if has_prev else slice(p * blk, (p + 1) * blk),
                    slice(0, 2 * blk) if has_prev else slice(blk, 2 * blk))

        for p in range(n_blocks):
            kr, cols = key_rows(p)
            s = lax.dot_general(qp[g, p * blk:(p + 1) * blk, :], kp[g, kr, :], nt, preferred_element_type=F32)
            s_scr[p, :, cols] = s + b_ref[:, cols]
        for p in range(n_blocks):
            _, cols = key_rows(p)
            m = jnp.max(s_scr[p, :, cols], axis=-1, keepdims=True)
            m_scr[g, nat_rows(p), :] = jnp.broadcast_to(m, (blk, HEAD_DIM))
            p_scr[p, :, cols] = jnp.exp2(s_scr[p, :, cols] - m).astype(BF16)
        for p in range(n_blocks):
            kr, cols = key_rows(p)
            od = jnp.dot(p_scr[p, :, cols], vp[g, kr, :], preferred_element_type=F32)
            o_scr[g, nat_rows(p), :] = od[:, 0:HEAD_DIM]
            d_scr[g, nat_rows(p), :] = od[:, HEAD_DIM:2 * HEAD_DIM]

    run_group(0, b1_ref, DILATED_CONFIGS[0][1])
    run_group(1, b2_ref, DILATED_CONFIGS[1][1])
    run_group(2, b3_ref, DILATED_CONFIGS[2][1])

    chunk = 256

    last_dil = DILATED_CONFIGS[2][1]

    def widest(scr, i):
        per = chunk // last_dil
        return jnp.concatenate(
            [scr[2, pl.ds(i * per + j, last_dil, stride=ATTN_PITCH), :] for j in range(per)], axis=0)

    def merge(i, carry):
        r = pl.ds(pl.multiple_of(i * chunk, chunk), chunk)
        m0, m1, m2 = m_scr[0, r, :], m_scr[1, r, :], widest(m_scr, i)
        mx = jnp.maximum(jnp.maximum(m0, m1), m2)
        w0, w1, w2 = jnp.exp2(m0 - mx), jnp.exp2(m1 - mx), jnp.exp2(m2 - mx)
        num = w0 * o_scr[0, r, :] + w1 * o_scr[1, r, :] + w2 * widest(o_scr, i)
        z = w0 * d_scr[0, r, :] + w1 * d_scr[1, r, :] + w2 * widest(d_scr, i)
        o_ref[r, :] = (num / z).astype(o_ref.dtype)
        return carry

    lax.fori_loop(0, S // chunk, merge, 0)


def _dilated_attention(qkv3, bias_tab):
    B, _, S, _ = qkv3.shape
    head_spec = lambda col0: pl.BlockSpec((None, None, S, HEAD_DIM), lambda b, h: (b, col0 + h, 0, 0))
    bias_spec = lambda g: pl.BlockSpec(
        (None, ATTN_BLOCK, 2 * ATTN_BLOCK), lambda b, h: (g * KV_HEADS + h, 0, 0))
    return pl.pallas_call(
        _dilated_kernel,
        grid=(B, KV_HEADS),
        in_specs=[
            head_spec(0), head_spec(KV_HEADS), head_spec(2 * KV_HEADS),
            head_spec(Q_HEADS), head_spec(Q_HEADS + KV_HEADS),
            bias_spec(0), bias_spec(1), bias_spec(2),
        ],
        out_specs=pl.BlockSpec((None, S, HEAD_DIM), lambda b, h: (b, 0, h)),
        out_shape=jax.ShapeDtypeStruct((B, S, KV_HEADS * HEAD_DIM), BF16),
        scratch_shapes=[
            pltpu.VMEM((N_GROUPS, S, HEAD_DIM), BF16),
            pltpu.VMEM((N_GROUPS, S, HEAD_DIM), BF16),
            pltpu.VMEM((N_GROUPS, S, 2 * HEAD_DIM), BF16),
            pltpu.VMEM((S, HEAD_DIM), F32),
            pltpu.VMEM((S, HEAD_DIM), F32),
            pltpu.VMEM((S, HEAD_DIM), F32),
            pltpu.VMEM((S // ATTN_BLOCK, ATTN_BLOCK, 2 * ATTN_BLOCK), F32),
            pltpu.VMEM((S // ATTN_BLOCK, ATTN_BLOCK, 2 * ATTN_BLOCK), BF16),
            pltpu.VMEM((N_GROUPS, S // ATTN_BLOCK * ATTN_PITCH, HEAD_DIM), F32),
            pltpu.VMEM((N_GROUPS, S // ATTN_BLOCK * ATTN_PITCH, HEAD_DIM), F32),
            pltpu.VMEM((N_GROUPS, S // ATTN_BLOCK * ATTN_PITCH, HEAD_DIM), F32),
        ],
        compiler_params=pltpu.CompilerParams(
            dimension_semantics=("arbitrary", "arbitrary"), vmem_limit_bytes=VMEM_LIMIT_BYTES),
        name="dilated_attn",
    )(qkv3, qkv3, qkv3, qkv3, qkv3, bias_tab, bias_tab, bias_tab)


def _gelu_tanh(x):
    return 0.5 * x * (1.0 + jnp.tanh(math.sqrt(2.0 / math.pi) * (x + 0.044715 * (x * x * x))))


def _mix_out_kernel(y_ref, o_ref, mg_ref, x_ref, wbr_ref, wba_ref, wout_ref, gmix_ref,
                    gpre_ref, wgu_ref, cw_ref, cb_ref, wd_ref, gpost_ref, out_ref,
                    mrg_scr, mix_scr, h_scr, hn_scr, gate_scr, up_scr, act_scr, ff_scr, *, tiles_per_seq):
    i = pl.program_id(0)
    sr, n_sub = FF_SR, FF_TM // FF_SR
    n_c = FFN_WIDTH // FF_TC

    @pl.when(i % tiles_per_seq == 0)
    def _():
        gate_scr[0, 0:SUBLANES, :] = jnp.zeros((SUBLANES, FFN_WIDTH), F32)

    def rows(r):
        return slice(r * sr, (r + 1) * sr)

    def stage_a(r, c):
        cols = slice(c * FF_DN, (c + 1) * FF_DN)
        br = jnp.dot(y_ref[rows(r), :], wbr_ref[:, cols], preferred_element_type=F32)
        ba = jnp.dot(o_ref[rows(r), :], wba_ref[:, cols], preferred_element_type=F32)
        g_rnn = _sigmoid(mg_ref[rows(r), c * FF_DN:(c + 1) * FF_DN].astype(F32))
        g_att = _sigmoid(mg_ref[rows(r), D_MODEL + c * FF_DN:D_MODEL + (c + 1) * FF_DN].astype(F32))
        mrg_scr[r % 2, :, cols] = (g_rnn * br + g_att * ba).astype(BF16)

    def stage_c(r, c):
        cols = slice(c * FF_DN, (c + 1) * FF_DN)
        mix_scr[r % 2, :, cols] = jnp.dot(mrg_scr[r % 2], wout_ref[:, cols], preferred_element_type=F32)

    def stage_n(r):
        h = x_ref[rows(r), :] + _rms_norm(mix_scr[r % 2], gmix_ref[...])
        h_scr[r] = h
        hn_scr[r % 2] = _rms_norm(h, gpre_ref[...]).astype(BF16)

    def stage_g(r, c):
        cols = slice(c * FF_TC, (c + 1) * FF_TC)
        res = jnp.dot(hn_scr[r % 2], wgu_ref[:, 2 * c * FF_TC:2 * (c + 1) * FF_TC], preferred_element_type=F32)
        gate = res[:, 0:FF_TC]
        gate_scr[r % 2, SUBLANES:SUBLANES + sr, cols] = gate
        gate_scr[(r + 1) % 2, 0:SUBLANES, cols] = gate[sr - SUBLANES:sr, :]
        up_scr[r % 2, :, cols] = res[:, FF_TC:2 * FF_TC]

    def stage_v(r, c):
        cols = slice(c * FF_TC, (c + 1) * FF_TC)
        conv = cb_ref[:, cols] + cw_ref[FFN_CONV - 1:FFN_CONV, cols] * gate_scr[r % 2, SUBLANES:SUBLANES + sr, cols]
        for k in range(FFN_CONV - 1):
            back = FFN_CONV - 1 - k
            conv = conv + cw_ref[k:k + 1, cols] * gate_scr[r % 2, SUBLANES - back:SUBLANES - back + sr, cols]
        act_scr[r % 2, :, cols] = (_gelu_tanh(conv) * up_scr[r % 2, :, cols]).astype(BF16)

    def stage_d(r, c):
        cols = slice(c * FF_DN, (c + 1) * FF_DN)
        ff_scr[r % 2, :, cols] = jnp.dot(act_scr[r % 2], wd_ref[:, cols], preferred_element_type=F32)

    def stage_e(r):
        out_ref[rows(r), :] = h_scr[r] + _rms_norm(ff_scr[r % 2], gpost_ref[...])

    n_d = D_MODEL // FF_DN
    live = lambda r: 0 <= r < n_sub
    for t in range(n_sub + 6):
        for c in range(n_d):
            if live(t):
                stage_a(t, c)
            if live(t - 1):
                stage_c(t - 1, c)
        if live(t - 2):
            stage_n(t - 2)
        for c in range(n_c):
            if live(t - 4):
                stage_v(t - 4, c)
            if live(t - 3):
                stage_g(t - 3, c)
            if live(t - 5):
                for d in range(c * n_d // n_c, (c + 1) * n_d // n_c):
                    stage_d(t - 5, d)
        if live(t - 6):
            stage_e(t - 6)


def _pack_gate_up_kernel(wg_ref, wu_ref, out_ref):
    out_ref[:, 0:FF_TC] = wg_ref[...].astype(BF16)
    out_ref[:, FF_TC:2 * FF_TC] = wu_ref[...].astype(BF16)


def _pack_gate_up(wg, wu):
    block = pl.BlockSpec((D_MODEL, FF_TC), lambda c: (0, c))
    return pl.pallas_call(
        _pack_gate_up_kernel,
        grid=(FFN_WIDTH // FF_TC,),
        in_specs=[block, block],
        out_specs=pl.BlockSpec((D_MODEL, 2 * FF_TC), lambda c: (0, c)),
        out_shape=jax.ShapeDtypeStruct((D_MODEL, 2 * FFN_WIDTH), BF16),
        name="pack_gate_up",
    )(wg, wu)


def _mix_out(y2, o2, mgates, x2, wbr, wba, wout, g_mix_post, g_pre, wgu, conv_w, conv_b, wd, g_post, seq_len):
    T = x2.shape[0]
    const = lambda i: (0, 0)
    row = lambda width: pl.BlockSpec((FF_TM, width), lambda i: (i, 0))
    resident = lambda shape: pl.BlockSpec(shape, const, pipeline_mode=pl.Buffered(1))
    return pl.pallas_call(
        functools.partial(_mix_out_kernel, tiles_per_seq=seq_len // FF_TM),
        grid=(T // FF_TM,),
        in_specs=[
            row(RNN_WIDTH), row(KV_HEADS * HEAD_DIM), row(GATE_WIDTH), row(D_MODEL),
            resident((RNN_WIDTH, D_MODEL)),
            resident((KV_HEADS * HEAD_DIM, D_MODEL)),
            resident((D_MODEL, D_MODEL)),
            pl.BlockSpec((1, D_MODEL), const),
            pl.BlockSpec((1, D_MODEL), const),
            resident((D_MODEL, 2 * FFN_WIDTH)),
            pl.BlockSpec((FFN_CONV, FFN_WIDTH), const),
            pl.BlockSpec((1, FFN_WIDTH), const),
            resident((FFN_WIDTH, D_MODEL)),
            pl.BlockSpec((1, D_MODEL), const),
        ],
        out_specs=row(D_MODEL),
        out_shape=jax.ShapeDtypeStruct((T, D_MODEL), F32),
        scratch_shapes=[
            pltpu.VMEM((2, FF_SR, D_MODEL), BF16),
            pltpu.VMEM((2, FF_SR, D_MODEL), F32),
            pltpu.VMEM((FF_TM // FF_SR, FF_SR, D_MODEL), F32),
            pltpu.VMEM((2, FF_SR, D_MODEL), BF16),
            pltpu.VMEM((2, SUBLANES + FF_SR, FFN_WIDTH), F32),
            pltpu.VMEM((2, FF_SR, FFN_WIDTH), F32),
            pltpu.VMEM((2, FF_SR, FFN_WIDTH), BF16),
            pltpu.VMEM((2, FF_SR, D_MODEL), F32),
        ],
        compiler_params=pltpu.CompilerParams(
            dimension_semantics=("arbitrary",), vmem_limit_bytes=VMEM_LIMIT_BYTES),
        name="mix_out",
    )(y2, o2, mgates, x2, wbr, wba, wout, g_mix_post, g_pre, wgu, conv_w, conv_b, wd, g_post)


def kernel(x, rel_bias, norm_mix_pre, norm_mix_post, w_in, conv_rnn_w, conv_rnn_b, w_rg_a, b_rg_a,
           w_rg_x, b_rg_x, lru_lambda, w_branch_rnn, w_branch_att, w_out, norm_ffn_pre, norm_ffn_post,
           w_ffn_gate, w_ffn_up, conv_ffn_w, conv_ffn_b, w_ffn_down):
    B, S, D = x.shape
    assert D == D_MODEL and S % FF_TM == 0 and S % RG_TS == 0 and B % RG_NB == 0
    assert (FF_TM // FF_SR) % 2 == 0, "the conv halo hand-off between grid steps relies on an even sub-tile count"
    assert S == DILATED_CONFIGS[-1][0], "attention block structure assumes window == sequence for the widest group"
    depth = w_in.shape[0]
    T = B * S
    bias_tab = _bias_tables(rel_bias)
    h = x.reshape(T, D)
    for l in range(depth):
        wax = jnp.concatenate([w_rg_a[l], w_rg_x[l]], axis=-1).astype(BF16)
        bax = jnp.stack([b_rg_a[l], b_rg_x[l]], axis=0)
        y_rnn, qkv, gates = _mix_in(h.reshape(B, S, D), norm_mix_pre[l][None], w_in[l].astype(BF16),
                                    conv_rnn_w[l], conv_rnn_b[l][None], wax, bax, lru_lambda[l][None])
        o_att = _dilated_attention(qkv, bias_tab)
        wgu = _pack_gate_up(w_ffn_gate[l], w_ffn_up[l])
        h = _mix_out(y_rnn.reshape(T, RNN_WIDTH), o_att.reshape(T, KV_HEADS * HEAD_DIM),
                     gates.reshape(T, GATE_WIDTH), h,
                     w_branch_rnn[l].astype(BF16), w_branch_att[l].astype(BF16), w_out[l].astype(BF16),
                     norm_mix_post[l][None], norm_ffn_pre[l][None],
                     wgu, conv_ffn_w[l], conv_ffn_b[l][None],
                     w_ffn_down[l].astype(BF16), norm_ffn_post[l][None], S)
    return h.reshape(B, S, D)
```
